```python
import math
import jax, jax.numpy as jnp
from jax import lax
import numpy as np

D_MODEL = 4096
BATCH = 1
SEQ = 16384
DEPTH = 2

GDN_DIM = 128
GDN_WIDTH = 3 * D_MODEL // 8
GDN_HEADS = GDN_WIDTH // GDN_DIM
GDN_CONV = 4
GDN_CHUNK = 64
SWA_DIM = 128
SWA_WIDTH = D_MODEL // 4
SWA_HEADS = SWA_WIDTH // SWA_DIM
SWA_PATTERNS = ((128, 1), (512, 4), (2048, 16))
NUM_BUCKETS = 32
MAX_DISTANCE = 2048
RWKV_DIM = 64
RWKV_WIDTH = D_MODEL - GDN_WIDTH - SWA_WIDTH
RWKV_HEADS = RWKV_WIDTH // RWKV_DIM
DECAY_LORA = 128
ICL_LORA = 128
GATE_LORA = 480
GDN_IN = 4 * GDN_WIDTH + 2 * GDN_HEADS
SWA_IN = 3 * SWA_WIDTH
RWKV_IN = 3 * RWKV_WIDTH + DECAY_LORA + ICL_LORA + GATE_LORA
IN_WIDTH = GDN_IN + SWA_IN + RWKV_IN
D_FF = 11008
FFN_CONV = 3
RMS_EPS = 1e-6
GN_EPS = 64e-5
NEG_INF = -1e30

kernel_name = "hybrid_gdn_dilated_rwkv7_convffn"


def rmsnorm(t, w):
    tf = t.astype(jnp.float32)
    y = tf * lax.rsqrt(jnp.mean(tf * tf, axis=-1, keepdims=True) + RMS_EPS) * w.astype(jnp.float32)
    return y.astype(t.dtype)


def l2norm(t, eps=1e-6):
    return t * lax.rsqrt(jnp.sum(t * t, axis=-1, keepdims=True) + eps)


def causal_dwconv(t, w):
    width = w.shape[0]
    seq = t.shape[1]
    tp = jnp.pad(t, ((0, 0), (width - 1, 0), (0, 0)))
    out = tp[:, 0:seq] * w[0]
    for i in range(1, width):
        out = out + tp[:, i:i + seq] * w[i]
    return out


def token_shift(t):
    return jnp.pad(t, ((0, 0), (1, 0), (0, 0)))[:, :-1]


def chunk_gated_delta_rule(q, k, v, g, beta):
    B, S, H, D = q.shape
    C = GDN_CHUNK
    N = S // C

    def chunked(t):
        return jnp.moveaxis(t.reshape(B, N, C, H, *t.shape[3:]), 3, 2)

    q, k, v = chunked(q), chunked(k), chunked(v)
    g, beta = chunked(g), chunked(beta)
    G = jnp.cumsum(g, axis=-1)
    causal = jnp.tril(jnp.ones((C, C), bool))
    strict = jnp.tril(jnp.ones((C, C), bool), -1)
    gamma = jnp.exp(jnp.where(causal, G[..., :, None] - G[..., None, :], -jnp.inf))
    kb = k * beta[..., None]
    a_mat = jnp.where(strict, jnp.einsum('bnhid,bnhjd->bnhij', kb, k) * gamma, 0.0)
    eye = jnp.eye(C, dtype=jnp.float32)
    rhs = jnp.concatenate([v * beta[..., None], kb * jnp.exp(G)[..., None]], axis=-1)
    sol = lax.linalg.triangular_solve(eye + a_mat, rhs, left_side=True, lower=True, unit_diagonal=True)
    u, w = jnp.split(sol, 2, axis=-1)
    attn = jnp.einsum('bnhid,bnhjd->bnhij', q, k) * gamma
    qg = q * jnp.exp(G)[..., None]
    kd = k * jnp.exp(G[..., -1:] - G)[..., None]
    g_tot = jnp.exp(G[..., -1])

    def step(state, xs):
        u_c, w_c, qg_c, kd_c, attn_c, gt_c = xs
        v_new = u_c - jnp.einsum('bhck,bhkv->bhcv', w_c, state)
        o = jnp.einsum('bhck,bhkv->bhcv', qg_c, state) + jnp.einsum('bhij,bhjv->bhiv', attn_c, v_new)
        state = state * gt_c[..., None, None] + jnp.einsum('bhck,bhcv->bhkv', kd_c, v_new)
        return state, o

    xs = tuple(jnp.moveaxis(t, 1, 0) for t in (u, w, qg, kd, attn, g_tot))
    state0 = jnp.zeros((B, H, D, D), jnp.float32)
    _, o = lax.scan(step, state0, xs)
    return jnp.moveaxis(o, 0, 1).swapaxes(2, 3).reshape(B, S, H, D)


def gated_deltanet(p, conv_w, a_log, dt_bias, norm_w):
    B, S, _ = p.shape
    H, Dh, W = GDN_HEADS, GDN_DIM, GDN_WIDTH
    qkv, z, b, a = jnp.split(p, [3 * W, 4 * W, 4 * W + H], axis=-1)
    qkv = jax.nn.silu(causal_dwconv(qkv, conv_w)).astype(jnp.float32)
    q, k, v = [t.reshape(B, S, H, Dh) for t in jnp.split(qkv, 3, axis=-1)]
    q = l2norm(q) * (Dh ** -0.5)
    k = l2norm(k)
    beta = jax.nn.sigmoid(b.astype(jnp.float32))
    g = -jnp.exp(a_log.astype(jnp.float32)) * jax.nn.softplus(a.astype(jnp.float32) + dt_bias.astype(jnp.float32))
    o = chunk_gated_delta_rule(q, k, v, g, beta)
    o = o * lax.rsqrt(jnp.mean(o * o, axis=-1, keepdims=True) + RMS_EPS) * norm_w.astype(jnp.float32)
    o = o * jax.nn.silu(z.astype(jnp.float32).reshape(B, S, H, Dh))
    return o.reshape(B, S, W).astype(p.dtype)


def t5_bucket(dist):
    exact = NUM_BUCKETS // 2
    d = jnp.maximum(dist, 1).astype(jnp.float32)
    log_b = exact + (jnp.log(d / exact) / math.log(MAX_DISTANCE / exact) * (NUM_BUCKETS - exact)).astype(jnp.int32)
    return jnp.where(dist < exact, dist, jnp.minimum(log_b, NUM_BUCKETS - 1))


def dilated_window_branch(q, k, v, rel_bias, window, dilation):
    B, S, H, D = q.shape
    blk = window // dilation
    span = blk * dilation
    s_pad = -(-S // span) * span
    n_sub = s_pad // dilation
    nb = n_sub // blk

    def to_blocks(t):
        t = jnp.pad(t, ((0, 0), (0, s_pad - S), (0, 0), (0, 0)))
        t = t.reshape(B, n_sub, dilation, H, D).transpose(0, 2, 1, 3, 4)
        return t.reshape(B, dilation, nb, blk, H, D)

    def with_prev(t):
        prev = jnp.pad(t, ((0, 0), (0, 0), (1, 0), (0, 0), (0, 0), (0, 0)))[:, :, :-1]
        return jnp.concatenate([prev, t], axis=3)

    qb = to_blocks(q)
    kw = with_prev(to_blocks(k))
    vw = with_prev(to_blocks(v))
    s = jnp.einsum('brnqhd,brnkhd->brnhqk', qb, kw)
    i = jnp.arange(blk)[:, None]
    j = jnp.arange(2 * blk)[None, :]
    steps = i + blk - j
    in_band = (steps >= 0) & (steps <= blk)
    first = (jnp.arange(nb) == 0)[:, None, None]
    valid = in_band[None] & ~(first & (j < blk)[None])
    bias = rel_bias[t5_bucket(jnp.maximum(steps, 0) * dilation)].transpose(2, 0, 1)
    s = jnp.where(valid[None, None, :, None], s + bias, NEG_INF)
    m = jnp.max(s, axis=-1)
    p = jnp.exp(s - m[..., None])
    l = jnp.sum(p, axis=-1)
    o = jnp.einsum('brnhqk,brnkhd->brnqhd', p, vw)

    def from_blocks(t):
        t = t.reshape(B, dilation, n_sub, *t.shape[4:]).swapaxes(1, 2)
        return t.reshape(B, s_pad, *t.shape[3:])[:, :S]

    return from_blocks(o), from_blocks(m.swapaxes(3, 4)), from_blocks(l.swapaxes(3, 4))


def dilated_attention_mixture(p, rel_bias):
    B, S, _ = p.shape
    q, k, v = [t.reshape(B, S, SWA_HEADS, SWA_DIM).astype(jnp.float32) for t in jnp.split(p, 3, axis=-1)]
    q = q * (SWA_DIM ** -0.5)
    rb = rel_bias.astype(jnp.float32)
    branches = [dilated_window_branch(q, k, v, rb, w, d) for (w, d) in SWA_PATTERNS]
    m_max = jnp.max(jnp.stack([br[1] for br in branches]), axis=0)
    num = jnp.zeros_like(q)
    den = jnp.zeros_like(m_max)
    for o, m, l in branches:
        sc = jnp.exp(m - m_max)
        num = num + o * sc[..., None]
        den = den + l * sc
    return (num / den[..., None]).reshape(B, S, SWA_WIDTH).astype(p.dtype)


def rwkv7_recurrence(r, w, k, v, a, b):
    B, S, H, N = r.shape

    def step(state, xs):
        r_t, w_t, k_t, v_t, a_t, b_t = xs
        sa = jnp.einsum('bhvk,bhk->bhv', state, a_t)
        state = state * w_t[:, :, None, :] + sa[..., None] * b_t[:, :, None, :] + v_t[..., None] * k_t[:, :, None, :]
        return state, jnp.einsum('bhvk,bhk->bhv', state, r_t)

    xs = tuple(jnp.moveaxis(t, 1, 0) for t in (r, w, k, v, a, b))
    _, y = lax.scan(step, jnp.zeros((B, H, N, N), jnp.float32), xs)
    return jnp.moveaxis(y, 0, 1)


def rwkv7_time_mix(p, mu, w0, w_up, a0, a_up, g_up, k_k, k_a, r_k, ln_w, ln_b):
    B, S, _ = p.shape
    H, N, W = RWKV_HEADS, RWKV_DIM, RWKV_WIDTH
    p = (p + mu * (token_shift(p) - p)).astype(jnp.float32)
    r, k, v, wd, ad, gd = jnp.split(p, [W, 2 * W, 3 * W, 3 * W + DECAY_LORA, 3 * W + DECAY_LORA + ICL_LORA], axis=-1)
    f = lambda t: t.astype(jnp.float32)
    heads = lambda t: t.reshape(B, S, H, N)
    w_log = -jax.nn.softplus(-(f(w0) + jnp.tanh(wd) @ f(w_up))) - 0.5
    decay = jnp.exp(-jnp.exp(w_log))
    a = jax.nn.sigmoid(f(a0) + ad @ f(a_up))
    g = jax.nn.sigmoid(gd) @ f(g_up)
    kk = l2norm(heads(k * f(k_k)), eps=1e-12)
    k = k * (1.0 + (a - 1.0) * f(k_a))
    y = rwkv7_recurrence(heads(r), heads(decay), heads(k), heads(v), -kk, kk * heads(a))
    mean = jnp.mean(y, axis=-1, keepdims=True)
    var = jnp.mean(jnp.square(y - mean), axis=-1, keepdims=True)
    y = ((y - mean) * lax.rsqrt(var + GN_EPS)).reshape(B, S, W) * f(ln_w) + f(ln_b)
    bonus = jnp.sum(heads(r) * heads(k) * f(r_k), axis=-1, keepdims=True) * heads(v)
    return ((y + bonus.reshape(B, S, W)) * g).astype(mu.dtype)


def conv_gated_ffn(h, w_gate, w_up, conv_w, conv_b, w_down):
    gate = causal_dwconv(h @ w_gate, conv_w) + conv_b
    return (jax.nn.silu(gate) * (h @ w_up)) @ w_down


def setup_inputs(seed: int = 0) -> dict:
    key = jax.random.key(seed)
    ks = jax.random.split(key, 32)
    L = DEPTH
    nrm = lambda kk, shape, scale: scale * jax.random.normal(kk, shape, jnp.float32)
    uni = lambda kk, shape, lo, hi: jax.random.uniform(kk, shape, jnp.float32, lo, hi)
    dt = jnp.exp(uni(ks[5], (L, GDN_HEADS), math.log(1e-3), math.log(1e-1)))
    return {
        "x": nrm(ks[0], (BATCH, SEQ, D_MODEL), 1.0),
        "attn_norm": 1.0 + nrm(ks[1], (L, D_MODEL), 0.02),
        "w_in": nrm(ks[2], (L, D_MODEL, IN_WIDTH), D_MODEL ** -0.5),
        "gdn_conv": nrm(ks[3], (L, GDN_CONV, 3 * GDN_WIDTH), GDN_CONV ** -0.5),
        "gdn_a_log": jnp.log(uni(ks[4], (L, GDN_HEADS), 1.0, 16.0)),
        "gdn_dt_bias": dt + jnp.log(-jnp.expm1(-dt)),
        "gdn_norm": 1.0 + nrm(ks[6], (L, GDN_DIM), 0.02),
        "rwkv_mu": uni(ks[7], (L, RWKV_IN), 0.0, 1.0),
        "rwkv_w0": uni(ks[8], (L, RWKV_WIDTH), -6.0, 1.0),
        "rwkv_w_up": nrm(ks[9], (L, DECAY_LORA, RWKV_WIDTH), 0.5 * DECAY_LORA ** -0.5),
        "rwkv_a0": nrm(ks[10], (L, RWKV_WIDTH), 0.1),
        "rwkv_a_up": nrm(ks[11], (L, ICL_LORA, RWKV_WIDTH), ICL_LORA ** -0.5),
        "rwkv_g_up": nrm(ks[12], (L, GATE_LORA, RWKV_WIDTH), GATE_LORA ** -0.5),
        "rwkv_k_k": 0.85 + nrm(ks[13], (L, RWKV_WIDTH), 0.02),
        "rwkv_k_a": 1.0 + nrm(ks[14], (L, RWKV_WIDTH), 0.02),
        "rwkv_r_k": nrm(ks[15], (L, RWKV_HEADS, RWKV_DIM), 0.1),
        "rwkv_ln_w": 1.0 + nrm(ks[16], (L, RWKV_WIDTH), 0.02),
        "rwkv_ln_b": nrm(ks[17], (L, RWKV_WIDTH), 0.01),
        "w_out": nrm(ks[18], (L, D_MODEL, D_MODEL), D_MODEL ** -0.5),
        "ffn_norm": 1.0 + nrm(ks[19], (L, D_MODEL), 0.02),
        "w_ffn_gate": nrm(ks[20], (L, D_MODEL, D_FF), D_MODEL ** -0.5),
        "w_ffn_up": nrm(ks[21], (L, D_MODEL, D_FF), D_MODEL ** -0.5),
        "ffn_conv": nrm(ks[22], (L, FFN_CONV, D_FF), FFN_CONV ** -0.5),
        "ffn_conv_b": nrm(ks[23], (L, D_FF), 0.01),
        "w_ffn_down": nrm(ks[24], (L, D_FF, D_MODEL), D_FF ** -0.5),
        "rel_bias": nrm(ks[25], (NUM_BUCKETS, SWA_HEADS), 0.5),
        "final_norm": 1.0 + nrm(ks[26], (D_MODEL,), 0.02),
    }


def reference(x, attn_norm, w_in, gdn_conv, gdn_a_log, gdn_dt_bias, gdn_norm, rwkv_mu, rwkv_w0, rwkv_w_up,
              rwkv_a0, rwkv_a_up, rwkv_g_up, rwkv_k_k, rwkv_k_a, rwkv_r_k, rwkv_ln_w, rwkv_ln_b, w_out,
              ffn_norm, w_ffn_gate, w_ffn_up, ffn_conv, ffn_conv_b, w_ffn_down, rel_bias, final_norm):
    for l in range(DEPTH):
        h = rmsnorm(x, attn_norm[l])
        proj = h @ w_in[l]
        p_a, p_b, p_c = jnp.split(proj, [GDN_IN, GDN_IN + SWA_IN], axis=-1)
        o_a = gated_deltanet(p_a, gdn_conv[l], gdn_a_log[l], gdn_dt_bias[l], gdn_norm[l])
        o_b = dilated_attention_mixture(p_b, rel_bias)
        o_c = rwkv7_time_mix(p_c, rwkv_mu[l], rwkv_w0[l], rwkv_w_up[l], rwkv_a0[l], rwkv_a_up[l], rwkv_g_up[l],
                             rwkv_k_k[l], rwkv_k_a[l], rwkv_r_k[l], rwkv_ln_w[l], rwkv_ln_b[l])
        mix = jnp.concatenate([o_a, o_b.astype(o_a.dtype), o_c.astype(o_a.dtype)], axis=-1)
        x = x + (mix @ w_out[l]).astype(x.dtype)
        h = rmsnorm(x, ffn_norm[l])
        x = x + conv_gated_ffn(h, w_ffn_gate[l], w_ffn_up[l], ffn_conv[l], ffn_conv_b[l], w_ffn_down[l]).astype(x.dtype)
    return rmsnorm(x, final_norm)
```

```python
import functools
import math

import numpy as np
import jax
import jax.numpy as jnp
from jax import lax
from jax.experimental import pallas as pl
from jax.experimental.pallas import tpu as pltpu

F32 = jnp.float32
BF16 = jnp.bfloat16
HIGHEST = lax.Precision.HIGHEST

LANE = 128
D_MODEL = 4096
RMS_EPS = 1e-6
GN_EPS = 64e-5
NEG_INF = -1e30
VMEM_LIMIT = 56 * 1024 * 1024

GDN_DIM = 128
GDN_HEADS = 12
GDN_WIDTH = GDN_HEADS * GDN_DIM
GDN_CONV = 4
GDN_CHUNK = 64
SWA_DIM = 128
SWA_HEADS = 8
SWA_WIDTH = SWA_HEADS * SWA_DIM
SWA_PATTERNS = ((128, 1), (512, 4), (2048, 16))
SWA_BLK = 128
SWA_TILE = 2048
NUM_BUCKETS = 32
MAX_DISTANCE = 2048
RWKV_DIM = 64
RWKV_HEADS = 24
RWKV_WIDTH = RWKV_HEADS * RWKV_DIM
RWKV_CHUNK = 64
DECAY_LORA = 128
ICL_LORA = 128
GATE_LORA = 480
GATE_LORA_PAD = 512
D_FF = 11008
FFN_CONV = 3

COL_GDN_Q, COL_GDN_K, COL_GDN_V, COL_GDN_Z, COL_GDN_BA = 0, 12, 24, 36, 48
COL_SWA_Q, COL_SWA_K, COL_SWA_V = 49, 57, 65
COL_WD, COL_AD = 73, 74
COL_GD = 76
COL_R, COL_K, COL_V = 80, 92, 104
PROJ_BLOCKS = 116
PROJ_WIDTH = PROJ_BLOCKS * LANE


def _cparams(sem):
    return pltpu.CompilerParams(dimension_semantics=sem, vmem_limit_bytes=VMEM_LIMIT)


def _silu(x):
    return x * (1.0 / (1.0 + jnp.exp(-x)))


def _sigmoid(x):
    return 1.0 / (1.0 + jnp.exp(-x))


def _softplus(x):
    return jnp.maximum(x, 0.0) + jnp.log(1.0 + jnp.exp(-jnp.abs(x)))


def _dot(a, b, dims=(((1,), (0,)), ((), ())), precise=False):
    if precise:
        return lax.dot_general(a.astype(F32), b.astype(F32), dims, precision=HIGHEST,
                               preferred_element_type=F32)
    return lax.dot_general(a.astype(BF16), b.astype(BF16), dims, preferred_element_type=F32)


_NT = (((1,), (1,)), ((), ()))
_TN = (((0,), (0,)), ((), ()))


def _rmsnorm_body(x_ref, w_ref, o_ref):
    x = x_ref[...]
    ms = jnp.mean(x * x, axis=-1, keepdims=True)
    o_ref[...] = (x * lax.rsqrt(ms + RMS_EPS) * w_ref[...]).astype(o_ref.dtype)


def _rmsnorm(x, w, out_dtype, tm=512):
    s, d = x.shape
    return pl.pallas_call(
        _rmsnorm_body,
        grid=(s // tm,),
        in_specs=[pl.BlockSpec((tm, d), lambda i: (i, 0)),
                  pl.BlockSpec((1, d), lambda i: (0, 0))],
        out_specs=pl.BlockSpec((tm, d), lambda i: (i, 0)),
        out_shape=jax.ShapeDtypeStruct((s, d), out_dtype),
        compiler_params=_cparams(("parallel",)),
        name="rmsnorm",
    )(x, w.reshape(1, d))


def _matmul_body(a_ref, b_ref, o_ref):
    o_ref[...] = jnp.dot(a_ref[...], b_ref[...], preferred_element_type=F32).astype(o_ref.dtype)


def _in_proj(h, w, tm=1024, tn=512):
    s, k = h.shape
    n = w.shape[1]
    return pl.pallas_call(
        _matmul_body,
        grid=(s // tm, n // tn),
        in_specs=[pl.BlockSpec((tm, k), lambda i, j: (i, 0)),
                  pl.BlockSpec((k, tn), lambda i, j: (0, j))],
        out_specs=pl.BlockSpec((tm, tn), lambda i, j: (i, j)),
        out_shape=jax.ShapeDtypeStruct((s, n), F32),
        compiler_params=_cparams(("parallel", "arbitrary")),
        name="in_proj",
    )(h, w)


def _out_proj_body(x_ref, a_ref, b_ref, c_ref, wa_ref, wb_ref, wc_ref, o_ref):
    acc = jnp.dot(a_ref[...], wa_ref[...], preferred_element_type=F32)
    acc += jnp.dot(b_ref[...], wb_ref[...], preferred_element_type=F32)
    acc += jnp.dot(c_ref[...], wc_ref[...], preferred_element_type=F32)
    o_ref[...] = x_ref[...] + acc


def _out_proj(x, oa, ob, oc, wa, wb, wc, tm=1024, tn=512):
    s, d = x.shape
    ka, kb, kc = oa.shape[1], ob.shape[1], oc.shape[1]
    return pl.pallas_call(
        _out_proj_body,
        grid=(s // tm, d // tn),
        in_specs=[pl.BlockSpec((tm, tn), lambda i, j: (i, j)),
                  pl.BlockSpec((tm, ka), lambda i, j: (i, 0)),
                  pl.BlockSpec((tm, kb), lambda i, j: (i, 0)),
                  pl.BlockSpec((tm, kc), lambda i, j: (i, 0)),
                  pl.BlockSpec((ka, tn), lambda i, j: (0, j)),
                  pl.BlockSpec((kb, tn), lambda i, j: (0, j)),
                  pl.BlockSpec((kc, tn), lambda i, j: (0, j))],
        out_specs=pl.BlockSpec((tm, tn), lambda i, j: (i, j)),
        out_shape=jax.ShapeDtypeStruct((s, d), F32),
        compiler_params=_cparams(("parallel", "arbitrary")),
        name="out_proj",
    )(x, oa, ob, oc, wa, wb, wc)


FFN_HALO = 16


def _ffn_up_body(h_ref, halo_ref, wg_ref, wu_ref, cw_ref, cb_ref, o_ref, hbuf, *, tm):
    i = pl.program_id(0)

    @pl.when(pl.program_id(1) == 0)
    def _stage():
        halo = halo_ref[...]
        hbuf[0:FFN_HALO, :] = jnp.where(i == 0, jnp.zeros_like(halo), halo)
        hbuf[FFN_HALO:FFN_HALO + tm, :] = h_ref[...]

    g = jnp.dot(hbuf[...], wg_ref[...], preferred_element_type=F32)
    u = jnp.dot(hbuf[FFN_HALO:FFN_HALO + tm, :], wu_ref[...], preferred_element_type=F32)
    cw = cw_ref[...]
    conv = (cw[0:1, :] * pltpu.roll(g, 2, 0)[FFN_HALO:, :]
            + cw[1:2, :] * pltpu.roll(g, 1, 0)[FFN_HALO:, :]
            + cw[2:3, :] * g[FFN_HALO:, :]) + cb_ref[...]
    o_ref[...] = (_silu(conv) * u).astype(o_ref.dtype)


def _ffn_up(h, wg, wu, cw, cb, tm=1024, tn=256):
    s, k = h.shape
    f = wg.shape[1]
    halo_blocks = tm // FFN_HALO
    return pl.pallas_call(
        functools.partial(_ffn_up_body, tm=tm),
        grid=(s // tm, f // tn),
        in_specs=[pl.BlockSpec((tm, k), lambda i, j: (i, 0)),
                  pl.BlockSpec((FFN_HALO, k), lambda i, j: (jnp.maximum(i * halo_blocks - 1, 0), 0)),
                  pl.BlockSpec((k, tn), lambda i, j: (0, j)),
                  pl.BlockSpec((k, tn), lambda i, j: (0, j)),
                  pl.BlockSpec((FFN_CONV, tn), lambda i, j: (0, j)),
                  pl.BlockSpec((1, tn), lambda i, j: (0, j))],
        out_specs=pl.BlockSpec((tm, tn), lambda i, j: (i, j)),
        out_shape=jax.ShapeDtypeStruct((s, f), BF16),
        scratch_shapes=[pltpu.VMEM((tm + FFN_HALO, k), BF16)],
        compiler_params=_cparams(("parallel", "arbitrary")),
        name="ffn_up",
    )(h, h, wg, wu, cw, cb.reshape(1, f))


def _ffn_down_body(x_ref, a_ref, w_ref, o_ref, acc_ref):
    kk = pl.program_id(2)

    @pl.when(kk == 0)
    def _init():
        acc_ref[...] = x_ref[...]

    acc_ref[...] += jnp.dot(a_ref[...], w_ref[...], preferred_element_type=F32)

    @pl.when(kk == pl.num_programs(2) - 1)
    def _done():
        o_ref[...] = acc_ref[...]


def _ffn_down(x, act, w, tm=1024, tn=512, ksplit=2):
    s, d = x.shape
    f = act.shape[1]
    tk = f // ksplit
    return pl.pallas_call(
        _ffn_down_body,
        grid=(s // tm, d // tn, ksplit),
        in_specs=[pl.BlockSpec((tm, tn), lambda i, j, kk: (i, j)),
                  pl.BlockSpec((tm, tk), lambda i, j, kk: (i, kk)),
                  pl.BlockSpec((tk, tn), lambda i, j, kk: (kk, j))],
        out_specs=pl.BlockSpec((tm, tn), lambda i, j, kk: (i, j)),
        out_shape=jax.ShapeDtypeStruct((s, d), F32),
        scratch_shapes=[pltpu.VMEM((tm, tn), F32)],
        compiler_params=_cparams(("parallel", "arbitrary", "arbitrary")),
        name="ffn_down",
    )(x, act, w)


def _unit_lower_inverse(n_mat, size):
    row = lax.broadcasted_iota(jnp.int32, (size, size), 0)
    col = lax.broadcasted_iota(jnp.int32, (size, size), 1)
    eye = jnp.where(row == col, 1.0, 0.0).astype(F32)
    inv = eye + n_mat
    power = n_mat
    steps = int(math.log2(size)) - 1
    for _ in range(steps):
        power = _dot(power, power, precise=True)
        inv = inv + _dot(inv, power, precise=True)
    return inv


def _gdn_body(alog_ref, dtb_ref, q_ref, k_ref, v_ref, z_ref, ba_ref, cq_ref, ck_ref, cv_ref,
              nw_ref, o_ref, xbuf, state_ref, *, tile):
    c = GDN_CHUNK
    h = pl.program_id(0)
    t = pl.program_id(1)

    @pl.when(t == 0)
    def _reset():
        state_ref[...] = jnp.zeros_like(state_ref)
        xbuf[:, 0:8, :] = jnp.zeros((3, 8, GDN_DIM), F32)

    convs = []
    for idx, (src, cw_ref) in enumerate(((q_ref, cq_ref), (k_ref, ck_ref), (v_ref, cv_ref))):
        xbuf[idx, 8:8 + tile, :] = src[...]
        cw = cw_ref[...]
        acc = cw[0:1, :] * xbuf[idx, 5:5 + tile, :]
        acc += cw[1:2, :] * xbuf[idx, 6:6 + tile, :]
        acc += cw[2:3, :] * xbuf[idx, 7:7 + tile, :]
        acc += cw[3:4, :] * xbuf[idx, 8:8 + tile, :]
        convs.append(_silu(acc))
        xbuf[idx, 0:8, :] = xbuf[idx, tile:tile + 8, :]
    q, k, v = convs
    q = q * lax.rsqrt(jnp.sum(q * q, axis=-1, keepdims=True) + 1e-6) * (GDN_DIM ** -0.5)
    k = k * lax.rsqrt(jnp.sum(k * k, axis=-1, keepdims=True) + 1e-6)

    ba = ba_ref[...]
    lane = lax.broadcasted_iota(jnp.int32, (1, LANE), 1)
    b_col = jnp.sum(jnp.where(lane == h, ba, 0.0), axis=-1, keepdims=True)
    a_col = jnp.sum(jnp.where(lane == h + GDN_HEADS, ba, 0.0), axis=-1, keepdims=True)
    beta = _sigmoid(b_col)
    g = -jnp.exp(alog_ref[h]) * _softplus(a_col + dtb_ref[h])

    ri = lax.broadcasted_iota(jnp.int32, (tile, tile), 0)
    ci = lax.broadcasted_iota(jnp.int32, (tile, tile), 1)
    same = (ri // c) == (ci // c)
    lower = jnp.where(same & (ci <= ri), 1.0, 0.0).astype(F32)
    upper = jnp.where(same & (ri <= ci), 1.0, 0.0).astype(F32)
    g_b = g * jnp.ones((1, tile), F32)
    g_cum_col = _dot(lower, g_b, precise=True)
    g_cum_row = _dot(jnp.ones((8, tile), F32), g_b * upper, precise=True)

    r64 = lax.broadcasted_iota(jnp.int32, (c, c), 0)
    c64 = lax.broadcasted_iota(jnp.int32, (c, c), 1)
    causal = c64 <= r64
    strict = c64 < r64
    kb = k * beta
    vb = v * beta
    nw = nw_ref[...]
    state = state_ref[...]
    for n in range(tile // c):
        rows = slice(n * c, (n + 1) * c)
        g_i = g_cum_col[rows, 0:c]
        g_j = g_cum_row[0:1, n * c:(n + 1) * c]
        g_wide = g_cum_col[rows, 0:GDN_DIM]
        g_last = g_cum_col[n * c + c - 1:n * c + c, 0:GDN_DIM]
        gamma = jnp.exp(jnp.minimum(g_i - g_j, 0.0))
        qc, kc, kbc, vbc = q[rows], k[rows], kb[rows], vb[rows]
        a_mat = jnp.where(strict, _dot(kbc, kc, _NT) * gamma, 0.0)
        t_inv = _unit_lower_inverse(-a_mat, c)
        exp_g = jnp.exp(g_wide)
        u = _dot(t_inv, vbc, precise=True)
        w = _dot(t_inv, kbc * exp_g, precise=True)
        attn = jnp.where(causal, _dot(qc, kc, _NT) * gamma, 0.0)
        qg = qc * exp_g
        kd = kc * jnp.exp(g_last - g_wide)
        v_new = u - _dot(w, state)
        o = _dot(qg, state) + _dot(attn, v_new)
        state = state * jnp.exp(g_last) + _dot(kd, v_new, _TN)
        o = o * lax.rsqrt(jnp.mean(o * o, axis=-1, keepdims=True) + RMS_EPS) * nw
        o_ref[rows, :] = (o * _silu(z_ref[rows, :])).astype(o_ref.dtype)
    state_ref[...] = state


def _gdn(proj, conv_w, a_log, dt_bias, norm_w, tile=256):
    s = proj.shape[0]
    blk = lambda col: pl.BlockSpec((tile, LANE), lambda h, t: (t, col + h))
    cblk = lambda col: pl.BlockSpec((GDN_CONV, LANE), lambda h, t: (0, col + h))
    smem = pl.BlockSpec(memory_space=pltpu.SMEM)
    return pl.pallas_call(
        functools.partial(_gdn_body, tile=tile),
        grid=(GDN_HEADS, s // tile),
        in_specs=[smem, smem,
                  blk(COL_GDN_Q), blk(COL_GDN_K), blk(COL_GDN_V), blk(COL_GDN_Z),
                  pl.BlockSpec((tile, LANE), lambda h, t: (t, COL_GDN_BA)),
                  cblk(0), cblk(GDN_HEADS), cblk(2 * GDN_HEADS),
                  pl.BlockSpec((1, GDN_DIM), lambda h, t: (0, 0))],
        out_specs=pl.BlockSpec((tile, LANE), lambda h, t: (t, h)),
        out_shape=jax.ShapeDtypeStruct((s, GDN_WIDTH), BF16),
        scratch_shapes=[pltpu.VMEM((3, tile + 8, GDN_DIM), F32),
                        pltpu.VMEM((GDN_DIM, GDN_DIM), F32)],
        compiler_params=_cparams(("parallel", "arbitrary")),
        name="gdn",
    )(a_log, dt_bias, proj, proj, proj, proj, proj, conv_w, conv_w, conv_w, norm_w.reshape(1, GDN_DIM))


def _t5_bucket_table():
    exact = NUM_BUCKETS // 2
    i = np.arange(SWA_BLK)[:, None]
    j = np.arange(2 * SWA_BLK)[None, :]
    steps = np.maximum(i + SWA_BLK - j, 0)
    tables = []
    for _, dilation in SWA_PATTERNS:
        dist = steps * dilation
        d = np.maximum(dist, 1).astype(np.float32)
        ratio = (np.log(d / np.float32(exact)) / np.float32(math.log(MAX_DISTANCE / exact))
                 * np.float32(NUM_BUCKETS - exact)).astype(np.float32)
        log_b = exact + ratio.astype(np.int32)
        tables.append(np.where(dist < exact, dist, np.minimum(log_b, NUM_BUCKETS - 1)))
    return np.stack(tables).astype(np.int32)


def _swa_body(rb_ref, bkt_ref, q_ref, kc_ref, kp_ref, vc_ref, vp_ref, o_ref,
              bias_ref, kbuf, vbuf, o_scr, m_scr, l_scr, *, tile):
    h = pl.program_id(0)
    t = pl.program_id(1)
    blk = SWA_BLK

    @pl.when(t == 0)
    def _bias():
        for p in range(len(SWA_PATTERNS)):
            bkt = bkt_ref[p]
            bias = jnp.zeros((blk, 2 * blk), F32)
            for b in range(NUM_BUCKETS):
                bias = jnp.where(bkt == b, rb_ref[b, h], bias)
            bias_ref[p] = bias

    kbuf[0:tile, :] = kp_ref[...]
    kbuf[tile:2 * tile, :] = kc_ref[...]
    vbuf[0:tile, :] = vp_ref[...]
    vbuf[tile:2 * tile, :] = vc_ref[...]

    qi = lax.broadcasted_iota(jnp.int32, (blk, 2 * blk), 0)
    kj = lax.broadcasted_iota(jnp.int32, (blk, 2 * blk), 1)
    in_band = (kj >= qi) & (kj <= qi + blk)
    scale = SWA_DIM ** -0.5

    for p, (window, dil) in enumerate(SWA_PATTERNS):
        span = blk * dil
        n_blocks = tile // span
        bias = bias_ref[p]

        def block(idx, carry, p=p, dil=dil, span=span, bias=bias):
            res = idx % dil
            n = idx // dil
            start = res + n * span
            qb = q_ref[pl.ds(start, blk, stride=dil), :] * scale
            kw = kbuf[pl.ds(tile + start - span, 2 * blk, stride=dil), :]
            vw = vbuf[pl.ds(tile + start - span, 2 * blk, stride=dil), :]
            sc = _dot(qb, kw, _NT) + bias
            first = jnp.logical_and(t == 0, n == 0)
            valid = in_band & jnp.logical_not(jnp.logical_and(first, kj < blk))
            sc = jnp.where(valid, sc, NEG_INF)
            m = jnp.max(sc, axis=-1, keepdims=True)
            pe = jnp.exp(sc - m)
            l = jnp.sum(pe, axis=-1, keepdims=True)
            o = _dot(pe, vw)
            o_scr[p, pl.ds(start, blk, stride=dil), :] = o
            m_scr[p, pl.ds(start, blk, stride=dil), :] = m * jnp.ones((1, SWA_DIM), F32)
            l_scr[p, pl.ds(start, blk, stride=dil), :] = l * jnp.ones((1, SWA_DIM), F32)
            return carry

        lax.fori_loop(0, n_blocks * dil, block, 0)

    m_max = jnp.maximum(jnp.maximum(m_scr[0], m_scr[1]), m_scr[2])
    num = jnp.zeros((tile, SWA_DIM), F32)
    den = jnp.zeros((tile, SWA_DIM), F32)
    for p in range(len(SWA_PATTERNS)):
        sc = jnp.exp(m_scr[p] - m_max)
        num += o_scr[p] * sc
        den += l_scr[p] * sc
    o_ref[...] = (num / den).astype(o_ref.dtype)


def _swa(proj, rel_bias, tile=SWA_TILE):
    s = proj.shape[0]
    n_pat = len(SWA_PATTERNS)
    bkt = jnp.asarray(_t5_bucket_table())
    cur = lambda col: pl.BlockSpec((tile, LANE), lambda h, t: (t, col + h))
    prev = lambda col: pl.BlockSpec((tile, LANE), lambda h, t: (jnp.maximum(t - 1, 0), col + h))
    return pl.pallas_call(
        functools.partial(_swa_body, tile=tile),
        grid=(SWA_HEADS, s // tile),
        in_specs=[pl.BlockSpec(memory_space=pltpu.SMEM),
                  pl.BlockSpec((n_pat, SWA_BLK, 2 * SWA_BLK), lambda h, t: (0, 0, 0)),
                  cur(COL_SWA_Q), cur(COL_SWA_K), prev(COL_SWA_K), cur(COL_SWA_V), prev(COL_SWA_V)],
        out_specs=pl.BlockSpec((tile, LANE), lambda h, t: (t, h)),
        out_shape=jax.ShapeDtypeStruct((s, SWA_WIDTH), BF16),
        scratch_shapes=[pltpu.VMEM((n_pat, SWA_BLK, 2 * SWA_BLK), F32),
                        pltpu.VMEM((2 * tile, SWA_DIM), F32),
                        pltpu.VMEM((2 * tile, SWA_DIM), F32),
                        pltpu.VMEM((n_pat, tile, SWA_DIM), F32),
                        pltpu.VMEM((n_pat, tile, SWA_DIM), F32),
                        pltpu.VMEM((n_pat, tile, SWA_DIM), F32)],
        compiler_params=_cparams(("parallel", "arbitrary")),
        name="swa",
    )(rel_bias, bkt, proj, proj, proj, proj, proj)


PRM_W0, PRM_A0, PRM_KK, PRM_KA, PRM_RK, PRM_LNW, PRM_LNB, PRM_MUR, PRM_MUK, PRM_MUV = range(10)
PRM_ROWS = 16


def _rwkv_body(r_ref, k_ref, v_ref, wd_ref, ad_ref, gd_ref, prm_ref, mul_ref, mug_ref,
               wup_ref, aup_ref, gup_ref, o_ref, xb, xg, state_ref, *, tile):
    c = RWKV_CHUNK
    nd = RWKV_DIM
    t = pl.program_id(1)

    @pl.when(t == 0)
    def _reset():
        state_ref[...] = jnp.zeros_like(state_ref)
        xb[:, 0:8, :] = jnp.zeros((5, 8, LANE), F32)
        xg[0:8, :] = jnp.zeros((8, GATE_LORA_PAD), F32)

    prm = prm_ref[...]
    row = lambda i: prm[i:i + 1, :]

    def shifted(idx, src, mu):
        xb[idx, 8:8 + tile, :] = src[...]
        cur = xb[idx, 8:8 + tile, :]
        prev = xb[idx, 7:7 + tile, :]
        xb[idx, 0:8, :] = xb[idx, tile:tile + 8, :]
        return cur + mu * (prev - cur)

    r = shifted(0, r_ref, row(PRM_MUR))
    k = shifted(1, k_ref, row(PRM_MUK))
    v = shifted(2, v_ref, row(PRM_MUV))
    wd = shifted(3, wd_ref, mul_ref[0:1, :])
    ad = shifted(4, ad_ref, mul_ref[1:2, :])
    xg[8:8 + tile, :] = gd_ref[...]
    gcur = xg[8:8 + tile, :]
    gprev = xg[7:7 + tile, :]
    xg[0:8, :] = xg[tile:tile + 8, :]
    gd = gcur + mug_ref[...] * (gprev - gcur)

    w_log = -_softplus(-(row(PRM_W0) + _dot(jnp.tanh(wd), wup_ref[...]))) - 0.5
    log_w = -jnp.exp(w_log)
    a_gate = _sigmoid(row(PRM_A0) + _dot(ad, aup_ref[...]))
    g_gate = _dot(_sigmoid(gd), gup_ref[...])
    kx = k * row(PRM_KK)
    k_mod = k * (1.0 + (a_gate - 1.0) * row(PRM_KA))

    ri = lax.broadcasted_iota(jnp.int32, (tile, tile), 0)
    ci = lax.broadcasted_iota(jnp.int32, (tile, tile), 1)
    lower = jnp.where(((ri // c) == (ci // c)) & (ci <= ri), 1.0, 0.0).astype(F32)
    cum = _dot(lower, log_w, precise=True)

    r64 = lax.broadcasted_iota(jnp.int32, (c, c), 0)
    c64 = lax.broadcasted_iota(jnp.int32, (c, c), 1)
    incl = c64 <= r64
    strict = c64 < r64

    for hh in range(LANE // nd):
        lanes = slice(hh * nd, (hh + 1) * nd)
        r_h, k_h, v_h = r[:, lanes], k_mod[:, lanes], v[:, lanes]
        kx_h = kx[:, lanes]
        kk_h = kx_h * lax.rsqrt(jnp.sum(kx_h * kx_h, axis=-1, keepdims=True) + 1e-12)
        a_h = -kk_h
        b_h = kk_h * a_gate[:, lanes]
        cum_h = cum[:, lanes]
        lw_h = log_w[:, lanes]
        state = state_ref[hh]
        ys = []
        for n in range(tile // c):
            rows = slice(n * c, (n + 1) * c)
            cl = cum_h[rows]
            c_last = cl[c - 1:c, :]
            p_in = jnp.exp(cl)
            p_out = jnp.exp(-cl)
            r_t = r_h[rows] * p_in
            a_t = a_h[rows] * jnp.exp(cl - lw_h[rows])
            k_t = k_h[rows] * p_out
            b_t = b_h[rows] * p_out
            decay_end = jnp.exp(c_last - cl)
            k_e = k_h[rows] * decay_end
            b_e = b_h[rows] * decay_end
            vv = v_h[rows]
            m_ab = jnp.where(strict, _dot(a_t, b_t, _NT, precise=True), 0.0)
            m_ak = jnp.where(strict, _dot(a_t, k_t, _NT, precise=True), 0.0)
            m_rb = jnp.where(incl, _dot(r_t, b_t, _NT, precise=True), 0.0)
            m_rk = jnp.where(incl, _dot(r_t, k_t, _NT, precise=True), 0.0)
            t_inv = _unit_lower_inverse(m_ab, c)
            u = _dot(t_inv, _dot(a_t, state, _NT, precise=True) + _dot(m_ak, vv, precise=True),
                     precise=True)
            ys.append(_dot(r_t, state, _NT, precise=True) + _dot(m_rb, u, precise=True)
                      + _dot(m_rk, vv, precise=True))
            state = (state * jnp.exp(c_last) + _dot(u, b_e, _TN, precise=True)
                     + _dot(vv, k_e, _TN, precise=True))
        state_ref[hh] = state
        y_h = jnp.concatenate(ys, axis=0)
        mean = jnp.mean(y_h, axis=-1, keepdims=True)
        var = jnp.mean(jnp.square(y_h - mean), axis=-1, keepdims=True)
        y_n = (y_h - mean) * lax.rsqrt(var + GN_EPS) * row(PRM_LNW)[:, lanes] + row(PRM_LNB)[:, lanes]
        bonus = jnp.sum(r_h * k_h * row(PRM_RK)[:, lanes], axis=-1, keepdims=True) * v_h
        o_ref[:, lanes] = ((y_n + bonus) * g_gate[:, lanes]).astype(o_ref.dtype)


def _rwkv(proj, prm, mu_lora, mu_gate, w_up, a_up, g_up, tile=256):
    s = proj.shape[0]
    pairs = RWKV_WIDTH // LANE
    blk = lambda col: pl.BlockSpec((tile, LANE), lambda h, t: (t, col + h))
    fixed = lambda col: pl.BlockSpec((tile, LANE), lambda h, t: (t, col))
    return pl.pallas_call(
        functools.partial(_rwkv_body, tile=tile),
        grid=(pairs, s // tile),
        in_specs=[blk(COL_R), blk(COL_K), blk(COL_V), fixed(COL_WD), fixed(COL_AD),
                  pl.BlockSpec((tile, GATE_LORA_PAD), lambda h, t: (t, COL_GD * LANE // GATE_LORA_PAD)),
                  pl.BlockSpec((PRM_ROWS, LANE), lambda h, t: (0, h)),
                  pl.BlockSpec((8, LANE), lambda h, t: (0, 0)),
                  pl.BlockSpec((1, GATE_LORA_PAD), lambda h, t: (0, 0)),
                  pl.BlockSpec((DECAY_LORA, LANE), lambda h, t: (0, h)),
                  pl.BlockSpec((ICL_LORA, LANE), lambda h, t: (0, h)),
                  pl.BlockSpec((GATE_LORA_PAD, LANE), lambda h, t: (0, h))],
        out_specs=pl.BlockSpec((tile, LANE), lambda h, t: (t, h)),
        out_shape=jax.ShapeDtypeStruct((s, RWKV_WIDTH), BF16),
        scratch_shapes=[pltpu.VMEM((5, tile + 8, LANE), F32),
                        pltpu.VMEM((tile + 8, GATE_LORA_PAD), F32),
                        pltpu.VMEM((LANE // RWKV_DIM, RWKV_DIM, RWKV_DIM), F32)],
        compiler_params=_cparams(("parallel", "arbitrary")),
        name="rwkv7",
    )(proj, proj, proj, proj, proj, proj, prm, mu_lora, mu_gate, w_up, a_up, g_up)


def _layout_w_in(w):
    d = w.shape[0]
    gw, sw, rw = GDN_WIDTH, SWA_WIDTH, RWKV_WIDTH
    o = 0
    gdn_main = w[:, o:o + 4 * gw]; o += 4 * gw
    gdn_ba = w[:, o:o + 2 * GDN_HEADS]; o += 2 * GDN_HEADS
    swa = w[:, o:o + 3 * sw]; o += 3 * sw
    rkv = w[:, o:o + 3 * rw]; o += 3 * rw
    wd = w[:, o:o + DECAY_LORA]; o += DECAY_LORA
    ad = w[:, o:o + ICL_LORA]; o += ICL_LORA
    gd = w[:, o:o + GATE_LORA]
    z = lambda n: jnp.zeros((d, n), w.dtype)
    out = jnp.concatenate([gdn_main, gdn_ba, z(LANE - 2 * GDN_HEADS), swa, wd, ad, z(LANE),
                           gd, z(GATE_LORA_PAD - GATE_LORA), rkv], axis=1)
    assert out.shape[1] == PROJ_WIDTH
    return out.astype(BF16)


def _layout_rwkv_params(mu, w0, a0, k_k, k_a, r_k, ln_w, ln_b):
    rw = RWKV_WIDTH
    mu_r, mu_k, mu_v = mu[0:rw], mu[rw:2 * rw], mu[2 * rw:3 * rw]
    o = 3 * rw
    mu_wd = mu[o:o + DECAY_LORA]; o += DECAY_LORA
    mu_ad = mu[o:o + ICL_LORA]; o += ICL_LORA
    mu_gd = mu[o:o + GATE_LORA]
    rows = [w0, a0, k_k, k_a, r_k.reshape(rw), ln_w, ln_b, mu_r, mu_k, mu_v]
    prm = jnp.stack(rows + [jnp.zeros((rw,), F32)] * (PRM_ROWS - len(rows))).astype(F32)
    mu_lora = jnp.stack([mu_wd, mu_ad] + [jnp.zeros((LANE,), F32)] * 6).astype(F32)
    mu_gate = jnp.concatenate([mu_gd, jnp.zeros((GATE_LORA_PAD - GATE_LORA,), F32)]).reshape(1, GATE_LORA_PAD)
    return prm, mu_lora, mu_gate


def kernel(x, attn_norm, w_in, gdn_conv, gdn_a_log, gdn_dt_bias, gdn_norm, rwkv_mu, rwkv_w0, rwkv_w_up, rwkv_a0, rwkv_a_up, rwkv_g_up, rwkv_k_k, rwkv_k_a, rwkv_r_k, rwkv_ln_w, rwkv_ln_b, w_out, ffn_norm, w_ffn_gate, w_ffn_up, ffn_conv, ffn_conv_b, w_ffn_down, rel_bias, final_norm):
    batch, seq, d = x.shape
    depth = w_in.shape[0]
    outs = []
    for b in range(batch):
        xb = x[b]
        for l in range(depth):
            h = _rmsnorm(xb, attn_norm[l], BF16)
            proj = _in_proj(h, _layout_w_in(w_in[l]))
            o_a = _gdn(proj, gdn_conv[l], gdn_a_log[l], gdn_dt_bias[l], gdn_norm[l])
            o_b = _swa(proj, rel_bias)
            prm, mu_lora, mu_gate = _layout_rwkv_params(
                rwkv_mu[l], rwkv_w0[l], rwkv_a0[l], rwkv_k_k[l], rwkv_k_a[l], rwkv_r_k[l],
                rwkv_ln_w[l], rwkv_ln_b[l])
            g_up = jnp.concatenate(
                [rwkv_g_up[l], jnp.zeros((GATE_LORA_PAD - GATE_LORA, RWKV_WIDTH), F32)], axis=0)
            o_c = _rwkv(proj, prm, mu_lora, mu_gate, rwkv_w_up[l].astype(BF16),
                        rwkv_a_up[l].astype(BF16), g_up.astype(BF16))
            wo = w_out[l].astype(BF16)
            xb = _out_proj(xb, o_a, o_b, o_c, wo[0:GDN_WIDTH], wo[GDN_WIDTH:GDN_WIDTH + SWA_WIDTH],
                           wo[GDN_WIDTH + SWA_WIDTH:])
            h = _rmsnorm(xb, ffn_norm[l], BF16)
            act = _ffn_up(h, w_ffn_gate[l].astype(BF16), w_ffn_up[l].astype(BF16), ffn_conv[l], ffn_conv_b[l])
            xb = _ffn_down(xb, act, w_ffn_down[l].astype(BF16))
        outs.append(_rmsnorm(xb, final_norm, x.dtype))
    return jnp.stack(outs)
```

```python
import functools
import math

import numpy as np
import jax
import jax.numpy as jnp
from jax import lax
from jax.experimental import pallas as pl
from jax.experimental.pallas import tpu as pltpu

F32 = jnp.float32
BF16 = jnp.bfloat16
HIGHEST = lax.Precision.HIGHEST

LANE = 128
D_MODEL = 4096
RMS_EPS = 1e-6
GN_EPS = 64e-5
NEG_INF = -1e30
VMEM_LIMIT = 56 * 1024 * 1024

GDN_DIM = 128
GDN_HEADS = 12
GDN_WIDTH = GDN_HEADS * GDN_DIM
GDN_CONV = 4
GDN_CHUNK = 64
GDN_GROUP = 2
SWA_DIM = 128
SWA_HEADS = 8
SWA_WIDTH = SWA_HEADS * SWA_DIM
SWA_PATTERNS = ((128, 1), (512, 4), (2048, 16))
SWA_BLK = 128
SWA_TILE = 2048
NUM_BUCKETS = 32
MAX_DISTANCE = 2048
RWKV_DIM = 64
RWKV_HEADS = 24
RWKV_WIDTH = RWKV_HEADS * RWKV_DIM
RWKV_CHUNK = 64
DECAY_LORA = 128
ICL_LORA = 128
GATE_LORA = 480
GATE_LORA_PAD = 512
D_FF = 11008
FFN_CONV = 3

COL_GDN_Q, COL_GDN_K, COL_GDN_V, COL_GDN_Z, COL_GDN_BA = 0, 12, 24, 36, 48
COL_SWA_Q, COL_SWA_K, COL_SWA_V = 49, 57, 65
COL_WD, COL_AD = 73, 74
COL_GD = 76
COL_R, COL_K, COL_V = 80, 92, 104
PROJ_BLOCKS = 116
PROJ_WIDTH = PROJ_BLOCKS * LANE


def _cparams(sem):
    return pltpu.CompilerParams(dimension_semantics=sem, vmem_limit_bytes=VMEM_LIMIT)


def _silu(x):
    return x * (1.0 / (1.0 + jnp.exp(-x)))


def _sigmoid(x):
    return 1.0 / (1.0 + jnp.exp(-x))


def _softplus(x):
    return jnp.maximum(x, 0.0) + jnp.log(1.0 + jnp.exp(-jnp.abs(x)))


def _split_bf16(x):
    hi = x.astype(BF16)
    return hi, (x - hi.astype(F32)).astype(BF16)


def _dot(a, b, dims=(((1,), (0,)), ((), ())), mode="bf16"):
    if mode == "f32":
        return lax.dot_general(a.astype(F32), b.astype(F32), dims, precision=HIGHEST,
                               preferred_element_type=F32)
    if mode == "x3":
        a_hi, a_lo = _split_bf16(a)
        b_hi, b_lo = _split_bf16(b)
        mm = lambda p, q: lax.dot_general(p, q, dims, preferred_element_type=F32)
        return mm(a_hi, b_hi) + (mm(a_hi, b_lo) + mm(a_lo, b_hi))
    return lax.dot_general(a.astype(BF16), b.astype(BF16), dims, preferred_element_type=F32)


MODE_CUMSUM = "f32"
MODE_INV = "bf16"
MODE_SOLVE = "bf16"
MODE_STATE = "bf16"


_NT = (((1,), (1,)), ((), ()))
_TN = (((0,), (0,)), ((), ()))


def _rmsnorm_body(x_ref, w_ref, o_ref):
    x = x_ref[...]
    ms = jnp.mean(x * x, axis=-1, keepdims=True)
    o_ref[...] = (x * lax.rsqrt(ms + RMS_EPS) * w_ref[...]).astype(o_ref.dtype)


def _rmsnorm(x, w, out_dtype, tm=512):
    s, d = x.shape
    return pl.pallas_call(
        _rmsnorm_body,
        grid=(s // tm,),
        in_specs=[pl.BlockSpec((tm, d), lambda i: (i, 0)),
                  pl.BlockSpec((1, d), lambda i: (0, 0))],
        out_specs=pl.BlockSpec((tm, d), lambda i: (i, 0)),
        out_shape=jax.ShapeDtypeStruct((s, d), out_dtype),
        compiler_params=_cparams(("parallel",)),
        name="rmsnorm",
    )(x, w.reshape(1, d))


def _matmul_body(a_ref, b_ref, o_ref):
    o_ref[...] = jnp.dot(a_ref[...], b_ref[...], preferred_element_type=F32).astype(o_ref.dtype)


def _in_proj(h, w, tm=1024, tn=512):
    s, k = h.shape
    n = w.shape[1]
    return pl.pallas_call(
        _matmul_body,
        grid=(s // tm, n // tn),
        in_specs=[pl.BlockSpec((tm, k), lambda i, j: (i, 0)),
                  pl.BlockSpec((k, tn), lambda i, j: (0, j))],
        out_specs=pl.BlockSpec((tm, tn), lambda i, j: (i, j)),
        out_shape=jax.ShapeDtypeStruct((s, n), F32),
        compiler_params=_cparams(("parallel", "arbitrary")),
        name="in_proj",
    )(h, w)


def _out_proj_body(x_ref, a_ref, b_ref, c_ref, wa_ref, wb_ref, wc_ref, o_ref):
    acc = jnp.dot(a_ref[...], wa_ref[...], preferred_element_type=F32)
    acc += jnp.dot(b_ref[...], wb_ref[...], preferred_element_type=F32)
    acc += jnp.dot(c_ref[...], wc_ref[...], preferred_element_type=F32)
    o_ref[...] = x_ref[...] + acc


def _out_proj(x, oa, ob, oc, wa, wb, wc, tm=1024, tn=512):
    s, d = x.shape
    ka, kb, kc = oa.shape[1], ob.shape[1], oc.shape[1]
    return pl.pallas_call(
        _out_proj_body,
        grid=(s // tm, d // tn),
        in_specs=[pl.BlockSpec((tm, tn), lambda i, j: (i, j)),
                  pl.BlockSpec((tm, ka), lambda i, j: (i, 0)),
                  pl.BlockSpec((tm, kb), lambda i, j: (i, 0)),
                  pl.BlockSpec((tm, kc), lambda i, j: (i, 0)),
                  pl.BlockSpec((ka, tn), lambda i, j: (0, j)),
                  pl.BlockSpec((kb, tn), lambda i, j: (0, j)),
                  pl.BlockSpec((kc, tn), lambda i, j: (0, j))],
        out_specs=pl.BlockSpec((tm, tn), lambda i, j: (i, j)),
        out_shape=jax.ShapeDtypeStruct((s, d), F32),
        compiler_params=_cparams(("parallel", "arbitrary")),
        name="out_proj",
    )(x, oa, ob, oc, wa, wb, wc)


FFN_HALO = 16


def _ffn_up_body(h_ref, halo_ref, wg_ref, wu_ref, cw_ref, cb_ref, o_ref, hbuf, *, tm):
    i = pl.program_id(0)

    @pl.when(pl.program_id(1) == 0)
    def _stage():
        halo = halo_ref[...]
        hbuf[0:FFN_HALO, :] = jnp.where(i == 0, jnp.zeros_like(halo), halo)
        hbuf[FFN_HALO:FFN_HALO + tm, :] = h_ref[...]

    g = jnp.dot(hbuf[...], wg_ref[...], preferred_element_type=F32)
    u = jnp.dot(hbuf[FFN_HALO:FFN_HALO + tm, :], wu_ref[...], preferred_element_type=F32)
    cw = cw_ref[...]
    conv = (cw[0:1, :] * pltpu.roll(g, 2, 0)[FFN_HALO:, :]
            + cw[1:2, :] * pltpu.roll(g, 1, 0)[FFN_HALO:, :]
            + cw[2:3, :] * g[FFN_HALO:, :]) + cb_ref[...]
    o_ref[...] = (_silu(conv) * u).astype(o_ref.dtype)


def _ffn_up(h, wg, wu, cw, cb, tm=1024, tn=256):
    s, k = h.shape
    f = wg.shape[1]
    halo_blocks = tm // FFN_HALO
    return pl.pallas_call(
        functools.partial(_ffn_up_body, tm=tm),
        grid=(s // tm, f // tn),
        in_specs=[pl.BlockSpec((tm, k), lambda i, j: (i, 0)),
                  pl.BlockSpec((FFN_HALO, k), lambda i, j: (jnp.maximum(i * halo_blocks - 1, 0), 0)),
                  pl.BlockSpec((k, tn), lambda i, j: (0, j)),
                  pl.BlockSpec((k, tn), lambda i, j: (0, j)),
                  pl.BlockSpec((FFN_CONV, tn), lambda i, j: (0, j)),
                  pl.BlockSpec((1, tn), lambda i, j: (0, j))],
        out_specs=pl.BlockSpec((tm, tn), lambda i, j: (i, j)),
        out_shape=jax.ShapeDtypeStruct((s, f), BF16),
        scratch_shapes=[pltpu.VMEM((tm + FFN_HALO, k), BF16)],
        compiler_params=_cparams(("parallel", "arbitrary")),
        name="ffn_up",
    )(h, h, wg, wu, cw, cb.reshape(1, f))


def _ffn_down_body(x_ref, a_ref, w_ref, o_ref, acc_ref):
    kk = pl.program_id(2)

    @pl.when(kk == 0)
    def _init():
        acc_ref[...] = x_ref[...]

    acc_ref[...] += jnp.dot(a_ref[...], w_ref[...], preferred_element_type=F32)

    @pl.when(kk == pl.num_programs(2) - 1)
    def _done():
        o_ref[...] = acc_ref[...]


def _ffn_down(x, act, w, tm=1024, tn=512, ksplit=2):
    s, d = x.shape
    f = act.shape[1]
    tk = f // ksplit
    return pl.pallas_call(
        _ffn_down_body,
        grid=(s // tm, d // tn, ksplit),
        in_specs=[pl.BlockSpec((tm, tn), lambda i, j, kk: (i, j)),
                  pl.BlockSpec((tm, tk), lambda i, j, kk: (i, kk)),
                  pl.BlockSpec((tk, tn), lambda i, j, kk: (kk, j))],
        out_specs=pl.BlockSpec((tm, tn), lambda i, j, kk: (i, j)),
        out_shape=jax.ShapeDtypeStruct((s, d), F32),
        scratch_shapes=[pltpu.VMEM((tm, tn), F32)],
        compiler_params=_cparams(("parallel", "arbitrary", "arbitrary")),
        name="ffn_down",
    )(x, act, w)


def _unit_lower_inverses(n_mats, size):
    row = lax.broadcasted_iota(jnp.int32, (size, size), 0)
    col = lax.broadcasted_iota(jnp.int32, (size, size), 1)
    eye = jnp.where(row == col, 1.0, 0.0).astype(F32)
    invs = [eye + n for n in n_mats]
    powers = [_dot(n, n, mode=MODE_INV) for n in n_mats]
    steps = int(math.log2(size)) - 1
    for s in range(steps):
        last = s == steps - 1
        for i, (inv, p) in enumerate(zip(invs, powers)):
            if last:
                invs[i] = inv + _dot(inv, p, mode=MODE_INV)
            else:
                both = _dot(jnp.concatenate([inv, p], axis=0), p, mode=MODE_INV)
                invs[i] = inv + both[0:size]
                powers[i] = both[size:2 * size]
    return invs


def _gdn_body(alog_ref, dtb_ref, q_ref, k_ref, v_ref, z_ref, ba_ref, cq_ref, ck_ref, cv_ref,
              nw_ref, o_ref, xbuf, state_ref, *, tile):
    c = GDN_CHUNK
    hd = GDN_DIM
    width = GDN_GROUP * hd
    n_chunks = tile // c
    t = pl.program_id(1)

    @pl.when(t == 0)
    def _reset():
        state_ref[...] = jnp.zeros_like(state_ref)
        xbuf[:, 0:8, :] = jnp.zeros((3, 8, width), F32)

    convs = []
    for idx, (src, cw_ref) in enumerate(((q_ref, cq_ref), (k_ref, ck_ref), (v_ref, cv_ref))):
        xbuf[idx, 8:8 + tile, :] = src[...]
        cw = cw_ref[...]
        acc = cw[0:1, :] * xbuf[idx, 5:5 + tile, :]
        acc += cw[1:2, :] * xbuf[idx, 6:6 + tile, :]
        acc += cw[2:3, :] * xbuf[idx, 7:7 + tile, :]
        acc += cw[3:4, :] * xbuf[idx, 8:8 + tile, :]
        convs.append(_silu(acc))
        xbuf[idx, 0:8, :] = xbuf[idx, tile:tile + 8, :]
    q_all, k_all, v_all = convs

    ba = ba_ref[...]
    lane = lax.broadcasted_iota(jnp.int32, (1, LANE), 1)
    ri = lax.broadcasted_iota(jnp.int32, (tile, tile), 0)
    ci = lax.broadcasted_iota(jnp.int32, (tile, tile), 1)
    same = (ri // c) == (ci // c)
    lower = jnp.where(same & (ci <= ri), 1.0, 0.0).astype(F32)
    upper = jnp.where(same & (ri <= ci), 1.0, 0.0).astype(F32)
    r2 = lax.broadcasted_iota(jnp.int32, (2 * c, c), 0)
    c2 = lax.broadcasted_iota(jnp.int32, (2 * c, c), 1)
    band = c2 <= jnp.where(r2 < c, r2 - 1, r2 - c)
    nw = nw_ref[...]

    units = []
    for j in range(GDN_GROUP):
        h = pl.program_id(0) * GDN_GROUP + j
        cols = slice(j * hd, (j + 1) * hd)
        q, k, v = q_all[:, cols], k_all[:, cols], v_all[:, cols]
        q = q * lax.rsqrt(jnp.sum(q * q, axis=-1, keepdims=True) + 1e-6) * (hd ** -0.5)
        k = k * lax.rsqrt(jnp.sum(k * k, axis=-1, keepdims=True) + 1e-6)
        b_col = jnp.sum(jnp.where(lane == h, ba, 0.0), axis=-1, keepdims=True)
        a_col = jnp.sum(jnp.where(lane == h + GDN_HEADS, ba, 0.0), axis=-1, keepdims=True)
        beta = _sigmoid(b_col)
        g = -jnp.exp(alog_ref[h]) * _softplus(a_col + dtb_ref[h])
        g_col = _dot(lower, g * jnp.ones((1, hd), F32), mode=MODE_CUMSUM)
        g_row = _dot(jnp.ones((8, tile), F32), (g * jnp.ones((1, tile), F32)) * upper,
                     mode=MODE_CUMSUM)
        exp_g = jnp.exp(g_col)
        kb = k * beta
        vb = v * beta
        kbe = kb * exp_g
        qg = q * exp_g
        for n in range(n_chunks):
            rows = slice(n * c, (n + 1) * c)
            g_last = g_col[n * c + c - 1:n * c + c, :]
            gamma = jnp.exp(jnp.minimum(g_col[rows, 0:c] - g_row[0:1, n * c:(n + 1) * c], 0.0))
            units.append(dict(j=j, n=n, k=k[rows], kq=jnp.concatenate([kb[rows], q[rows]], axis=0),
                              rhs=jnp.concatenate([vb[rows], kbe[rows]], axis=1), qg=qg[rows],
                              kd=k[rows] * jnp.exp(g_last - g_col[rows]), decay=jnp.exp(g_last),
                              gamma2=jnp.concatenate([gamma, gamma], axis=0)))
    for un in units:
        un["scores"] = jnp.where(band, _dot(un["kq"], un["k"], _NT) * un["gamma2"], 0.0)
    t_invs = _unit_lower_inverses([-un["scores"][0:c] for un in units], c)
    for un, t_inv in zip(units, t_invs):
        un["uw"] = _dot(t_inv, un["rhs"], mode=MODE_SOLVE)

    states = [state_ref[j] for j in range(GDN_GROUP)]
    outs = [[None] * n_chunks for _ in range(GDN_GROUP)]
    for n in range(n_chunks):
        for un in units:
            if un["n"] != n:
                continue
            j = un["j"]
            ws = _dot(jnp.concatenate([un["uw"][:, hd:2 * hd], un["qg"]], axis=0), states[j],
                      mode=MODE_STATE)
            v_new = un["uw"][:, 0:hd] - ws[0:c]
            outs[j][n] = ws[c:2 * c] + _dot(un["scores"][c:2 * c], v_new, mode=MODE_STATE)
            states[j] = states[j] * un["decay"] + _dot(un["kd"], v_new, _TN, mode=MODE_STATE)
    for j in range(GDN_GROUP):
        state_ref[j] = states[j]
        cols = slice(j * hd, (j + 1) * hd)
        o = jnp.concatenate(outs[j], axis=0)
        o = o * lax.rsqrt(jnp.mean(o * o, axis=-1, keepdims=True) + RMS_EPS) * nw
        o_ref[:, cols] = (o * _silu(z_ref[:, cols])).astype(o_ref.dtype)


def _gdn(proj, conv_w, a_log, dt_bias, norm_w, tile=256):
    s = proj.shape[0]
    width = GDN_GROUP * GDN_DIM
    per = LANE // GDN_DIM * GDN_GROUP
    blk = lambda col: pl.BlockSpec((tile, width), lambda h, t: (t, col // per + h))
    cblk = lambda col: pl.BlockSpec((GDN_CONV, width), lambda h, t: (0, col // per + h))
    smem = pl.BlockSpec(memory_space=pltpu.SMEM)
    return pl.pallas_call(
        functools.partial(_gdn_body, tile=tile),
        grid=(GDN_HEADS // GDN_GROUP, s // tile),
        in_specs=[smem, smem,
                  blk(COL_GDN_Q), blk(COL_GDN_K), blk(COL_GDN_V), blk(COL_GDN_Z),
                  pl.BlockSpec((tile, LANE), lambda h, t: (t, COL_GDN_BA)),
                  cblk(0), cblk(GDN_HEADS), cblk(2 * GDN_HEADS),
                  pl.BlockSpec((1, GDN_DIM), lambda h, t: (0, 0))],
        out_specs=pl.BlockSpec((tile, width), lambda h, t: (t, h)),
        out_shape=jax.ShapeDtypeStruct((s, GDN_WIDTH), BF16),
        scratch_shapes=[pltpu.VMEM((3, tile + 8, width), F32),
                        pltpu.VMEM((GDN_GROUP, GDN_DIM, GDN_DIM), F32)],
        compiler_params=_cparams(("parallel", "arbitrary")),
        name="gdn",
    )(a_log, dt_bias, proj, proj, proj, proj, proj, conv_w, conv_w, conv_w, norm_w.reshape(1, GDN_DIM))


def _t5_bucket_table():
    exact = NUM_BUCKETS // 2
    i = np.arange(SWA_BLK)[:, None]
    j = np.arange(2 * SWA_BLK)[None, :]
    steps = np.maximum(i + SWA_BLK - j, 0)
    tables = []
    for _, dilation in SWA_PATTERNS:
        dist = steps * dilation
        d = np.maximum(dist, 1).astype(np.float32)
        ratio = (np.log(d / np.float32(exact)) / np.float32(math.log(MAX_DISTANCE / exact))
                 * np.float32(NUM_BUCKETS - exact)).astype(np.float32)
        log_b = exact + ratio.astype(np.int32)
        tables.append(np.where(dist < exact, dist, np.minimum(log_b, NUM_BUCKETS - 1)))
    return np.stack(tables).astype(np.int32)


def _swa_body(rb_ref, bkt_ref, q_ref, kc_ref, kp_ref, vc_ref, vp_ref, o_ref,
              bias_ref, kbuf, vbuf, o_scr, m_scr, l_scr, *, tile):
    h = pl.program_id(0)
    t = pl.program_id(1)
    blk = SWA_BLK

    @pl.when(t == 0)
    def _bias():
        for p in range(len(SWA_PATTERNS)):
            bkt = bkt_ref[p]
            bias = jnp.zeros((blk, 2 * blk), F32)
            for b in range(NUM_BUCKETS):
                bias = jnp.where(bkt == b, rb_ref[b, h], bias)
            bias_ref[p] = bias

    kbuf[0:tile, :] = kp_ref[...]
    kbuf[tile:2 * tile, :] = kc_ref[...]
    vbuf[0:tile, :] = vp_ref[...]
    vbuf[tile:2 * tile, :] = vc_ref[...]

    qi = lax.broadcasted_iota(jnp.int32, (blk, 2 * blk), 0)
    kj = lax.broadcasted_iota(jnp.int32, (blk, 2 * blk), 1)
    in_band = (kj >= qi) & (kj <= qi + blk)
    scale = SWA_DIM ** -0.5

    for p, (window, dil) in enumerate(SWA_PATTERNS):
        span = blk * dil
        n_blocks = tile // span
        bias = bias_ref[p]

        def block(idx, carry, p=p, dil=dil, span=span, bias=bias):
            res = idx % dil
            n = idx // dil
            start = res + n * span
            qb = q_ref[pl.ds(start, blk, stride=dil), :] * scale
            kw = kbuf[pl.ds(tile + start - span, 2 * blk, stride=dil), :]
            vw = vbuf[pl.ds(tile + start - span, 2 * blk, stride=dil), :]
            sc = _dot(qb, kw, _NT) + bias
            first = jnp.logical_and(t == 0, n == 0)
            valid = in_band & jnp.logical_not(jnp.logical_and(first, kj < blk))
            sc = jnp.where(valid, sc, NEG_INF)
            m = jnp.max(sc, axis=-1, keepdims=True)
            pe = jnp.exp(sc - m)
            l = jnp.sum(pe, axis=-1, keepdims=True)
            o = _dot(pe, vw)
            o_scr[p, pl.ds(start, blk, stride=dil), :] = o
            m_scr[p, pl.ds(start, blk, stride=dil), :] = m * jnp.ones((1, SWA_DIM), F32)
            l_scr[p, pl.ds(start, blk, stride=dil), :] = l * jnp.ones((1, SWA_DIM), F32)
            return carry

        lax.fori_loop(0, n_blocks * dil, block, 0)

    m_max = jnp.maximum(jnp.maximum(m_scr[0], m_scr[1]), m_scr[2])
    num = jnp.zeros((tile, SWA_DIM), F32)
    den = jnp.zeros((tile, SWA_DIM), F32)
    for p in range(len(SWA_PATTERNS)):
        sc = jnp.exp(m_scr[p] - m_max)
        num += o_scr[p] * sc
        den += l_scr[p] * sc
    o_ref[...] = (num / den).astype(o_ref.dtype)


def _swa(proj, rel_bias, tile=SWA_TILE):
    s = proj.shape[0]
    n_pat = len(SWA_PATTERNS)
    bkt = jnp.asarray(_t5_bucket_table())
    cur = lambda col: pl.BlockSpec((tile, LANE), lambda h, t: (t, col + h))
    prev = lambda col: pl.BlockSpec((tile, LANE), lambda h, t: (jnp.maximum(t - 1, 0), col + h))
    return pl.pallas_call(
        functools.partial(_swa_body, tile=tile),
        grid=(SWA_HEADS, s // tile),
        in_specs=[pl.BlockSpec(memory_space=pltpu.SMEM),
                  pl.BlockSpec((n_pat, SWA_BLK, 2 * SWA_BLK), lambda h, t: (0, 0, 0)),
                  cur(COL_SWA_Q), cur(COL_SWA_K), prev(COL_SWA_K), cur(COL_SWA_V), prev(COL_SWA_V)],
        out_specs=pl.BlockSpec((tile, LANE), lambda h, t: (t, h)),
        out_shape=jax.ShapeDtypeStruct((s, SWA_WIDTH), BF16),
        scratch_shapes=[pltpu.VMEM((n_pat, SWA_BLK, 2 * SWA_BLK), F32),
                        pltpu.VMEM((2 * tile, SWA_DIM), F32),
                        pltpu.VMEM((2 * tile, SWA_DIM), F32),
                        pltpu.VMEM((n_pat, tile, SWA_DIM), F32),
                        pltpu.VMEM((n_pat, tile, SWA_DIM), F32),
                        pltpu.VMEM((n_pat, tile, SWA_DIM), F32)],
        compiler_params=_cparams(("parallel", "arbitrary")),
        name="swa",
    )(rel_bias, bkt, proj, proj, proj, proj, proj)


PRM_W0, PRM_A0, PRM_KK, PRM_KA, PRM_RK, PRM_LNW, PRM_LNB, PRM_MUR, PRM_MUK, PRM_MUV = range(10)
PRM_ROWS = 16


def _rwkv_body(r_ref, k_ref, v_ref, wd_ref, ad_ref, gd_ref, prm_ref, mul_ref, mug_ref,
               wup_ref, aup_ref, gup_ref, o_ref, xb, xg, state_ref, *, tile):
    c = RWKV_CHUNK
    nd = RWKV_DIM
    t = pl.program_id(1)

    @pl.when(t == 0)
    def _reset():
        state_ref[...] = jnp.zeros_like(state_ref)
        xb[:, 0:8, :] = jnp.zeros((5, 8, LANE), F32)
        xg[0:8, :] = jnp.zeros((8, GATE_LORA_PAD), F32)

    prm = prm_ref[...]
    row = lambda i: prm[i:i + 1, :]

    def shifted(idx, src, mu):
        xb[idx, 8:8 + tile, :] = src[...]
        cur = xb[idx, 8:8 + tile, :]
        prev = xb[idx, 7:7 + tile, :]
        xb[idx, 0:8, :] = xb[idx, tile:tile + 8, :]
        return cur + mu * (prev - cur)

    r = shifted(0, r_ref, row(PRM_MUR))
    k = shifted(1, k_ref, row(PRM_MUK))
    v = shifted(2, v_ref, row(PRM_MUV))
    wd = shifted(3, wd_ref, mul_ref[0:1, :])
    ad = shifted(4, ad_ref, mul_ref[1:2, :])
    xg[8:8 + tile, :] = gd_ref[...]
    gcur = xg[8:8 + tile, :]
    gprev = xg[7:7 + tile, :]
    xg[0:8, :] = xg[tile:tile + 8, :]
    gd = gcur + mug_ref[...] * (gprev - gcur)

    w_log = -_softplus(-(row(PRM_W0) + _dot(jnp.tanh(wd), wup_ref[...]))) - 0.5
    log_w = -jnp.exp(w_log)
    a_gate = _sigmoid(row(PRM_A0) + _dot(ad, aup_ref[...]))
    g_gate = _dot(_sigmoid(gd), gup_ref[...])
    kx = k * row(PRM_KK)
    k_mod = k * (1.0 + (a_gate - 1.0) * row(PRM_KA))

    ri = lax.broadcasted_iota(jnp.int32, (tile, tile), 0)
    ci = lax.broadcasted_iota(jnp.int32, (tile, tile), 1)
    lower = jnp.where(((ri // c) == (ci // c)) & (ci <= ri), 1.0, 0.0).astype(F32)
    cum = _dot(lower, log_w, mode=MODE_CUMSUM)

    r2 = lax.broadcasted_iota(jnp.int32, (2 * c, 2 * c), 0)
    c2 = lax.broadcasted_iota(jnp.int32, (2 * c, 2 * c), 1)
    col_in = jnp.where(c2 < c, c2, c2 - c)
    band = col_in <= jnp.where(r2 < c, r2 - 1, r2 - c)

    n_heads = LANE // nd
    n_chunks = tile // c
    lane = lax.broadcasted_iota(jnp.int32, (1, LANE), 1)
    head_masks = [(lane // nd) == hh for hh in range(n_heads)]

    def head_sum(x):
        out = jnp.zeros_like(x)
        for msk in head_masks:
            out = jnp.where(msk, jnp.sum(jnp.where(msk, x, 0.0), axis=-1, keepdims=True), out)
        return out

    kk = kx * lax.rsqrt(head_sum(kx * kx) + 1e-12)
    b_vec = kk * a_gate
    c_last = jnp.concatenate(
        [jnp.broadcast_to(cum[n * c + c - 1:n * c + c, :], (c, LANE)) for n in range(n_chunks)], axis=0)
    p_out = jnp.exp(-cum)
    a_t = -kk * jnp.exp(cum - log_w)
    r_t = r * jnp.exp(cum)
    k_t = k_mod * p_out
    b_t = b_vec * p_out
    decay_end = jnp.exp(c_last - cum)
    k_e = k_mod * decay_end
    b_e = b_vec * decay_end
    decay_all = jnp.exp(c_last)

    units = []
    for hh in range(n_heads):
        lanes = slice(hh * nd, (hh + 1) * nd)
        for n in range(n_chunks):
            rows = slice(n * c, (n + 1) * c)
            units.append(dict(hh=hh, n=n, a=a_t[rows, lanes], r=r_t[rows, lanes], v=v[rows, lanes],
                              ar=jnp.concatenate([a_t[rows, lanes], r_t[rows, lanes]], axis=0),
                              bk=jnp.concatenate([b_t[rows, lanes], k_t[rows, lanes]], axis=0),
                              bk_e=jnp.concatenate([b_e[rows, lanes], k_e[rows, lanes]], axis=0),
                              decay=decay_all[n * c:n * c + 1, lanes]))
    for un in units:
        un["mm"] = jnp.where(band, _dot(un["ar"], un["bk"], _NT), 0.0)
    t_invs = _unit_lower_inverses([un["mm"][0:c, 0:c] for un in units], c)
    for un in units:
        un["mv"] = _dot(un["mm"][0:c, c:2 * c], un["v"], mode=MODE_SOLVE)
    for un, t_inv in zip(units, t_invs):
        un["wu"] = _dot(t_inv, jnp.concatenate([un["a"], un["mv"]], axis=1), mode=MODE_SOLVE)

    states = [state_ref[hh] for hh in range(n_heads)]
    ys = [[None] * n_chunks for _ in range(n_heads)]
    for n in range(n_chunks):
        for un in units:
            if un["n"] != n:
                continue
            hh = un["hh"]
            sr = _dot(jnp.concatenate([un["wu"][:, 0:nd], un["r"]], axis=0), states[hh], _NT,
                      mode=MODE_STATE)
            u = sr[0:c] + un["wu"][:, nd:2 * nd]
            uv = jnp.concatenate([u, un["v"]], axis=0)
            ys[hh][n] = sr[c:2 * c] + _dot(un["mm"][c:2 * c], uv, mode=MODE_STATE)
            states[hh] = states[hh] * un["decay"] + _dot(uv, un["bk_e"], _TN, mode=MODE_STATE)
    for hh in range(n_heads):
        state_ref[hh] = states[hh]
    y = jnp.concatenate([jnp.concatenate(ys[hh], axis=0) for hh in range(n_heads)], axis=1)
    mean = head_sum(y) * (1.0 / nd)
    var = head_sum(jnp.square(y - mean)) * (1.0 / nd)
    y_n = (y - mean) * lax.rsqrt(var + GN_EPS) * row(PRM_LNW) + row(PRM_LNB)
    bonus = head_sum(r * k_mod * row(PRM_RK)) * v
    o_ref[...] = ((y_n + bonus) * g_gate).astype(o_ref.dtype)


def _rwkv(proj, prm, mu_lora, mu_gate, w_up, a_up, g_up, tile=256):
    s = proj.shape[0]
    pairs = RWKV_WIDTH // LANE
    blk = lambda col: pl.BlockSpec((tile, LANE), lambda h, t: (t, col + h))
    fixed = lambda col: pl.BlockSpec((tile, LANE), lambda h, t: (t, col))
    return pl.pallas_call(
        functools.partial(_rwkv_body, tile=tile),
        grid=(pairs, s // tile),
        in_specs=[blk(COL_R), blk(COL_K), blk(COL_V), fixed(COL_WD), fixed(COL_AD),
                  pl.BlockSpec((tile, GATE_LORA_PAD), lambda h, t: (t, COL_GD * LANE // GATE_LORA_PAD)),
                  pl.BlockSpec((PRM_ROWS, LANE), lambda h, t: (0, h)),
                  pl.BlockSpec((8, LANE), lambda h, t: (0, 0)),
                  pl.BlockSpec((1, GATE_LORA_PAD), lambda h, t: (0, 0)),
                  pl.BlockSpec((DECAY_LORA, LANE), lambda h, t: (0, h)),
                  pl.BlockSpec((ICL_LORA, LANE), lambda h, t: (0, h)),
                  pl.BlockSpec((GATE_LORA_PAD, LANE), lambda h, t: (0, h))],
        out_specs=pl.BlockSpec((tile, LANE), lambda h, t: (t, h)),
        out_shape=jax.ShapeDtypeStruct((s, RWKV_WIDTH), BF16),
        scratch_shapes=[pltpu.VMEM((5, tile + 8, LANE), F32),
                        pltpu.VMEM((tile + 8, GATE_LORA_PAD), F32),
                        pltpu.VMEM((LANE // RWKV_DIM, RWKV_DIM, RWKV_DIM), F32)],
        compiler_params=_cparams(("parallel", "arbitrary")),
        name="rwkv7",
    )(proj, proj, proj, proj, proj, proj, prm, mu_lora, mu_gate, w_up, a_up, g_up)


def _layout_w_in(w):
    d = w.shape[0]
    gw, sw, rw = GDN_WIDTH, SWA_WIDTH, RWKV_WIDTH
    o = 0
    gdn_main = w[:, o:o + 4 * gw]; o += 4 * gw
    gdn_ba = w[:, o:o + 2 * GDN_HEADS]; o += 2 * GDN_HEADS
    swa = w[:, o:o + 3 * sw]; o += 3 * sw
    rkv = w[:, o:o + 3 * rw]; o += 3 * rw
    wd = w[:, o:o + DECAY_LORA]; o += DECAY_LORA
    ad = w[:, o:o + ICL_LORA]; o += ICL_LORA
    gd = w[:, o:o + GATE_LORA]
    z = lambda n: jnp.zeros((d, n), w.dtype)
    out = jnp.concatenate([gdn_main, gdn_ba, z(LANE - 2 * GDN_HEADS), swa, wd, ad, z(LANE),
                           gd, z(GATE_LORA_PAD - GATE_LORA), rkv], axis=1)
    assert out.shape[1] == PROJ_WIDTH
    return out.astype(BF16)


def _layout_rwkv_params(mu, w0, a0, k_k, k_a, r_k, ln_w, ln_b):
    rw = RWKV_WIDTH
    mu_r, mu_k, mu_v = mu[0:rw], mu[rw:2 * rw], mu[2 * rw:3 * rw]
    o = 3 * rw
    mu_wd = mu[o:o + DECAY_LORA]; o += DECAY_LORA
    mu_ad = mu[o:o + ICL_LORA]; o += ICL_LORA
    mu_gd = mu[o:o + GATE_LORA]
    rows = [w0, a0, k_k, k_a, r_k.reshape(rw), ln_w, ln_b, mu_r, mu_k, mu_v]
    prm = jnp.stack(rows + [jnp.zeros((rw,), F32)] * (PRM_ROWS - len(rows))).astype(F32)
    mu_lora = jnp.stack([mu_wd, mu_ad] + [jnp.zeros((LANE,), F32)] * 6).astype(F32)
    mu_gate = jnp.concatenate([mu_gd, jnp.zeros((GATE_LORA_PAD - GATE_LORA,), F32)]).reshape(1, GATE_LORA_PAD)
    return prm, mu_lora, mu_gate


def kernel(x, attn_norm, w_in, gdn_conv, gdn_a_log, gdn_dt_bias, gdn_norm, rwkv_mu, rwkv_w0, rwkv_w_up, rwkv_a0, rwkv_a_up, rwkv_g_up, rwkv_k_k, rwkv_k_a, rwkv_r_k, rwkv_ln_w, rwkv_ln_b, w_out, ffn_norm, w_ffn_gate, w_ffn_up, ffn_conv, ffn_conv_b, w_ffn_down, rel_bias, final_norm):
    batch, seq, d = x.shape
    depth = w_in.shape[0]
    outs = []
    for b in range(batch):
        xb = x[b]
        for l in range(depth):
            h = _rmsnorm(xb, attn_norm[l], BF16)
            proj = _in_proj(h, _layout_w_in(w_in[l]))
            o_a = _gdn(proj, gdn_conv[l], gdn_a_log[l], gdn_dt_bias[l], gdn_norm[l])
            o_b = _swa(proj, rel_bias)
            prm, mu_lora, mu_gate = _layout_rwkv_params(
                rwkv_mu[l], rwkv_w0[l], rwkv_a0[l], rwkv_k_k[l], rwkv_k_a[l], rwkv_r_k[l],
                rwkv_ln_w[l], rwkv_ln_b[l])
            g_up = jnp.concatenate(
                [rwkv_g_up[l], jnp.zeros((GATE_LORA_PAD - GATE_LORA, RWKV_WIDTH), F32)], axis=0)
            o_c = _rwkv(proj, prm, mu_lora, mu_gate, rwkv_w_up[l].astype(BF16),
                        rwkv_a_up[l].astype(BF16), g_up.astype(BF16))
            wo = w_out[l].astype(BF16)
            xb = _out_proj(xb, o_a, o_b, o_c, wo[0:GDN_WIDTH], wo[GDN_WIDTH:GDN_WIDTH + SWA_WIDTH],
                           wo[GDN_WIDTH + SWA_WIDTH:])
            h = _rmsnorm(xb, ffn_norm[l], BF16)
            act = _ffn_up(h, w_ffn_gate[l].astype(BF16), w_ffn_up[l].astype(BF16), ffn_conv[l], ffn_conv_b[l])
            xb = _ffn_down(xb, act, w_ffn_down[l].astype(BF16))
        outs.append(_rmsnorm(xb, final_norm, x.dtype))
    return jnp.stack(outs)
```

```python
import functools
import math

import numpy as np
import jax
import jax.numpy as jnp
from jax import lax
from jax.experimental import pallas as pl
from jax.experimental.pallas import tpu as pltpu

F32 = jnp.float32
BF16 = jnp.bfloat16
HIGHEST = lax.Precision.HIGHEST

LANE = 128
D_MODEL = 4096
RMS_EPS = 1e-6
GN_EPS = 64e-5
NEG_INF = -1e30
VMEM_LIMIT = 56 * 1024 * 1024

GDN_DIM = 128
GDN_HEADS = 12
GDN_WIDTH = GDN_HEADS * GDN_DIM
GDN_CONV = 4
GDN_CHUNK = 64
GDN_GROUP = 2
SWA_DIM = 128
SWA_HEADS = 8
SWA_WIDTH = SWA_HEADS * SWA_DIM
SWA_PATTERNS = ((128, 1), (512, 4), (2048, 16))
SWA_BLK = 128
SWA_TILE = 2048
SWA_GROUP = 4
NUM_BUCKETS = 32
MAX_DISTANCE = 2048
RWKV_DIM = 64
RWKV_HEADS = 24
RWKV_WIDTH = RWKV_HEADS * RWKV_DIM
RWKV_CHUNK = 64
DECAY_LORA = 128
ICL_LORA = 128
GATE_LORA = 480
GATE_LORA_PAD = 512
D_FF = 11008
FFN_CONV = 3

COL_GDN_Q, COL_GDN_K, COL_GDN_V, COL_GDN_Z, COL_GDN_BA = 0, 12, 24, 36, 48
COL_SWA_Q, COL_SWA_K, COL_SWA_V = 49, 57, 65
COL_WD, COL_AD = 73, 74
COL_GD = 76
COL_R, COL_K, COL_V = 80, 92, 104
PROJ_BLOCKS = 116
PROJ_WIDTH = PROJ_BLOCKS * LANE


def _cparams(sem):
    return pltpu.CompilerParams(dimension_semantics=sem, vmem_limit_bytes=VMEM_LIMIT)


def _silu(x):
    return x * (1.0 / (1.0 + jnp.exp(-x)))


def _sigmoid(x):
    return 1.0 / (1.0 + jnp.exp(-x))


def _softplus(x):
    return jnp.maximum(x, 0.0) + jnp.log(1.0 + jnp.exp(-jnp.abs(x)))


def _split_bf16(x):
    hi = x.astype(BF16)
    return hi, (x - hi.astype(F32)).astype(BF16)


def _dot(a, b, dims=(((1,), (0,)), ((), ())), mode="bf16"):
    if mode == "f32":
        return lax.dot_general(a.astype(F32), b.astype(F32), dims, precision=HIGHEST,
                               preferred_element_type=F32)
    if mode == "x3":
        a_hi, a_lo = _split_bf16(a)
        b_hi, b_lo = _split_bf16(b)
        mm = lambda p, q: lax.dot_general(p, q, dims, preferred_element_type=F32)
        return mm(a_hi, b_hi) + (mm(a_hi, b_lo) + mm(a_lo, b_hi))
    return lax.dot_general(a.astype(BF16), b.astype(BF16), dims, preferred_element_type=F32)


MODE_CUMSUM = "f32"
MODE_INV = "bf16"
MODE_SOLVE = "bf16"
MODE_STATE = "bf16"


_NT = (((1,), (1,)), ((), ()))
_TN = (((0,), (0,)), ((), ()))


def _rmsnorm_body(x_ref, w_ref, o_ref):
    x = x_ref[...]
    ms = jnp.mean(x * x, axis=-1, keepdims=True)
    o_ref[...] = (x * lax.rsqrt(ms + RMS_EPS) * w_ref[...]).astype(o_ref.dtype)


def _rmsnorm(x, w, out_dtype, tm=512):
    s, d = x.shape
    return pl.pallas_call(
        _rmsnorm_body,
        grid=(s // tm,),
        in_specs=[pl.BlockSpec((tm, d), lambda i: (i, 0)),
                  pl.BlockSpec((1, d), lambda i: (0, 0))],
        out_specs=pl.BlockSpec((tm, d), lambda i: (i, 0)),
        out_shape=jax.ShapeDtypeStruct((s, d), out_dtype),
        compiler_params=_cparams(("parallel",)),
        name="rmsnorm",
    )(x, w.reshape(1, d))


def _matmul_body(a_ref, b_ref, o_ref):
    o_ref[...] = jnp.dot(a_ref[...], b_ref[...], preferred_element_type=F32).astype(o_ref.dtype)


def _in_proj(h, w, tm=1024, tn=512):
    s, k = h.shape
    n = w.shape[1]
    return pl.pallas_call(
        _matmul_body,
        grid=(s // tm, n // tn),
        in_specs=[pl.BlockSpec((tm, k), lambda i, j: (i, 0)),
                  pl.BlockSpec((k, tn), lambda i, j: (0, j))],
        out_specs=pl.BlockSpec((tm, tn), lambda i, j: (i, j)),
        out_shape=jax.ShapeDtypeStruct((s, n), F32),
        compiler_params=_cparams(("parallel", "arbitrary")),
        name="in_proj",
    )(h, w)


def _out_proj_body(x_ref, a_ref, b_ref, c_ref, wa_ref, wb_ref, wc_ref, o_ref):
    acc = jnp.dot(a_ref[...], wa_ref[...], preferred_element_type=F32)
    acc += jnp.dot(b_ref[...], wb_ref[...], preferred_element_type=F32)
    acc += jnp.dot(c_ref[...], wc_ref[...], preferred_element_type=F32)
    o_ref[...] = x_ref[...] + acc


def _out_proj(x, oa, ob, oc, wa, wb, wc, tm=1024, tn=512):
    s, d = x.shape
    ka, kb, kc = oa.shape[1], ob.shape[1], oc.shape[1]
    return pl.pallas_call(
        _out_proj_body,
        grid=(s // tm, d // tn),
        in_specs=[pl.BlockSpec((tm, tn), lambda i, j: (i, j)),
                  pl.BlockSpec((tm, ka), lambda i, j: (i, 0)),
                  pl.BlockSpec((tm, kb), lambda i, j: (i, 0)),
                  pl.BlockSpec((tm, kc), lambda i, j: (i, 0)),
                  pl.BlockSpec((ka, tn), lambda i, j: (0, j)),
                  pl.BlockSpec((kb, tn), lambda i, j: (0, j)),
                  pl.BlockSpec((kc, tn), lambda i, j: (0, j))],
        out_specs=pl.BlockSpec((tm, tn), lambda i, j: (i, j)),
        out_shape=jax.ShapeDtypeStruct((s, d), F32),
        compiler_params=_cparams(("parallel", "arbitrary")),
        name="out_proj",
    )(x, oa, ob, oc, wa, wb, wc)


FFN_HALO = 16


def _ffn_up_body(h_ref, halo_ref, wg_ref, wu_ref, cw_ref, cb_ref, o_ref, hbuf, *, tm):
    i = pl.program_id(0)

    @pl.when(pl.program_id(1) == 0)
    def _stage():
        halo = halo_ref[...]
        hbuf[0:FFN_HALO, :] = jnp.where(i == 0, jnp.zeros_like(halo), halo)
        hbuf[FFN_HALO:FFN_HALO + tm, :] = h_ref[...]

    g = jnp.dot(hbuf[...], wg_ref[...], preferred_element_type=F32)
    u = jnp.dot(hbuf[FFN_HALO:FFN_HALO + tm, :], wu_ref[...], preferred_element_type=F32)
    cw = cw_ref[...]
    conv = (cw[0:1, :] * pltpu.roll(g, 2, 0)[FFN_HALO:, :]
            + cw[1:2, :] * pltpu.roll(g, 1, 0)[FFN_HALO:, :]
            + cw[2:3, :] * g[FFN_HALO:, :]) + cb_ref[...]
    o_ref[...] = (_silu(conv) * u).astype(o_ref.dtype)


def _ffn_up(h, wg, wu, cw, cb, tm=1024, tn=256):
    s, k = h.shape
    f = wg.shape[1]
    halo_blocks = tm // FFN_HALO
    return pl.pallas_call(
        functools.partial(_ffn_up_body, tm=tm),
        grid=(s // tm, f // tn),
        in_specs=[pl.BlockSpec((tm, k), lambda i, j: (i, 0)),
                  pl.BlockSpec((FFN_HALO, k), lambda i, j: (jnp.maximum(i * halo_blocks - 1, 0), 0)),
                  pl.BlockSpec((k, tn), lambda i, j: (0, j)),
                  pl.BlockSpec((k, tn), lambda i, j: (0, j)),
                  pl.BlockSpec((FFN_CONV, tn), lambda i, j: (0, j)),
                  pl.BlockSpec((1, tn), lambda i, j: (0, j))],
        out_specs=pl.BlockSpec((tm, tn), lambda i, j: (i, j)),
        out_shape=jax.ShapeDtypeStruct((s, f), BF16),
        scratch_shapes=[pltpu.VMEM((tm + FFN_HALO, k), BF16)],
        compiler_params=_cparams(("parallel", "arbitrary")),
        name="ffn_up",
    )(h, h, wg, wu, cw, cb.reshape(1, f))


def _ffn_down_body(x_ref, a_ref, w_ref, o_ref, acc_ref):
    kk = pl.program_id(2)

    @pl.when(kk == 0)
    def _init():
        acc_ref[...] = x_ref[...]

    acc_ref[...] += jnp.dot(a_ref[...], w_ref[...], preferred_element_type=F32)

    @pl.when(kk == pl.num_programs(2) - 1)
    def _done():
        o_ref[...] = acc_ref[...]


def _ffn_down(x, act, w, tm=1024, tn=512, ksplit=2):
    s, d = x.shape
    f = act.shape[1]
    tk = f // ksplit
    return pl.pallas_call(
        _ffn_down_body,
        grid=(s // tm, d // tn, ksplit),
        in_specs=[pl.BlockSpec((tm, tn), lambda i, j, kk: (i, j)),
                  pl.BlockSpec((tm, tk), lambda i, j, kk: (i, kk)),
                  pl.BlockSpec((tk, tn), lambda i, j, kk: (kk, j))],
        out_specs=pl.BlockSpec((tm, tn), lambda i, j, kk: (i, j)),
        out_shape=jax.ShapeDtypeStruct((s, d), F32),
        scratch_shapes=[pltpu.VMEM((tm, tn), F32)],
        compiler_params=_cparams(("parallel", "arbitrary", "arbitrary")),
        name="ffn_down",
    )(x, act, w)


def _unit_lower_inverses(n_mats, size):
    row = lax.broadcasted_iota(jnp.int32, (size, size), 0)
    col = lax.broadcasted_iota(jnp.int32, (size, size), 1)
    eye = jnp.where(row == col, 1.0, 0.0).astype(F32)
    invs = [eye + n for n in n_mats]
    powers = [_dot(n, n, mode=MODE_INV) for n in n_mats]
    steps = int(math.log2(size)) - 1
    for s in range(steps):
        last = s == steps - 1
        for i, (inv, p) in enumerate(zip(invs, powers)):
            if last:
                invs[i] = inv + _dot(inv, p, mode=MODE_INV)
            else:
                both = _dot(jnp.concatenate([inv, p], axis=0), p, mode=MODE_INV)
                invs[i] = inv + both[0:size]
                powers[i] = both[size:2 * size]
    return invs


def _gdn_body(alog_ref, dtb_ref, q_ref, k_ref, v_ref, z_ref, ba_ref, cq_ref, ck_ref, cv_ref,
              nw_ref, o_ref, xbuf, state_ref, *, tile):
    c = GDN_CHUNK
    hd = GDN_DIM
    width = GDN_GROUP * hd
    n_chunks = tile // c
    t = pl.program_id(1)

    @pl.when(t == 0)
    def _reset():
        state_ref[...] = jnp.zeros_like(state_ref)
        xbuf[:, 0:8, :] = jnp.zeros((3, 8, width), F32)

    convs = []
    for idx, (src, cw_ref) in enumerate(((q_ref, cq_ref), (k_ref, ck_ref), (v_ref, cv_ref))):
        xbuf[idx, 8:8 + tile, :] = src[...]
        cw = cw_ref[...]
        acc = cw[0:1, :] * xbuf[idx, 5:5 + tile, :]
        acc += cw[1:2, :] * xbuf[idx, 6:6 + tile, :]
        acc += cw[2:3, :] * xbuf[idx, 7:7 + tile, :]
        acc += cw[3:4, :] * xbuf[idx, 8:8 + tile, :]
        convs.append(_silu(acc))
        xbuf[idx, 0:8, :] = xbuf[idx, tile:tile + 8, :]
    q_all, k_all, v_all = convs

    ba = ba_ref[...]
    lane = lax.broadcasted_iota(jnp.int32, (1, LANE), 1)
    ri = lax.broadcasted_iota(jnp.int32, (c, c), 0)
    ci = lax.broadcasted_iota(jnp.int32, (c, c), 1)
    lower = jnp.where(ci <= ri, 1.0, 0.0).astype(F32)
    upper = jnp.where(ri <= ci, 1.0, 0.0).astype(F32)
    r2 = lax.broadcasted_iota(jnp.int32, (2 * c, c), 0)
    c2 = lax.broadcasted_iota(jnp.int32, (2 * c, c), 1)
    band = c2 <= jnp.where(r2 < c, r2 - 1, r2 - c)
    nw = nw_ref[...]

    units = []
    for j in range(GDN_GROUP):
        h = pl.program_id(0) * GDN_GROUP + j
        cols = slice(j * hd, (j + 1) * hd)
        q, k, v = q_all[:, cols], k_all[:, cols], v_all[:, cols]
        q = q * lax.rsqrt(jnp.sum(q * q, axis=-1, keepdims=True) + 1e-6) * (hd ** -0.5)
        k = k * lax.rsqrt(jnp.sum(k * k, axis=-1, keepdims=True) + 1e-6)
        b_col = jnp.sum(jnp.where(lane == h, ba, 0.0), axis=-1, keepdims=True)
        a_col = jnp.sum(jnp.where(lane == h + GDN_HEADS, ba, 0.0), axis=-1, keepdims=True)
        beta = _sigmoid(b_col)
        g = -jnp.exp(alog_ref[h]) * _softplus(a_col + dtb_ref[h])
        g_b = g * jnp.ones((1, hd), F32)
        g_col = jnp.concatenate([_dot(lower, g_b[n * c:(n + 1) * c], mode=MODE_CUMSUM)
                                 for n in range(n_chunks)], axis=0)
        g_row = jnp.concatenate([_dot(jnp.ones((8, c), F32), g_b[n * c:(n + 1) * c, 0:c] * upper,
                                      mode=MODE_CUMSUM) for n in range(n_chunks)], axis=1)
        exp_g = jnp.exp(g_col)
        kb = k * beta
        vb = v * beta
        kbe = kb * exp_g
        qg = q * exp_g
        for n in range(n_chunks):
            rows = slice(n * c, (n + 1) * c)
            g_last = g_col[n * c + c - 1:n * c + c, :]
            gamma = jnp.exp(jnp.minimum(g_col[rows, 0:c] - g_row[0:1, n * c:(n + 1) * c], 0.0))
            units.append(dict(j=j, n=n, k=k[rows], kq=jnp.concatenate([kb[rows], q[rows]], axis=0),
                              rhs=jnp.concatenate([vb[rows], kbe[rows]], axis=1), qg=qg[rows],
                              kd=k[rows] * jnp.exp(g_last - g_col[rows]), decay=jnp.exp(g_last),
                              gamma2=jnp.concatenate([gamma, gamma], axis=0)))
    for un in units:
        un["scores"] = jnp.where(band, _dot(un["kq"], un["k"], _NT) * un["gamma2"], 0.0)
    t_invs = _unit_lower_inverses([-un["scores"][0:c] for un in units], c)
    for un, t_inv in zip(units, t_invs):
        un["uw"] = _dot(t_inv, un["rhs"], mode=MODE_SOLVE)

    for un in units:
        un["trans"] = _dot(un["kd"], un["uw"][:, hd:2 * hd], _TN, mode=MODE_STATE)
        un["const"] = _dot(un["kd"], un["uw"][:, 0:hd], _TN, mode=MODE_STATE)
    states = [state_ref[j] for j in range(GDN_GROUP)]
    for n in range(n_chunks):
        for un in units:
            if un["n"] != n:
                continue
            j = un["j"]
            un["s0"] = states[j]
            states[j] = (states[j] * un["decay"] - _dot(un["trans"], states[j], mode=MODE_STATE)
                         + un["const"])
    for j in range(GDN_GROUP):
        state_ref[j] = states[j]
    outs = [[None] * n_chunks for _ in range(GDN_GROUP)]
    for un in units:
        un["ws"] = _dot(jnp.concatenate([un["uw"][:, hd:2 * hd], un["qg"]], axis=0), un["s0"],
                        mode=MODE_STATE)
    for un in units:
        v_new = un["uw"][:, 0:hd] - un["ws"][0:c]
        outs[un["j"]][un["n"]] = un["ws"][c:2 * c] + _dot(un["scores"][c:2 * c], v_new, mode=MODE_STATE)
    for j in range(GDN_GROUP):
        cols = slice(j * hd, (j + 1) * hd)
        o = jnp.concatenate(outs[j], axis=0)
        o = o * lax.rsqrt(jnp.mean(o * o, axis=-1, keepdims=True) + RMS_EPS) * nw
        o_ref[:, cols] = (o * _silu(z_ref[:, cols])).astype(o_ref.dtype)


def _gdn(proj, conv_w, a_log, dt_bias, norm_w, tile=512):
    s = proj.shape[0]
    width = GDN_GROUP * GDN_DIM
    per = LANE // GDN_DIM * GDN_GROUP
    blk = lambda col: pl.BlockSpec((tile, width), lambda h, t: (t, col // per + h))
    cblk = lambda col: pl.BlockSpec((GDN_CONV, width), lambda h, t: (0, col // per + h))
    smem = pl.BlockSpec(memory_space=pltpu.SMEM)
    return pl.pallas_call(
        functools.partial(_gdn_body, tile=tile),
        grid=(GDN_HEADS // GDN_GROUP, s // tile),
        in_specs=[smem, smem,
                  blk(COL_GDN_Q), blk(COL_GDN_K), blk(COL_GDN_V), blk(COL_GDN_Z),
                  pl.BlockSpec((tile, LANE), lambda h, t: (t, COL_GDN_BA)),
                  cblk(0), cblk(GDN_HEADS), cblk(2 * GDN_HEADS),
                  pl.BlockSpec((1, GDN_DIM), lambda h, t: (0, 0))],
        out_specs=pl.BlockSpec((tile, width), lambda h, t: (t, h)),
        out_shape=jax.ShapeDtypeStruct((s, GDN_WIDTH), BF16),
        scratch_shapes=[pltpu.VMEM((3, tile + 8, width), F32),
                        pltpu.VMEM((GDN_GROUP, GDN_DIM, GDN_DIM), F32)],
        compiler_params=_cparams(("parallel", "arbitrary")),
        name="gdn",
    )(a_log, dt_bias, proj, proj, proj, proj, proj, conv_w, conv_w, conv_w, norm_w.reshape(1, GDN_DIM))


def _t5_bucket_table():
    exact = NUM_BUCKETS // 2
    i = np.arange(SWA_BLK)[:, None]
    j = np.arange(2 * SWA_BLK)[None, :]
    steps = np.maximum(i + SWA_BLK - j, 0)
    tables = []
    for _, dilation in SWA_PATTERNS:
        dist = steps * dilation
        d = np.maximum(dist, 1).astype(np.float32)
        ratio = (np.log(d / np.float32(exact)) / np.float32(math.log(MAX_DISTANCE / exact))
                 * np.float32(NUM_BUCKETS - exact)).astype(np.float32)
        log_b = exact + ratio.astype(np.int32)
        tables.append(np.where(dist < exact, dist, np.minimum(log_b, NUM_BUCKETS - 1)))
    return np.stack(tables).astype(np.int32)


def _swa_body(rb_ref, bkt_ref, q_ref, kc_ref, kp_ref, vc_ref, vp_ref, o_ref,
              bias_ref, kbuf, vbuf, o_scr, m_scr, l_scr, *, tile):
    h = pl.program_id(0)
    t = pl.program_id(1)
    blk = SWA_BLK

    @pl.when(t == 0)
    def _bias():
        for p in range(len(SWA_PATTERNS)):
            bkt = bkt_ref[p]
            bias = jnp.zeros((blk, 2 * blk), F32)
            for b in range(NUM_BUCKETS):
                bias = jnp.where(bkt == b, rb_ref[b, h], bias)
            bias_ref[p] = bias

    kbuf[0:tile, :] = kp_ref[...]
    kbuf[tile:2 * tile, :] = kc_ref[...]
    vbuf[0:tile, :] = vp_ref[...]
    vbuf[tile:2 * tile, :] = vc_ref[...]

    qi = lax.broadcasted_iota(jnp.int32, (blk, 2 * blk), 0)
    kj = lax.broadcasted_iota(jnp.int32, (blk, 2 * blk), 1)
    in_band = (kj >= qi) & (kj <= qi + blk)
    scale = SWA_DIM ** -0.5

    band_first = in_band & (kj >= jnp.where(t == 0, blk, 0))
    ones_row = jnp.ones((1, SWA_DIM), F32)

    for p, (window, dil) in enumerate(SWA_PATTERNS):
        span = blk * dil
        bias = bias_ref[p]
        blocks = [(res + n * span, n) for n in range(tile // span) for res in range(dil)]
        for g0 in range(0, len(blocks), SWA_GROUP):
            group = blocks[g0:g0 + SWA_GROUP]
            scores = []
            for start, n in group:
                qb = q_ref[pl.ds(start, blk, stride=dil), :] * scale
                kw = kbuf[pl.ds(tile + start - span, 2 * blk, stride=dil), :]
                sc = _dot(qb, kw, _NT) + bias
                scores.append(jnp.where(band_first if n == 0 else in_band, sc, NEG_INF))
            probs = []
            for sc in scores:
                m = jnp.max(sc, axis=-1, keepdims=True)
                pe = jnp.exp(sc - m)
                probs.append((pe, m, jnp.sum(pe, axis=-1, keepdims=True)))
            for (start, n), (pe, m, l) in zip(group, probs):
                vw = vbuf[pl.ds(tile + start - span, 2 * blk, stride=dil), :]
                o_scr[p, pl.ds(start, blk, stride=dil), :] = _dot(pe, vw)
                m_scr[p, pl.ds(start, blk, stride=dil), :] = m * ones_row
                l_scr[p, pl.ds(start, blk, stride=dil), :] = l * ones_row

    m_max = jnp.maximum(jnp.maximum(m_scr[0], m_scr[1]), m_scr[2])
    num = jnp.zeros((tile, SWA_DIM), F32)
    den = jnp.zeros((tile, SWA_DIM), F32)
    for p in range(len(SWA_PATTERNS)):
        sc = jnp.exp(m_scr[p] - m_max)
        num += o_scr[p] * sc
        den += l_scr[p] * sc
    o_ref[...] = (num / den).astype(o_ref.dtype)


def _swa(proj, rel_bias, tile=SWA_TILE):
    s = proj.shape[0]
    n_pat = len(SWA_PATTERNS)
    bkt = jnp.asarray(_t5_bucket_table())
    cur = lambda col: pl.BlockSpec((tile, LANE), lambda h, t: (t, col + h))
    prev = lambda col: pl.BlockSpec((tile, LANE), lambda h, t: (jnp.maximum(t - 1, 0), col + h))
    return pl.pallas_call(
        functools.partial(_swa_body, tile=tile),
        grid=(SWA_HEADS, s // tile),
        in_specs=[pl.BlockSpec(memory_space=pltpu.SMEM),
                  pl.BlockSpec((n_pat, SWA_BLK, 2 * SWA_BLK), lambda h, t: (0, 0, 0)),
                  cur(COL_SWA_Q), cur(COL_SWA_K), prev(COL_SWA_K), cur(COL_SWA_V), prev(COL_SWA_V)],
        out_specs=pl.BlockSpec((tile, LANE), lambda h, t: (t, h)),
        out_shape=jax.ShapeDtypeStruct((s, SWA_WIDTH), BF16),
        scratch_shapes=[pltpu.VMEM((n_pat, SWA_BLK, 2 * SWA_BLK), F32),
                        pltpu.VMEM((2 * tile, SWA_DIM), F32),
                        pltpu.VMEM((2 * tile, SWA_DIM), F32),
                        pltpu.VMEM((n_pat, tile, SWA_DIM), F32),
                        pltpu.VMEM((n_pat, tile, SWA_DIM), F32),
                        pltpu.VMEM((n_pat, tile, SWA_DIM), F32)],
        compiler_params=_cparams(("parallel", "arbitrary")),
        name="swa",
    )(rel_bias, bkt, proj, proj, proj, proj, proj)


PRM_W0, PRM_A0, PRM_KK, PRM_KA, PRM_RK, PRM_LNW, PRM_LNB, PRM_MUR, PRM_MUK, PRM_MUV = range(10)
PRM_ROWS = 16


def _rwkv_body(r_ref, k_ref, v_ref, wd_ref, ad_ref, gd_ref, prm_ref, mul_ref, mug_ref,
               wup_ref, aup_ref, gup_ref, o_ref, xb, xg, state_ref, *, tile):
    c = RWKV_CHUNK
    nd = RWKV_DIM
    t = pl.program_id(1)

    @pl.when(t == 0)
    def _reset():
        state_ref[...] = jnp.zeros_like(state_ref)
        xb[:, 0:8, :] = jnp.zeros((5, 8, LANE), F32)
        xg[0:8, :] = jnp.zeros((8, GATE_LORA_PAD), F32)

    prm = prm_ref[...]
    row = lambda i: prm[i:i + 1, :]

    def shifted(idx, src, mu):
        xb[idx, 8:8 + tile, :] = src[...]
        cur = xb[idx, 8:8 + tile, :]
        prev = xb[idx, 7:7 + tile, :]
        xb[idx, 0:8, :] = xb[idx, tile:tile + 8, :]
        return cur + mu * (prev - cur)

    r = shifted(0, r_ref, row(PRM_MUR))
    k = shifted(1, k_ref, row(PRM_MUK))
    v = shifted(2, v_ref, row(PRM_MUV))
    wd = shifted(3, wd_ref, mul_ref[0:1, :])
    ad = shifted(4, ad_ref, mul_ref[1:2, :])
    xg[8:8 + tile, :] = gd_ref[...]
    gcur = xg[8:8 + tile, :]
    gprev = xg[7:7 + tile, :]
    xg[0:8, :] = xg[tile:tile + 8, :]
    gd = gcur + mug_ref[...] * (gprev - gcur)

    w_log = -_softplus(-(row(PRM_W0) + _dot(jnp.tanh(wd), wup_ref[...]))) - 0.5
    log_w = -jnp.exp(w_log)
    a_gate = _sigmoid(row(PRM_A0) + _dot(ad, aup_ref[...]))
    g_gate = _dot(_sigmoid(gd), gup_ref[...])
    kx = k * row(PRM_KK)
    k_mod = k * (1.0 + (a_gate - 1.0) * row(PRM_KA))

    ri = lax.broadcasted_iota(jnp.int32, (c, c), 0)
    ci = lax.broadcasted_iota(jnp.int32, (c, c), 1)
    lower = jnp.where(ci <= ri, 1.0, 0.0).astype(F32)
    cum = jnp.concatenate([_dot(lower, log_w[n * c:(n + 1) * c], mode=MODE_CUMSUM)
                           for n in range(tile // c)], axis=0)

    r2 = lax.broadcasted_iota(jnp.int32, (2 * c, 2 * c), 0)
    c2 = lax.broadcasted_iota(jnp.int32, (2 * c, 2 * c), 1)
    col_in = jnp.where(c2 < c, c2, c2 - c)
    band = col_in <= jnp.where(r2 < c, r2 - 1, r2 - c)

    n_heads = LANE // nd
    n_chunks = tile // c
    lane = lax.broadcasted_iota(jnp.int32, (1, LANE), 1)
    head_masks = [(lane // nd) == hh for hh in range(n_heads)]

    def head_sum(x):
        out = jnp.zeros_like(x)
        for msk in head_masks:
            out = jnp.where(msk, jnp.sum(jnp.where(msk, x, 0.0), axis=-1, keepdims=True), out)
        return out

    kk = kx * lax.rsqrt(head_sum(kx * kx) + 1e-12)
    b_vec = kk * a_gate
    c_last = jnp.concatenate(
        [jnp.broadcast_to(cum[n * c + c - 1:n * c + c, :], (c, LANE)) for n in range(n_chunks)], axis=0)
    p_out = jnp.exp(-cum)
    a_t = -kk * jnp.exp(cum - log_w)
    r_t = r * jnp.exp(cum)
    k_t = k_mod * p_out
    b_t = b_vec * p_out
    decay_end = jnp.exp(c_last - cum)
    k_e = k_mod * decay_end
    b_e = b_vec * decay_end
    decay_all = jnp.exp(c_last)

    units = []
    for hh in range(n_heads):
        lanes = slice(hh * nd, (hh + 1) * nd)
        for n in range(n_chunks):
            rows = slice(n * c, (n + 1) * c)
            units.append(dict(hh=hh, n=n, a=a_t[rows, lanes], r=r_t[rows, lanes], v=v[rows, lanes],
                              ar=jnp.concatenate([a_t[rows, lanes], r_t[rows, lanes]], axis=0),
                              bk=jnp.concatenate([b_t[rows, lanes], k_t[rows, lanes]], axis=0),
                              bk_e=jnp.concatenate([b_e[rows, lanes], k_e[rows, lanes]], axis=0),
                              decay=decay_all[n * c:n * c + 1, lanes]))
    for un in units:
        un["mm"] = jnp.where(band, _dot(un["ar"], un["bk"], _NT), 0.0)
    t_invs = _unit_lower_inverses([un["mm"][0:c, 0:c] for un in units], c)
    for un in units:
        un["mv"] = _dot(un["mm"][0:c, c:2 * c], un["v"], mode=MODE_SOLVE)
    for un, t_inv in zip(units, t_invs):
        un["wu"] = _dot(t_inv, jnp.concatenate([un["a"], un["mv"]], axis=1), mode=MODE_SOLVE)

    for un in units:
        un["trans"] = _dot(un["wu"][:, 0:nd], un["bk_e"][0:c], _TN, mode=MODE_STATE)
        un["const"] = _dot(jnp.concatenate([un["wu"][:, nd:2 * nd], un["v"]], axis=0), un["bk_e"], _TN,
                           mode=MODE_STATE)
    states = [state_ref[hh] for hh in range(n_heads)]
    for n in range(n_chunks):
        for un in units:
            if un["n"] != n:
                continue
            hh = un["hh"]
            un["s0"] = states[hh]
            states[hh] = (states[hh] * un["decay"] + _dot(states[hh], un["trans"], mode=MODE_STATE)
                          + un["const"])
    for hh in range(n_heads):
        state_ref[hh] = states[hh]
    ys = [[None] * n_chunks for _ in range(n_heads)]
    for un in units:
        un["sr"] = _dot(jnp.concatenate([un["wu"][:, 0:nd], un["r"]], axis=0), un["s0"], _NT,
                        mode=MODE_STATE)
    for un in units:
        u = un["sr"][0:c] + un["wu"][:, nd:2 * nd]
        uv = jnp.concatenate([u, un["v"]], axis=0)
        ys[un["hh"]][un["n"]] = un["sr"][c:2 * c] + _dot(un["mm"][c:2 * c], uv, mode=MODE_STATE)
    y = jnp.concatenate([jnp.concatenate(ys[hh], axis=0) for hh in range(n_heads)], axis=1)
    mean = head_sum(y) * (1.0 / nd)
    var = head_sum(jnp.square(y - mean)) * (1.0 / nd)
    y_n = (y - mean) * lax.rsqrt(var + GN_EPS) * row(PRM_LNW) + row(PRM_LNB)
    bonus = head_sum(r * k_mod * row(PRM_RK)) * v
    o_ref[...] = ((y_n + bonus) * g_gate).astype(o_ref.dtype)


def _rwkv(proj, prm, mu_lora, mu_gate, w_up, a_up, g_up, tile=512):
    s = proj.shape[0]
    pairs = RWKV_WIDTH // LANE
    blk = lambda col: pl.BlockSpec((tile, LANE), lambda h, t: (t, col + h))
    fixed = lambda col: pl.BlockSpec((tile, LANE), lambda h, t: (t, col))
    return pl.pallas_call(
        functools.partial(_rwkv_body, tile=tile),
        grid=(pairs, s // tile),
        in_specs=[blk(COL_R), blk(COL_K), blk(COL_V), fixed(COL_WD), fixed(COL_AD),
                  pl.BlockSpec((tile, GATE_LORA_PAD), lambda h, t: (t, COL_GD * LANE // GATE_LORA_PAD)),
                  pl.BlockSpec((PRM_ROWS, LANE), lambda h, t: (0, h)),
                  pl.BlockSpec((8, LANE), lambda h, t: (0, 0)),
                  pl.BlockSpec((1, GATE_LORA_PAD), lambda h, t: (0, 0)),
                  pl.BlockSpec((DECAY_LORA, LANE), lambda h, t: (0, h)),
                  pl.BlockSpec((ICL_LORA, LANE), lambda h, t: (0, h)),
                  pl.BlockSpec((GATE_LORA_PAD, LANE), lambda h, t: (0, h))],
        out_specs=pl.BlockSpec((tile, LANE), lambda h, t: (t, h)),
        out_shape=jax.ShapeDtypeStruct((s, RWKV_WIDTH), BF16),
        scratch_shapes=[pltpu.VMEM((5, tile + 8, LANE), F32),
                        pltpu.VMEM((tile + 8, GATE_LORA_PAD), F32),
                        pltpu.VMEM((LANE // RWKV_DIM, RWKV_DIM, RWKV_DIM), F32)],
        compiler_params=_cparams(("parallel", "arbitrary")),
        name="rwkv7",
    )(proj, proj, proj, proj, proj, proj, prm, mu_lora, mu_gate, w_up, a_up, g_up)


def _layout_w_in(w):
    d = w.shape[0]
    gw, sw, rw = GDN_WIDTH, SWA_WIDTH, RWKV_WIDTH
    o = 0
    gdn_main = w[:, o:o + 4 * gw]; o += 4 * gw
    gdn_ba = w[:, o:o + 2 * GDN_HEADS]; o += 2 * GDN_HEADS
    swa = w[:, o:o + 3 * sw]; o += 3 * sw
    rkv = w[:, o:o + 3 * rw]; o += 3 * rw
    wd = w[:, o:o + DECAY_LORA]; o += DECAY_LORA
    ad = w[:, o:o + ICL_LORA]; o += ICL_LORA
    gd = w[:, o:o + GATE_LORA]
    z = lambda n: jnp.zeros((d, n), w.dtype)
    out = jnp.concatenate([gdn_main, gdn_ba, z(LANE - 2 * GDN_HEADS), swa, wd, ad, z(LANE),
                           gd, z(GATE_LORA_PAD - GATE_LORA), rkv], axis=1)
    assert out.shape[1] == PROJ_WIDTH
    return out.astype(BF16)


def _layout_rwkv_params(mu, w0, a0, k_k, k_a, r_k, ln_w, ln_b):
    rw = RWKV_WIDTH
    mu_r, mu_k, mu_v = mu[0:rw], mu[rw:2 * rw], mu[2 * rw:3 * rw]
    o = 3 * rw
    mu_wd = mu[o:o + DECAY_LORA]; o += DECAY_LORA
    mu_ad = mu[o:o + ICL_LORA]; o += ICL_LORA
    mu_gd = mu[o:o + GATE_LORA]
    rows = [w0, a0, k_k, k_a, r_k.reshape(rw), ln_w, ln_b, mu_r, mu_k, mu_v]
    prm = jnp.stack(rows + [jnp.zeros((rw,), F32)] * (PRM_ROWS - len(rows))).astype(F32)
    mu_lora = jnp.stack([mu_wd, mu_ad] + [jnp.zeros((LANE,), F32)] * 6).astype(F32)
    mu_gate = jnp.concatenate([mu_gd, jnp.zeros((GATE_LORA_PAD - GATE_LORA,), F32)]).reshape(1, GATE_LORA_PAD)
    return prm, mu_lora, mu_gate


def kernel(x, attn_norm, w_in, gdn_conv, gdn_a_log, gdn_dt_bias, gdn_norm, rwkv_mu, rwkv_w0, rwkv_w_up, rwkv_a0, rwkv_a_up, rwkv_g_up, rwkv_k_k, rwkv_k_a, rwkv_r_k, rwkv_ln_w, rwkv_ln_b, w_out, ffn_norm, w_ffn_gate, w_ffn_up, ffn_conv, ffn_conv_b, w_ffn_down, rel_bias, final_norm):
    batch, seq, d = x.shape
    depth = w_in.shape[0]
    outs = []
    for b in range(batch):
        xb = x[b]
        for l in range(depth):
            h = _rmsnorm(xb, attn_norm[l], BF16)
            proj = _in_proj(h, _layout_w_in(w_in[l]))
            o_a = _gdn(proj, gdn_conv[l], gdn_a_log[l], gdn_dt_bias[l], gdn_norm[l])
            o_b = _swa(proj, rel_bias)
            prm, mu_lora, mu_gate = _layout_rwkv_params(
                rwkv_mu[l], rwkv_w0[l], rwkv_a0[l], rwkv_k_k[l], rwkv_k_a[l], rwkv_r_k[l],
                rwkv_ln_w[l], rwkv_ln_b[l])
            g_up = jnp.concatenate(
                [rwkv_g_up[l], jnp.zeros((GATE_LORA_PAD - GATE_LORA, RWKV_WIDTH), F32)], axis=0)
            o_c = _rwkv(proj, prm, mu_lora, mu_gate, rwkv_w_up[l].astype(BF16),
                        rwkv_a_up[l].astype(BF16), g_up.astype(BF16))
            wo = w_out[l].astype(BF16)
            xb = _out_proj(xb, o_a, o_b, o_c, wo[0:GDN_WIDTH], wo[GDN_WIDTH:GDN_WIDTH + SWA_WIDTH],
                           wo[GDN_WIDTH + SWA_WIDTH:])
            h = _rmsnorm(xb, ffn_norm[l], BF16)
            act = _ffn_up(h, w_ffn_gate[l].astype(BF16), w_ffn_up[l].astype(BF16), ffn_conv[l], ffn_conv_b[l])
            xb = _ffn_down(xb, act, w_ffn_down[l].astype(BF16))
        outs.append(_rmsnorm(xb, final_norm, x.dtype))
    return jnp.stack(outs)
```

```python
import functools
import math

import numpy as np
import jax
import jax.numpy as jnp
from jax import lax
from jax.experimental import pallas as pl
from jax.experimental.pallas import tpu as pltpu

F32 = jnp.float32
BF16 = jnp.bfloat16
HIGHEST = lax.Precision.HIGHEST

LANE = 128
D_MODEL = 4096
RMS_EPS = 1e-6
GN_EPS = 64e-5
NEG_INF = -1e30
VMEM_LIMIT = 56 * 1024 * 1024

GDN_DIM = 128
GDN_HEADS = 12
GDN_WIDTH = GDN_HEADS * GDN_DIM
GDN_CONV = 4
GDN_CHUNK = 64
GDN_GROUP = 2
SWA_DIM = 128
SWA_HEADS = 8
SWA_WIDTH = SWA_HEADS * SWA_DIM
SWA_PATTERNS = ((128, 1), (512, 4), (2048, 16))
SWA_BLK = 128
SWA_TILE = 2048
SWA_GROUP = 4
NUM_BUCKETS = 32
MAX_DISTANCE = 2048
RWKV_DIM = 64
RWKV_HEADS = 24
RWKV_WIDTH = RWKV_HEADS * RWKV_DIM
RWKV_CHUNK = 64
DECAY_LORA = 128
ICL_LORA = 128
GATE_LORA = 480
GATE_LORA_PAD = 512
D_FF = 11008
FFN_CONV = 3

COL_GDN_Q, COL_GDN_K, COL_GDN_V, COL_GDN_Z, COL_GDN_BA = 0, 12, 24, 36, 48
COL_SWA_Q, COL_SWA_K, COL_SWA_V = 49, 57, 65
COL_WD, COL_AD = 73, 74
COL_GD = 76
COL_R, COL_K, COL_V = 80, 92, 104
PROJ_BLOCKS = 116
PROJ_WIDTH = PROJ_BLOCKS * LANE


def _cparams(sem):
    return pltpu.CompilerParams(dimension_semantics=sem, vmem_limit_bytes=VMEM_LIMIT)


def _silu(x):
    return x * (1.0 / (1.0 + jnp.exp(-x)))


def _sigmoid(x):
    return 1.0 / (1.0 + jnp.exp(-x))


def _softplus(x):
    return jnp.maximum(x, 0.0) + jnp.log(1.0 + jnp.exp(-jnp.abs(x)))


def _split_bf16(x):
    hi = x.astype(BF16)
    return hi, (x - hi.astype(F32)).astype(BF16)


def _dot(a, b, dims=(((1,), (0,)), ((), ())), mode="bf16"):
    if mode == "f32":
        return lax.dot_general(a.astype(F32), b.astype(F32), dims, precision=HIGHEST,
                               preferred_element_type=F32)
    if mode == "x3":
        a_hi, a_lo = _split_bf16(a)
        b_hi, b_lo = _split_bf16(b)
        mm = lambda p, q: lax.dot_general(p, q, dims, preferred_element_type=F32)
        return mm(a_hi, b_hi) + (mm(a_hi, b_lo) + mm(a_lo, b_hi))
    return lax.dot_general(a.astype(BF16), b.astype(BF16), dims, preferred_element_type=F32)


MODE_CUMSUM = "f32"
MODE_INV = "bf16"
MODE_SOLVE = "bf16"
MODE_STATE = "bf16"


_NT = (((1,), (1,)), ((), ()))
_TN = (((0,), (0,)), ((), ()))


def _rmsnorm_body(x_ref, w_ref, o_ref):
    x = x_ref[...]
    ms = jnp.mean(x * x, axis=-1, keepdims=True)
    o_ref[...] = (x * lax.rsqrt(ms + RMS_EPS) * w_ref[...]).astype(o_ref.dtype)


def _rmsnorm(x, w, out_dtype, tm=512):
    s, d = x.shape
    return pl.pallas_call(
        _rmsnorm_body,
        grid=(s // tm,),
        in_specs=[pl.BlockSpec((tm, d), lambda i: (i, 0)),
                  pl.BlockSpec((1, d), lambda i: (0, 0))],
        out_specs=pl.BlockSpec((tm, d), lambda i: (i, 0)),
        out_shape=jax.ShapeDtypeStruct((s, d), out_dtype),
        compiler_params=_cparams(("parallel",)),
        name="rmsnorm",
    )(x, w.reshape(1, d))


def _cast_body(w_ref, o_ref):
    o_ref[...] = w_ref[...].astype(o_ref.dtype)


def _to_bf16(w, layer, rows=256):
    _, r, c = w.shape
    return pl.pallas_call(
        _cast_body,
        grid=(r // rows,),
        in_specs=[pl.BlockSpec((None, rows, c), lambda i: (layer, i, 0))],
        out_specs=pl.BlockSpec((rows, c), lambda i: (i, 0)),
        out_shape=jax.ShapeDtypeStruct((r, c), BF16),
        compiler_params=_cparams(("parallel",)),
        name="to_bf16",
    )(w)


def _matmul_body(a_ref, b_ref, o_ref):
    o_ref[...] = jnp.dot(a_ref[...], b_ref[...], preferred_element_type=F32).astype(o_ref.dtype)


def _in_proj(h, w, tm=1024, tn=512):
    s, k = h.shape
    n = w.shape[1]
    return pl.pallas_call(
        _matmul_body,
        grid=(s // tm, n // tn),
        in_specs=[pl.BlockSpec((tm, k), lambda i, j: (i, 0)),
                  pl.BlockSpec((k, tn), lambda i, j: (0, j))],
        out_specs=pl.BlockSpec((tm, tn), lambda i, j: (i, j)),
        out_shape=jax.ShapeDtypeStruct((s, n), F32),
        compiler_params=_cparams(("parallel", "arbitrary")),
        name="in_proj",
    )(h, w)


def _out_proj_body(x_ref, a_ref, b_ref, c_ref, wa_ref, wb_ref, wc_ref, o_ref):
    acc = jnp.dot(a_ref[...], wa_ref[...], preferred_element_type=F32)
    acc += jnp.dot(b_ref[...], wb_ref[...], preferred_element_type=F32)
    acc += jnp.dot(c_ref[...], wc_ref[...], preferred_element_type=F32)
    o_ref[...] = x_ref[...] + acc


def _out_proj(x, oa, ob, oc, wa, wb, wc, tm=1024, tn=512):
    s, d = x.shape
    ka, kb, kc = oa.shape[1], ob.shape[1], oc.shape[1]
    return pl.pallas_call(
        _out_proj_body,
        grid=(s // tm, d // tn),
        in_specs=[pl.BlockSpec((tm, tn), lambda i, j: (i, j)),
                  pl.BlockSpec((tm, ka), lambda i, j: (i, 0)),
                  pl.BlockSpec((tm, kb), lambda i, j: (i, 0)),
                  pl.BlockSpec((tm, kc), lambda i, j: (i, 0)),
                  pl.BlockSpec((ka, tn), lambda i, j: (0, j)),
                  pl.BlockSpec((kb, tn), lambda i, j: (0, j)),
                  pl.BlockSpec((kc, tn), lambda i, j: (0, j))],
        out_specs=pl.BlockSpec((tm, tn), lambda i, j: (i, j)),
        out_shape=jax.ShapeDtypeStruct((s, d), F32),
        compiler_params=_cparams(("parallel", "arbitrary")),
        name="out_proj",
    )(x, oa, ob, oc, wa, wb, wc)


FFN_HALO = 16
FFN_EPI_ROWS = 128


def _ffn_up_body(h_ref, halo_ref, wg_ref, wu_ref, cw_ref, cb_ref, o_ref, hbuf, g0, u0, g1, u1, *, tm):
    i = pl.program_id(0)
    j = pl.program_id(1)

    @pl.when(j == 0)
    def _stage():
        halo = halo_ref[...]
        hbuf[0:FFN_HALO, :] = jnp.where(i == 0, jnp.zeros_like(halo), halo)
        hbuf[FFN_HALO:FFN_HALO + tm, :] = h_ref[...]

    @pl.when(jnp.logical_and(i == 0, j == 0))
    def _nothing_pending():
        g1[...] = jnp.zeros_like(g1)
        u1[...] = jnp.zeros_like(u1)

    def step(g_new, u_new, g_old, u_old):
        cw = cw_ref[...]
        cb = cb_ref[...]
        for r0 in range(0, tm, FFN_EPI_ROWS):
            rows = FFN_EPI_ROWS
            conv = (cw[0:1, :] * g_old[pl.ds(FFN_HALO - 2 + r0, rows), :]
                    + cw[1:2, :] * g_old[pl.ds(FFN_HALO - 1 + r0, rows), :]
                    + cw[2:3, :] * g_old[pl.ds(FFN_HALO + r0, rows), :]) + cb
            o_ref[pl.ds(r0, rows), :] = (_silu(conv) * u_old[pl.ds(r0, rows), :]).astype(o_ref.dtype)
        g_new[...] = jnp.dot(hbuf[...], wg_ref[...], preferred_element_type=F32)
        u_new[...] = jnp.dot(hbuf[FFN_HALO:FFN_HALO + tm, :], wu_ref[...], preferred_element_type=F32)

    @pl.when(j % 2 == 0)
    def _even():
        step(g0, u0, g1, u1)

    @pl.when(j % 2 == 1)
    def _odd():
        step(g1, u1, g0, u0)


def _ffn_up(h, wg, wu, cw, cb, tm=1024, tn=256):
    s, k = h.shape
    f = wg.shape[1]
    nj = f // tn
    assert (nj + 1) % 2 == 0
    halo_blocks = tm // FFN_HALO
    cur = lambda i, j: (0, jnp.minimum(j, nj - 1))
    done = lambda i, j: (0, jnp.maximum(j - 1, 0))
    return pl.pallas_call(
        functools.partial(_ffn_up_body, tm=tm),
        grid=(s // tm, nj + 1),
        in_specs=[pl.BlockSpec((tm, k), lambda i, j: (i, 0)),
                  pl.BlockSpec((FFN_HALO, k), lambda i, j: (jnp.maximum(i * halo_blocks - 1, 0), 0)),
                  pl.BlockSpec((k, tn), cur),
                  pl.BlockSpec((k, tn), cur),
                  pl.BlockSpec((FFN_CONV, tn), done),
                  pl.BlockSpec((1, tn), done)],
        out_specs=pl.BlockSpec((tm, tn), lambda i, j: (i, jnp.maximum(j - 1, 0))),
        out_shape=jax.ShapeDtypeStruct((s, f), BF16),
        scratch_shapes=[pltpu.VMEM((tm + FFN_HALO, k), BF16),
                        pltpu.VMEM((tm + FFN_HALO, tn), F32), pltpu.VMEM((tm, tn), F32),
                        pltpu.VMEM((tm + FFN_HALO, tn), F32), pltpu.VMEM((tm, tn), F32)],
        compiler_params=_cparams(("arbitrary", "arbitrary")),
        name="ffn_up",
    )(h, h, wg, wu, cw, cb.reshape(1, f))


def _ffn_down_body(x_ref, a_ref, w_ref, o_ref, acc_ref):
    kk = pl.program_id(2)

    @pl.when(kk == 0)
    def _init():
        acc_ref[...] = x_ref[...]

    acc_ref[...] += jnp.dot(a_ref[...], w_ref[...], preferred_element_type=F32)

    @pl.when(kk == pl.num_programs(2) - 1)
    def _done():
        o_ref[...] = acc_ref[...]


def _ffn_down(x, act, w, tm=1024, tn=512, ksplit=2):
    s, d = x.shape
    f = act.shape[1]
    tk = f // ksplit
    return pl.pallas_call(
        _ffn_down_body,
        grid=(s // tm, d // tn, ksplit),
        in_specs=[pl.BlockSpec((tm, tn), lambda i, j, kk: (i, j)),
                  pl.BlockSpec((tm, tk), lambda i, j, kk: (i, kk)),
                  pl.BlockSpec((tk, tn), lambda i, j, kk: (kk, j))],
        out_specs=pl.BlockSpec((tm, tn), lambda i, j, kk: (i, j)),
        out_shape=jax.ShapeDtypeStruct((s, d), F32),
        scratch_shapes=[pltpu.VMEM((tm, tn), F32)],
        compiler_params=_cparams(("parallel", "arbitrary", "arbitrary")),
        name="ffn_down",
    )(x, act, w)


def _unit_lower_inverses(n_mats, size):
    row = lax.broadcasted_iota(jnp.int32, (size, size), 0)
    col = lax.broadcasted_iota(jnp.int32, (size, size), 1)
    eye = jnp.where(row == col, 1.0, 0.0).astype(F32)
    invs = [eye + n for n in n_mats]
    powers = [_dot(n, n, mode=MODE_INV) for n in n_mats]
    steps = int(math.log2(size)) - 1
    for s in range(steps):
        last = s == steps - 1
        for i, (inv, p) in enumerate(zip(invs, powers)):
            if last:
                invs[i] = inv + _dot(inv, p, mode=MODE_INV)
            else:
                both = _dot(jnp.concatenate([inv, p], axis=0), p, mode=MODE_INV)
                invs[i] = inv + both[0:size]
                powers[i] = both[size:2 * size]
    return invs


def _gdn_body(alog_ref, dtb_ref, q_ref, k_ref, v_ref, z_ref, ba_ref, cq_ref, ck_ref, cv_ref,
              nw_ref, o_ref, xbuf, state_ref, *, tile):
    c = GDN_CHUNK
    hd = GDN_DIM
    width = GDN_GROUP * hd
    n_chunks = tile // c
    t = pl.program_id(1)

    @pl.when(t == 0)
    def _reset():
        state_ref[...] = jnp.zeros_like(state_ref)
        xbuf[:, 0:8, :] = jnp.zeros((3, 8, width), F32)

    convs = []
    for idx, (src, cw_ref) in enumerate(((q_ref, cq_ref), (k_ref, ck_ref), (v_ref, cv_ref))):
        xbuf[idx, 8:8 + tile, :] = src[...]
        cw = cw_ref[...]
        acc = cw[0:1, :] * xbuf[idx, 5:5 + tile, :]
        acc += cw[1:2, :] * xbuf[idx, 6:6 + tile, :]
        acc += cw[2:3, :] * xbuf[idx, 7:7 + tile, :]
        acc += cw[3:4, :] * xbuf[idx, 8:8 + tile, :]
        convs.append(_silu(acc))
        xbuf[idx, 0:8, :] = xbuf[idx, tile:tile + 8, :]
    q_all, k_all, v_all = convs

    ba = ba_ref[...]
    lane = lax.broadcasted_iota(jnp.int32, (1, LANE), 1)
    ri = lax.broadcasted_iota(jnp.int32, (c, c), 0)
    ci = lax.broadcasted_iota(jnp.int32, (c, c), 1)
    lower = jnp.where(ci <= ri, 1.0, 0.0).astype(F32)
    upper = jnp.where(ri <= ci, 1.0, 0.0).astype(F32)
    r2 = lax.broadcasted_iota(jnp.int32, (2 * c, c), 0)
    c2 = lax.broadcasted_iota(jnp.int32, (2 * c, c), 1)
    band = c2 <= jnp.where(r2 < c, r2 - 1, r2 - c)
    nw = nw_ref[...]

    units = []
    for j in range(GDN_GROUP):
        h = pl.program_id(0) * GDN_GROUP + j
        cols = slice(j * hd, (j + 1) * hd)
        q, k, v = q_all[:, cols], k_all[:, cols], v_all[:, cols]
        q = q * lax.rsqrt(jnp.sum(q * q, axis=-1, keepdims=True) + 1e-6) * (hd ** -0.5)
        k = k * lax.rsqrt(jnp.sum(k * k, axis=-1, keepdims=True) + 1e-6)
        b_col = jnp.sum(jnp.where(lane == h, ba, 0.0), axis=-1, keepdims=True)
        a_col = jnp.sum(jnp.where(lane == h + GDN_HEADS, ba, 0.0), axis=-1, keepdims=True)
        beta = _sigmoid(b_col)
        g = -jnp.exp(alog_ref[h]) * _softplus(a_col + dtb_ref[h])
        g_b = g * jnp.ones((1, hd), F32)
        g_col = jnp.concatenate([_dot(lower, g_b[n * c:(n + 1) * c], mode=MODE_CUMSUM)
                                 for n in range(n_chunks)], axis=0)
        g_row = jnp.concatenate([_dot(jnp.ones((8, c), F32), g_b[n * c:(n + 1) * c, 0:c] * upper,
                                      mode=MODE_CUMSUM) for n in range(n_chunks)], axis=1)
        exp_g = jnp.exp(g_col)
        kb = k * beta
        vb = v * beta
        kbe = kb * exp_g
        qg = q * exp_g
        for n in range(n_chunks):
            rows = slice(n * c, (n + 1) * c)
            g_last = g_col[n * c + c - 1:n * c + c, :]
            gamma = jnp.exp(jnp.minimum(g_col[rows, 0:c] - g_row[0:1, n * c:(n + 1) * c], 0.0))
            units.append(dict(j=j, n=n, k=k[rows], kq=jnp.concatenate([kb[rows], q[rows]], axis=0),
                              rhs=jnp.concatenate([vb[rows], kbe[rows]], axis=1), qg=qg[rows],
                              kd=k[rows] * jnp.exp(g_last - g_col[rows]), decay=jnp.exp(g_last),
                              gamma2=jnp.concatenate([gamma, gamma], axis=0)))
    for un in units:
        un["scores"] = jnp.where(band, _dot(un["kq"], un["k"], _NT) * un["gamma2"], 0.0)
    t_invs = _unit_lower_inverses([-un["scores"][0:c] for un in units], c)
    for un, t_inv in zip(units, t_invs):
        un["uw"] = _dot(t_inv, un["rhs"], mode=MODE_SOLVE)

    for un in units:
        un["trans"] = _dot(un["kd"], un["uw"][:, hd:2 * hd], _TN, mode=MODE_STATE)
        un["const"] = _dot(un["kd"], un["uw"][:, 0:hd], _TN, mode=MODE_STATE)
    states = [state_ref[j] for j in range(GDN_GROUP)]
    for n in range(n_chunks):
        for un in units:
            if un["n"] != n:
                continue
            j = un["j"]
            un["s0"] = states[j]
            states[j] = (states[j] * un["decay"] - _dot(un["trans"], states[j], mode=MODE_STATE)
                         + un["const"])
    for j in range(GDN_GROUP):
        state_ref[j] = states[j]
    outs = [[None] * n_chunks for _ in range(GDN_GROUP)]
    for un in units:
        un["ws"] = _dot(jnp.concatenate([un["uw"][:, hd:2 * hd], un["qg"]], axis=0), un["s0"],
                        mode=MODE_STATE)
    for un in units:
        v_new = un["uw"][:, 0:hd] - un["ws"][0:c]
        outs[un["j"]][un["n"]] = un["ws"][c:2 * c] + _dot(un["scores"][c:2 * c], v_new, mode=MODE_STATE)
    for j in range(GDN_GROUP):
        cols = slice(j * hd, (j + 1) * hd)
        o = jnp.concatenate(outs[j], axis=0)
        o = o * lax.rsqrt(jnp.mean(o * o, axis=-1, keepdims=True) + RMS_EPS) * nw
        o_ref[:, cols] = (o * _silu(z_ref[:, cols])).astype(o_ref.dtype)


def _gdn(proj, conv_w, a_log, dt_bias, norm_w, tile=512):
    s = proj.shape[0]
    width = GDN_GROUP * GDN_DIM
    per = LANE // GDN_DIM * GDN_GROUP
    blk = lambda col: pl.BlockSpec((tile, width), lambda h, t: (t, col // per + h))
    cblk = lambda col: pl.BlockSpec((GDN_CONV, width), lambda h, t: (0, col // per + h))
    smem = pl.BlockSpec(memory_space=pltpu.SMEM)
    return pl.pallas_call(
        functools.partial(_gdn_body, tile=tile),
        grid=(GDN_HEADS // GDN_GROUP, s // tile),
        in_specs=[smem, smem,
                  blk(COL_GDN_Q), blk(COL_GDN_K), blk(COL_GDN_V), blk(COL_GDN_Z),
                  pl.BlockSpec((tile, LANE), lambda h, t: (t, COL_GDN_BA)),
                  cblk(0), cblk(GDN_HEADS), cblk(2 * GDN_HEADS),
                  pl.BlockSpec((1, GDN_DIM), lambda h, t: (0, 0))],
        out_specs=pl.BlockSpec((tile, width), lambda h, t: (t, h)),
        out_shape=jax.ShapeDtypeStruct((s, GDN_WIDTH), BF16),
        scratch_shapes=[pltpu.VMEM((3, tile + 8, width), F32),
                        pltpu.VMEM((GDN_GROUP, GDN_DIM, GDN_DIM), F32)],
        compiler_params=_cparams(("parallel", "arbitrary")),
        name="gdn",
    )(a_log, dt_bias, proj, proj, proj, proj, proj, conv_w, conv_w, conv_w, norm_w.reshape(1, GDN_DIM))


def _t5_bucket_table():
    exact = NUM_BUCKETS // 2
    i = np.arange(SWA_BLK)[:, None]
    j = np.arange(2 * SWA_BLK)[None, :]
    steps = np.maximum(i + SWA_BLK - j, 0)
    tables = []
    for _, dilation in SWA_PATTERNS:
        dist = steps * dilation
        d = np.maximum(dist, 1).astype(np.float32)
        ratio = (np.log(d / np.float32(exact)) / np.float32(math.log(MAX_DISTANCE / exact))
                 * np.float32(NUM_BUCKETS - exact)).astype(np.float32)
        log_b = exact + ratio.astype(np.int32)
        tables.append(np.where(dist < exact, dist, np.minimum(log_b, NUM_BUCKETS - 1)))
    return np.stack(tables).astype(np.int32)


def _swa_body(rb_ref, bkt_ref, q_ref, kc_ref, kp_ref, vc_ref, vp_ref, o_ref,
              bias_ref, kbuf, vbuf, o_scr, m_scr, l_scr, *, tile):
    h = pl.program_id(0)
    t = pl.program_id(1)
    blk = SWA_BLK

    @pl.when(t == 0)
    def _bias():
        for p in range(len(SWA_PATTERNS)):
            bkt = bkt_ref[p]
            bias = jnp.zeros((blk, 2 * blk), F32)
            for b in range(NUM_BUCKETS):
                bias = jnp.where(bkt == b, rb_ref[b, h], bias)
            bias_ref[p] = bias

    kbuf[0:tile, :] = kp_ref[...]
    kbuf[tile:2 * tile, :] = kc_ref[...]
    vbuf[0:tile, :] = vp_ref[...]
    vbuf[tile:2 * tile, :] = vc_ref[...]

    qi = lax.broadcasted_iota(jnp.int32, (blk, 2 * blk), 0)
    kj = lax.broadcasted_iota(jnp.int32, (blk, 2 * blk), 1)
    in_band = (kj >= qi) & (kj <= qi + blk)
    scale = SWA_DIM ** -0.5

    band_first = in_band & (kj >= jnp.where(t == 0, blk, 0))
    ones_row = jnp.ones((1, SWA_DIM), F32)

    for p, (window, dil) in enumerate(SWA_PATTERNS):
        span = blk * dil
        bias = bias_ref[p]
        blocks = [(res + n * span, n) for n in range(tile // span) for res in range(dil)]
        for g0 in range(0, len(blocks), SWA_GROUP):
            group = blocks[g0:g0 + SWA_GROUP]
            scores = []
            for start, n in group:
                qb = q_ref[pl.ds(start, blk, stride=dil), :] * scale
                kw = kbuf[pl.ds(tile + start - span, 2 * blk, stride=dil), :]
                sc = _dot(qb, kw, _NT) + bias
                scores.append(jnp.where(band_first if n == 0 else in_band, sc, NEG_INF))
            probs = []
            for sc in scores:
                m = jnp.max(sc, axis=-1, keepdims=True)
                pe = jnp.exp(sc - m)
                probs.append((pe, m, jnp.sum(pe, axis=-1, keepdims=True)))
            for (start, n), (pe, m, l) in zip(group, probs):
                vw = vbuf[pl.ds(tile + start - span, 2 * blk, stride=dil), :]
                o_scr[p, pl.ds(start, blk, stride=dil), :] = _dot(pe, vw)
                m_scr[p, pl.ds(start, blk, stride=dil), :] = m * ones_row
                l_scr[p, pl.ds(start, blk, stride=dil), :] = l * ones_row

    m_max = jnp.maximum(jnp.maximum(m_scr[0], m_scr[1]), m_scr[2])
    num = jnp.zeros((tile, SWA_DIM), F32)
    den = jnp.zeros((tile, SWA_DIM), F32)
    for p in range(len(SWA_PATTERNS)):
        sc = jnp.exp(m_scr[p] - m_max)
        num += o_scr[p] * sc
        den += l_scr[p] * sc
    o_ref[...] = (num / den).astype(o_ref.dtype)


def _swa(proj, rel_bias, tile=SWA_TILE):
    s = proj.shape[0]
    n_pat = len(SWA_PATTERNS)
    bkt = jnp.asarray(_t5_bucket_table())
    cur = lambda col: pl.BlockSpec((tile, LANE), lambda h, t: (t, col + h))
    prev = lambda col: pl.BlockSpec((tile, LANE), lambda h, t: (jnp.maximum(t - 1, 0), col + h))
    return pl.pallas_call(
        functools.partial(_swa_body, tile=tile),
        grid=(SWA_HEADS, s // tile),
        in_specs=[pl.BlockSpec(memory_space=pltpu.SMEM),
                  pl.BlockSpec((n_pat, SWA_BLK, 2 * SWA_BLK), lambda h, t: (0, 0, 0)),
                  cur(COL_SWA_Q), cur(COL_SWA_K), prev(COL_SWA_K), cur(COL_SWA_V), prev(COL_SWA_V)],
        out_specs=pl.BlockSpec((tile, LANE), lambda h, t: (t, h)),
        out_shape=jax.ShapeDtypeStruct((s, SWA_WIDTH), BF16),
        scratch_shapes=[pltpu.VMEM((n_pat, SWA_BLK, 2 * SWA_BLK), F32),
                        pltpu.VMEM((2 * tile, SWA_DIM), F32),
                        pltpu.VMEM((2 * tile, SWA_DIM), F32),
                        pltpu.VMEM((n_pat, tile, SWA_DIM), F32),
                        pltpu.VMEM((n_pat, tile, SWA_DIM), F32),
                        pltpu.VMEM((n_pat, tile, SWA_DIM), F32)],
        compiler_params=_cparams(("parallel", "arbitrary")),
        name="swa",
    )(rel_bias, bkt, proj, proj, proj, proj, proj)


PRM_W0, PRM_A0, PRM_KK, PRM_KA, PRM_RK, PRM_LNW, PRM_LNB, PRM_MUR, PRM_MUK, PRM_MUV = range(10)
PRM_ROWS = 16


def _rwkv_body(r_ref, k_ref, v_ref, wd_ref, ad_ref, gd_ref, prm_ref, mul_ref, mug_ref,
               wup_ref, aup_ref, gup_ref, o_ref, xb, xg, state_ref, *, tile):
    c = RWKV_CHUNK
    nd = RWKV_DIM
    t = pl.program_id(1)

    @pl.when(t == 0)
    def _reset():
        state_ref[...] = jnp.zeros_like(state_ref)
        xb[:, 0:8, :] = jnp.zeros((5, 8, LANE), F32)
        xg[0:8, :] = jnp.zeros((8, GATE_LORA_PAD), F32)

    prm = prm_ref[...]
    row = lambda i: prm[i:i + 1, :]

    def shifted(idx, src, mu):
        xb[idx, 8:8 + tile, :] = src[...]
        cur = xb[idx, 8:8 + tile, :]
        prev = xb[idx, 7:7 + tile, :]
        xb[idx, 0:8, :] = xb[idx, tile:tile + 8, :]
        return cur + mu * (prev - cur)

    r = shifted(0, r_ref, row(PRM_MUR))
    k = shifted(1, k_ref, row(PRM_MUK))
    v = shifted(2, v_ref, row(PRM_MUV))
    wd = shifted(3, wd_ref, mul_ref[0:1, :])
    ad = shifted(4, ad_ref, mul_ref[1:2, :])
    xg[8:8 + tile, :] = gd_ref[...]
    gcur = xg[8:8 + tile, :]
    gprev = xg[7:7 + tile, :]
    xg[0:8, :] = xg[tile:tile + 8, :]
    gd = gcur + mug_ref[...] * (gprev - gcur)

    w_log = -_softplus(-(row(PRM_W0) + _dot(jnp.tanh(wd), wup_ref[...]))) - 0.5
    log_w = -jnp.exp(w_log)
    a_gate = _sigmoid(row(PRM_A0) + _dot(ad, aup_ref[...]))
    g_gate = _dot(_sigmoid(gd), gup_ref[...])
    kx = k * row(PRM_KK)
    k_mod = k * (1.0 + (a_gate - 1.0) * row(PRM_KA))

    ri = lax.broadcasted_iota(jnp.int32, (c, c), 0)
    ci = lax.broadcasted_iota(jnp.int32, (c, c), 1)
    lower = jnp.where(ci <= ri, 1.0, 0.0).astype(F32)
    cum = jnp.concatenate([_dot(lower, log_w[n * c:(n + 1) * c], mode=MODE_CUMSUM)
                           for n in range(tile // c)], axis=0)

    r2 = lax.broadcasted_iota(jnp.int32, (2 * c, 2 * c), 0)
    c2 = lax.broadcasted_iota(jnp.int32, (2 * c, 2 * c), 1)
    col_in = jnp.where(c2 < c, c2, c2 - c)
    band = col_in <= jnp.where(r2 < c, r2 - 1, r2 - c)

    n_heads = LANE // nd
    n_chunks = tile // c
    lane = lax.broadcasted_iota(jnp.int32, (1, LANE), 1)
    head_masks = [(lane // nd) == hh for hh in range(n_heads)]

    def head_sum(x):
        out = jnp.zeros_like(x)
        for msk in head_masks:
            out = jnp.where(msk, jnp.sum(jnp.where(msk, x, 0.0), axis=-1, keepdims=True), out)
        return out

    kk = kx * lax.rsqrt(head_sum(kx * kx) + 1e-12)
    b_vec = kk * a_gate
    c_last = jnp.concatenate(
        [jnp.broadcast_to(cum[n * c + c - 1:n * c + c, :], (c, LANE)) for n in range(n_chunks)], axis=0)
    p_out = jnp.exp(-cum)
    a_t = -kk * jnp.exp(cum - log_w)
    r_t = r * jnp.exp(cum)
    k_t = k_mod * p_out
    b_t = b_vec * p_out
    decay_end = jnp.exp(c_last - cum)
    k_e = k_mod * decay_end
    b_e = b_vec * decay_end
    decay_all = jnp.exp(c_last)

    units = []
    for hh in range(n_heads):
        lanes = slice(hh * nd, (hh + 1) * nd)
        for n in range(n_chunks):
            rows = slice(n * c, (n + 1) * c)
            units.append(dict(hh=hh, n=n, a=a_t[rows, lanes], r=r_t[rows, lanes], v=v[rows, lanes],
                              ar=jnp.concatenate([a_t[rows, lanes], r_t[rows, lanes]], axis=0),
                              bk=jnp.concatenate([b_t[rows, lanes], k_t[rows, lanes]], axis=0),
                              bk_e=jnp.concatenate([b_e[rows, lanes], k_e[rows, lanes]], axis=0),
                              decay=decay_all[n * c:n * c + 1, lanes]))
    for un in units:
        un["mm"] = jnp.where(band, _dot(un["ar"], un["bk"], _NT), 0.0)
    t_invs = _unit_lower_inverses([un["mm"][0:c, 0:c] for un in units], c)
    for un in units:
        un["mv"] = _dot(un["mm"][0:c, c:2 * c], un["v"], mode=MODE_SOLVE)
    for un, t_inv in zip(units, t_invs):
        un["wu"] = _dot(t_inv, jnp.concatenate([un["a"], un["mv"]], axis=1), mode=MODE_SOLVE)

    for un in units:
        un["trans"] = _dot(un["wu"][:, 0:nd], un["bk_e"][0:c], _TN, mode=MODE_STATE)
        un["const"] = _dot(jnp.concatenate([un["wu"][:, nd:2 * nd], un["v"]], axis=0), un["bk_e"], _TN,
                           mode=MODE_STATE)
    states = [state_ref[hh] for hh in range(n_heads)]
    for n in range(n_chunks):
        for un in units:
            if un["n"] != n:
                continue
            hh = un["hh"]
            un["s0"] = states[hh]
            states[hh] = (states[hh] * un["decay"] + _dot(states[hh], un["trans"], mode=MODE_STATE)
                          + un["const"])
    for hh in range(n_heads):
        state_ref[hh] = states[hh]
    ys = [[None] * n_chunks for _ in range(n_heads)]
    for un in units:
        un["sr"] = _dot(jnp.concatenate([un["wu"][:, 0:nd], un["r"]], axis=0), un["s0"], _NT,
                        mode=MODE_STATE)
    for un in units:
        u = un["sr"][0:c] + un["wu"][:, nd:2 * nd]
        uv = jnp.concatenate([u, un["v"]], axis=0)
        ys[un["hh"]][un["n"]] = un["sr"][c:2 * c] + _dot(un["mm"][c:2 * c], uv, mode=MODE_STATE)
    y = jnp.concatenate([jnp.concatenate(ys[hh], axis=0) for hh in range(n_heads)], axis=1)
    mean = head_sum(y) * (1.0 / nd)
    var = head_sum(jnp.square(y - mean)) * (1.0 / nd)
    y_n = (y - mean) * lax.rsqrt(var + GN_EPS) * row(PRM_LNW) + row(PRM_LNB)
    bonus = head_sum(r * k_mod * row(PRM_RK)) * v
    o_ref[...] = ((y_n + bonus) * g_gate).astype(o_ref.dtype)


def _rwkv(proj, prm, mu_lora, mu_gate, w_up, a_up, g_up, tile=512):
    s = proj.shape[0]
    pairs = RWKV_WIDTH // LANE
    blk = lambda col: pl.BlockSpec((tile, LANE), lambda h, t: (t, col + h))
    fixed = lambda col: pl.BlockSpec((tile, LANE), lambda h, t: (t, col))
    return pl.pallas_call(
        functools.partial(_rwkv_body, tile=tile),
        grid=(pairs, s // tile),
        in_specs=[blk(COL_R), blk(COL_K), blk(COL_V), fixed(COL_WD), fixed(COL_AD),
                  pl.BlockSpec((tile, GATE_LORA_PAD), lambda h, t: (t, COL_GD * LANE // GATE_LORA_PAD)),
                  pl.BlockSpec((PRM_ROWS, LANE), lambda h, t: (0, h)),
                  pl.BlockSpec((8, LANE), lambda h, t: (0, 0)),
                  pl.BlockSpec((1, GATE_LORA_PAD), lambda h, t: (0, 0)),
                  pl.BlockSpec((DECAY_LORA, LANE), lambda h, t: (0, h)),
                  pl.BlockSpec((ICL_LORA, LANE), lambda h, t: (0, h)),
                  pl.BlockSpec((GATE_LORA_PAD, LANE), lambda h, t: (0, h))],
        out_specs=pl.BlockSpec((tile, LANE), lambda h, t: (t, h)),
        out_shape=jax.ShapeDtypeStruct((s, RWKV_WIDTH), BF16),
        scratch_shapes=[pltpu.VMEM((5, tile + 8, LANE), F32),
                        pltpu.VMEM((tile + 8, GATE_LORA_PAD), F32),
                        pltpu.VMEM((LANE // RWKV_DIM, RWKV_DIM, RWKV_DIM), F32)],
        compiler_params=_cparams(("parallel", "arbitrary")),
        name="rwkv7",
    )(proj, proj, proj, proj, proj, proj, prm, mu_lora, mu_gate, w_up, a_up, g_up)


def _layout_w_in(w):
    d = w.shape[0]
    gw, sw, rw = GDN_WIDTH, SWA_WIDTH, RWKV_WIDTH
    o = 0
    gdn_main = w[:, o:o + 4 * gw]; o += 4 * gw
    gdn_ba = w[:, o:o + 2 * GDN_HEADS]; o += 2 * GDN_HEADS
    swa = w[:, o:o + 3 * sw]; o += 3 * sw
    rkv = w[:, o:o + 3 * rw]; o += 3 * rw
    wd = w[:, o:o + DECAY_LORA]; o += DECAY_LORA
    ad = w[:, o:o + ICL_LORA]; o += ICL_LORA
    gd = w[:, o:o + GATE_LORA]
    z = lambda n: jnp.zeros((d, n), w.dtype)
    out = jnp.concatenate([gdn_main, gdn_ba, z(LANE - 2 * GDN_HEADS), swa, wd, ad, z(LANE),
                           gd, z(GATE_LORA_PAD - GATE_LORA), rkv], axis=1)
    assert out.shape[1] == PROJ_WIDTH
    return out.astype(BF16)


def _layout_rwkv_params(mu, w0, a0, k_k, k_a, r_k, ln_w, ln_b):
    rw = RWKV_WIDTH
    mu_r, mu_k, mu_v = mu[0:rw], mu[rw:2 * rw], mu[2 * rw:3 * rw]
    o = 3 * rw
    mu_wd = mu[o:o + DECAY_LORA]; o += DECAY_LORA
    mu_ad = mu[o:o + ICL_LORA]; o += ICL_LORA
    mu_gd = mu[o:o + GATE_LORA]
    rows = [w0, a0, k_k, k_a, r_k.reshape(rw), ln_w, ln_b, mu_r, mu_k, mu_v]
    prm = jnp.stack(rows + [jnp.zeros((rw,), F32)] * (PRM_ROWS - len(rows))).astype(F32)
    mu_lora = jnp.stack([mu_wd, mu_ad] + [jnp.zeros((LANE,), F32)] * 6).astype(F32)
    mu_gate = jnp.concatenate([mu_gd, jnp.zeros((GATE_LORA_PAD - GATE_LORA,), F32)]).reshape(1, GATE_LORA_PAD)
    return prm, mu_lora, mu_gate


def kernel(x, attn_norm, w_in, gdn_conv, gdn_a_log, gdn_dt_bias, gdn_norm, rwkv_mu, rwkv_w0, rwkv_w_up, rwkv_a0, rwkv_a_up, rwkv_g_up, rwkv_k_k, rwkv_k_a, rwkv_r_k, rwkv_ln_w, rwkv_ln_b, w_out, ffn_norm, w_ffn_gate, w_ffn_up, ffn_conv, ffn_conv_b, w_ffn_down, rel_bias, final_norm):
    batch, seq, d = x.shape
    depth = w_in.shape[0]
    outs = []
    rows_in = [x.reshape(seq, d)] if batch == 1 else [x[b] for b in range(batch)]
    for xb in rows_in:
        for l in range(depth):
            h = _rmsnorm(xb, attn_norm[l], BF16)
            proj = _in_proj(h, _layout_w_in(w_in[l]))
            o_a = _gdn(proj, gdn_conv[l], gdn_a_log[l], gdn_dt_bias[l], gdn_norm[l])
            o_b = _swa(proj, rel_bias)
            prm, mu_lora, mu_gate = _layout_rwkv_params(
                rwkv_mu[l], rwkv_w0[l], rwkv_a0[l], rwkv_k_k[l], rwkv_k_a[l], rwkv_r_k[l],
                rwkv_ln_w[l], rwkv_ln_b[l])
            g_up = jnp.concatenate(
                [rwkv_g_up[l], jnp.zeros((GATE_LORA_PAD - GATE_LORA, RWKV_WIDTH), F32)], axis=0)
            o_c = _rwkv(proj, prm, mu_lora, mu_gate, rwkv_w_up[l].astype(BF16),
                        rwkv_a_up[l].astype(BF16), g_up.astype(BF16))
            wo = _to_bf16(w_out, l)
            xb = _out_proj(xb, o_a, o_b, o_c, wo[0:GDN_WIDTH], wo[GDN_WIDTH:GDN_WIDTH + SWA_WIDTH],
                           wo[GDN_WIDTH + SWA_WIDTH:])
            h = _rmsnorm(xb, ffn_norm[l], BF16)
            act = _ffn_up(h, _to_bf16(w_ffn_gate, l), _to_bf16(w_ffn_up, l), ffn_conv[l], ffn_conv_b[l])
            xb = _ffn_down(xb, act, _to_bf16(w_ffn_down, l))
        outs.append(_rmsnorm(xb, final_norm, x.dtype))
    return outs[0].reshape(1, seq, d) if batch == 1 else jnp.stack(outs)
```

```python
import functools
import math

import numpy as np
import jax
import jax.numpy as jnp
from jax import lax
from jax.experimental import pallas as pl
from jax.experimental.pallas import tpu as pltpu

F32 = jnp.float32
BF16 = jnp.bfloat16
HIGHEST = lax.Precision.HIGHEST

LANE = 128
D_MODEL = 4096
RMS_EPS = 1e-6
GN_EPS = 64e-5
NEG_INF = -1e30
VMEM_LIMIT = 56 * 1024 * 1024

GDN_DIM = 128
GDN_HEADS = 12
GDN_WIDTH = GDN_HEADS * GDN_DIM
GDN_CONV = 4
GDN_CHUNK = 64
GDN_GROUP = 2
SWA_DIM = 128
SWA_HEADS = 8
SWA_WIDTH = SWA_HEADS * SWA_DIM
SWA_PATTERNS = ((128, 1), (512, 4), (2048, 16))
SWA_BLK = 128
SWA_TILE = 2048
SWA_GROUP = 4
NUM_BUCKETS = 32
MAX_DISTANCE = 2048
RWKV_DIM = 64
RWKV_HEADS = 24
RWKV_WIDTH = RWKV_HEADS * RWKV_DIM
RWKV_CHUNK = 64
DECAY_LORA = 128
ICL_LORA = 128
GATE_LORA = 480
GATE_LORA_PAD = 512
D_FF = 11008
FFN_CONV = 3

COL_GDN_Q, COL_GDN_K, COL_GDN_V, COL_GDN_Z, COL_GDN_BA = 0, 12, 24, 36, 48
COL_SWA_Q, COL_SWA_K, COL_SWA_V = 49, 57, 65
COL_WD, COL_AD = 73, 74
COL_GD = 76
COL_R, COL_K, COL_V = 80, 92, 104
PROJ_BLOCKS = 116
PROJ_WIDTH = PROJ_BLOCKS * LANE


def _cparams(sem):
    return pltpu.CompilerParams(dimension_semantics=sem, vmem_limit_bytes=VMEM_LIMIT)


def _silu(x):
    return x * (1.0 / (1.0 + jnp.exp(-x)))


def _sigmoid(x):
    return 1.0 / (1.0 + jnp.exp(-x))


def _softplus(x):
    return jnp.maximum(x, 0.0) + jnp.log(1.0 + jnp.exp(-jnp.abs(x)))


def _split_bf16(x):
    hi = x.astype(BF16)
    return hi, (x - hi.astype(F32)).astype(BF16)


def _dot(a, b, dims=(((1,), (0,)), ((), ())), mode="bf16"):
    if mode == "f32":
        return lax.dot_general(a.astype(F32), b.astype(F32), dims, precision=HIGHEST,
                               preferred_element_type=F32)
    if mode == "x3":
        a_hi, a_lo = _split_bf16(a)
        b_hi, b_lo = _split_bf16(b)
        mm = lambda p, q: lax.dot_general(p, q, dims, preferred_element_type=F32)
        return mm(a_hi, b_hi) + (mm(a_hi, b_lo) + mm(a_lo, b_hi))
    return lax.dot_general(a.astype(BF16), b.astype(BF16), dims, preferred_element_type=F32)


MODE_CUMSUM = "f32"
MODE_INV = "bf16"
MODE_SOLVE = "bf16"
MODE_STATE = "bf16"


_NT = (((1,), (1,)), ((), ()))
_TN = (((0,), (0,)), ((), ()))


def _rmsnorm_body(x_ref, w_ref, o_ref):
    x = x_ref[...]
    ms = jnp.mean(x * x, axis=-1, keepdims=True)
    o_ref[...] = (x * lax.rsqrt(ms + RMS_EPS) * w_ref[...]).astype(o_ref.dtype)


def _rmsnorm(x, w, out_dtype, tm=512):
    s, d = x.shape
    return pl.pallas_call(
        _rmsnorm_body,
        grid=(s // tm,),
        in_specs=[pl.BlockSpec((tm, d), lambda i: (i, 0)),
                  pl.BlockSpec((1, d), lambda i: (0, 0))],
        out_specs=pl.BlockSpec((tm, d), lambda i: (i, 0)),
        out_shape=jax.ShapeDtypeStruct((s, d), out_dtype),
        compiler_params=_cparams(("parallel",)),
        name="rmsnorm",
    )(x, w.reshape(1, d))


def _cast_body(w_ref, o_ref):
    o_ref[...] = w_ref[...].astype(o_ref.dtype)


def _to_bf16(w, layer, rows=256):
    _, r, c = w.shape
    return pl.pallas_call(
        _cast_body,
        grid=(r // rows,),
        in_specs=[pl.BlockSpec((None, rows, c), lambda i: (layer, i, 0))],
        out_specs=pl.BlockSpec((rows, c), lambda i: (i, 0)),
        out_shape=jax.ShapeDtypeStruct((r, c), BF16),
        compiler_params=_cparams(("parallel",)),
        name="to_bf16",
    )(w)


def _matmul_body(a_ref, b_ref, o_ref):
    o_ref[...] = jnp.dot(a_ref[...], b_ref[...], preferred_element_type=F32).astype(o_ref.dtype)


def _in_proj(h, w, tm=1024, tn=512):
    s, k = h.shape
    n = w.shape[1]
    return pl.pallas_call(
        _matmul_body,
        grid=(s // tm, n // tn),
        in_specs=[pl.BlockSpec((tm, k), lambda i, j: (i, 0)),
                  pl.BlockSpec((k, tn), lambda i, j: (0, j))],
        out_specs=pl.BlockSpec((tm, tn), lambda i, j: (i, j)),
        out_shape=jax.ShapeDtypeStruct((s, n), F32),
        compiler_params=_cparams(("parallel", "arbitrary")),
        name="in_proj",
    )(h, w)


def _out_proj_body(x_ref, a_ref, b_ref, c_ref, wa_ref, wb_ref, wc_ref, o_ref):
    acc = jnp.dot(a_ref[...], wa_ref[...], preferred_element_type=F32)
    acc += jnp.dot(b_ref[...], wb_ref[...], preferred_element_type=F32)
    acc += jnp.dot(c_ref[...], wc_ref[...], preferred_element_type=F32)
    o_ref[...] = x_ref[...] + acc


def _out_proj(x, oa, ob, oc, wa, wb, wc, tm=1024, tn=512):
    s, d = x.shape
    ka, kb, kc = oa.shape[1], ob.shape[1], oc.shape[1]
    return pl.pallas_call(
        _out_proj_body,
        grid=(s // tm, d // tn),
        in_specs=[pl.BlockSpec((tm, tn), lambda i, j: (i, j)),
                  pl.BlockSpec((tm, ka), lambda i, j: (i, 0)),
                  pl.BlockSpec((tm, kb), lambda i, j: (i, 0)),
                  pl.BlockSpec((tm, kc), lambda i, j: (i, 0)),
                  pl.BlockSpec((ka, tn), lambda i, j: (0, j)),
                  pl.BlockSpec((kb, tn), lambda i, j: (0, j)),
                  pl.BlockSpec((kc, tn), lambda i, j: (0, j))],
        out_specs=pl.BlockSpec((tm, tn), lambda i, j: (i, j)),
        out_shape=jax.ShapeDtypeStruct((s, d), F32),
        compiler_params=_cparams(("parallel", "arbitrary")),
        name="out_proj",
    )(x, oa, ob, oc, wa, wb, wc)


FFN_HALO = 16


def _ffn_up_body(h_ref, halo_ref, wg_ref, wu_ref, cw_ref, cb_ref, o_ref, hbuf, *, tm):
    i = pl.program_id(0)

    @pl.when(pl.program_id(1) == 0)
    def _stage():
        halo = halo_ref[...]
        hbuf[0:FFN_HALO, :] = jnp.where(i == 0, jnp.zeros_like(halo), halo)
        hbuf[FFN_HALO:FFN_HALO + tm, :] = h_ref[...]

    g = jnp.dot(hbuf[...], wg_ref[...], preferred_element_type=F32)
    u = jnp.dot(hbuf[FFN_HALO:FFN_HALO + tm, :], wu_ref[...], preferred_element_type=F32)
    cw = cw_ref[...]
    conv = (cw[0:1, :] * pltpu.roll(g, 2, 0)[FFN_HALO:, :]
            + cw[1:2, :] * pltpu.roll(g, 1, 0)[FFN_HALO:, :]
            + cw[2:3, :] * g[FFN_HALO:, :]) + cb_ref[...]
    o_ref[...] = (_silu(conv) * u).astype(o_ref.dtype)


def _ffn_up(h, wg, wu, cw, cb, tm=1024, tn=256):
    s, k = h.shape
    f = wg.shape[1]
    halo_blocks = tm // FFN_HALO
    return pl.pallas_call(
        functools.partial(_ffn_up_body, tm=tm),
        grid=(s // tm, f // tn),
        in_specs=[pl.BlockSpec((tm, k), lambda i, j: (i, 0)),
                  pl.BlockSpec((FFN_HALO, k), lambda i, j: (jnp.maximum(i * halo_blocks - 1, 0), 0)),
                  pl.BlockSpec((k, tn), lambda i, j: (0, j)),
                  pl.BlockSpec((k, tn), lambda i, j: (0, j)),
                  pl.BlockSpec((FFN_CONV, tn), lambda i, j: (0, j)),
                  pl.BlockSpec((1, tn), lambda i, j: (0, j))],
        out_specs=pl.BlockSpec((tm, tn), lambda i, j: (i, j)),
        out_shape=jax.ShapeDtypeStruct((s, f), BF16),
        scratch_shapes=[pltpu.VMEM((tm + FFN_HALO, k), BF16)],
        compiler_params=_cparams(("parallel", "arbitrary")),
        name="ffn_up",
    )(h, h, wg, wu, cw, cb.reshape(1, f))


def _ffn_down_body(x_ref, a_ref, w_ref, o_ref, acc_ref):
    kk = pl.program_id(2)

    @pl.when(kk == 0)
    def _init():
        acc_ref[...] = x_ref[...]

    acc_ref[...] += jnp.dot(a_ref[...], w_ref[...], preferred_element_type=F32)

    @pl.when(kk == pl.num_programs(2) - 1)
    def _done():
        o_ref[...] = acc_ref[...]


def _ffn_down(x, act, w, tm=1024, tn=512, ksplit=2):
    s, d = x.shape
    f = act.shape[1]
    tk = f // ksplit
    return pl.pallas_call(
        _ffn_down_body,
        grid=(s // tm, d // tn, ksplit),
        in_specs=[pl.BlockSpec((tm, tn), lambda i, j, kk: (i, j)),
                  pl.BlockSpec((tm, tk), lambda i, j, kk: (i, kk)),
                  pl.BlockSpec((tk, tn), lambda i, j, kk: (kk, j))],
        out_specs=pl.BlockSpec((tm, tn), lambda i, j, kk: (i, j)),
        out_shape=jax.ShapeDtypeStruct((s, d), F32),
        scratch_shapes=[pltpu.VMEM((tm, tn), F32)],
        compiler_params=_cparams(("parallel", "arbitrary", "arbitrary")),
        name="ffn_down",
    )(x, act, w)


def _unit_lower_inverses(n_mats, size):
    row = lax.broadcasted_iota(jnp.int32, (size, size), 0)
    col = lax.broadcasted_iota(jnp.int32, (size, size), 1)
    eye = jnp.where(row == col, 1.0, 0.0).astype(F32)
    invs = [eye + n for n in n_mats]
    powers = [_dot(n, n, mode=MODE_INV) for n in n_mats]
    steps = int(math.log2(size)) - 1
    for s in range(steps):
        last = s == steps - 1
        for i, (inv, p) in enumerate(zip(invs, powers)):
            if last:
                invs[i] = inv + _dot(inv, p, mode=MODE_INV)
            else:
                both = _dot(jnp.concatenate([inv, p], axis=0), p, mode=MODE_INV)
                invs[i] = inv + both[0:size]
                powers[i] = both[size:2 * size]
    return invs


def _gdn_body(alog_ref, dtb_ref, q_ref, k_ref, v_ref, z_ref, ba_ref, cq_ref, ck_ref, cv_ref,
              nw_ref, o_ref, xbuf, state_ref, *, tile):
    c = GDN_CHUNK
    hd = GDN_DIM
    width = GDN_GROUP * hd
    n_chunks = tile // c
    t = pl.program_id(1)

    @pl.when(t == 0)
    def _reset():
        state_ref[...] = jnp.zeros_like(state_ref)
        xbuf[:, 0:8, :] = jnp.zeros((3, 8, width), F32)

    convs = []
    for idx, (src, cw_ref) in enumerate(((q_ref, cq_ref), (k_ref, ck_ref), (v_ref, cv_ref))):
        xbuf[idx, 8:8 + tile, :] = src[...]
        cw = cw_ref[...]
        acc = cw[0:1, :] * xbuf[idx, 5:5 + tile, :]
        acc += cw[1:2, :] * xbuf[idx, 6:6 + tile, :]
        acc += cw[2:3, :] * xbuf[idx, 7:7 + tile, :]
        acc += cw[3:4, :] * xbuf[idx, 8:8 + tile, :]
        convs.append(_silu(acc))
        xbuf[idx, 0:8, :] = xbuf[idx, tile:tile + 8, :]
    q_all, k_all, v_all = convs

    ba = ba_ref[...]
    lane = lax.broadcasted_iota(jnp.int32, (1, LANE), 1)
    ri = lax.broadcasted_iota(jnp.int32, (c, c), 0)
    ci = lax.broadcasted_iota(jnp.int32, (c, c), 1)
    lower = jnp.where(ci <= ri, 1.0, 0.0).astype(F32)
    upper = jnp.where(ri <= ci, 1.0, 0.0).astype(F32)
    r2 = lax.broadcasted_iota(jnp.int32, (2 * c, c), 0)
    c2 = lax.broadcasted_iota(jnp.int32, (2 * c, c), 1)
    band = c2 <= jnp.where(r2 < c, r2 - 1, r2 - c)
    nw = nw_ref[...]

    units = []
    for j in range(GDN_GROUP):
        h = pl.program_id(0) * GDN_GROUP + j
        cols = slice(j * hd, (j + 1) * hd)
        q, k, v = q_all[:, cols], k_all[:, cols], v_all[:, cols]
        q = q * lax.rsqrt(jnp.sum(q * q, axis=-1, keepdims=True) + 1e-6) * (hd ** -0.5)
        k = k * lax.rsqrt(jnp.sum(k * k, axis=-1, keepdims=True) + 1e-6)
        b_col = jnp.sum(jnp.where(lane == h, ba, 0.0), axis=-1, keepdims=True)
        a_col = jnp.sum(jnp.where(lane == h + GDN_HEADS, ba, 0.0), axis=-1, keepdims=True)
        beta = _sigmoid(b_col)
        g = -jnp.exp(alog_ref[h]) * _softplus(a_col + dtb_ref[h])
        g_b = g * jnp.ones((1, hd), F32)
        g_col = jnp.concatenate([_dot(lower, g_b[n * c:(n + 1) * c], mode=MODE_CUMSUM)
                                 for n in range(n_chunks)], axis=0)
        g_row = jnp.concatenate([_dot(jnp.ones((8, c), F32), g_b[n * c:(n + 1) * c, 0:c] * upper,
                                      mode=MODE_CUMSUM) for n in range(n_chunks)], axis=1)
        exp_g = jnp.exp(g_col)
        kb = k * beta
        vb = v * beta
        kbe = kb * exp_g
        qg = q * exp_g
        for n in range(n_chunks):
            rows = slice(n * c, (n + 1) * c)
            g_last = g_col[n * c + c - 1:n * c + c, :]
            gamma = jnp.exp(jnp.minimum(g_col[rows, 0:c] - g_row[0:1, n * c:(n + 1) * c], 0.0))
            units.append(dict(j=j, n=n, k=k[rows], kq=jnp.concatenate([kb[rows], q[rows]], axis=0),
                              rhs=jnp.concatenate([vb[rows], kbe[rows]], axis=1), qg=qg[rows],
                              kd=k[rows] * jnp.exp(g_last - g_col[rows]), decay=jnp.exp(g_last),
                              gamma2=jnp.concatenate([gamma, gamma], axis=0)))
    for un in units:
        un["scores"] = jnp.where(band, _dot(un["kq"], un["k"], _NT) * un["gamma2"], 0.0)
    t_invs = _unit_lower_inverses([-un["scores"][0:c] for un in units], c)
    for un, t_inv in zip(units, t_invs):
        un["uw"] = _dot(t_inv, un["rhs"], mode=MODE_SOLVE)

    for un in units:
        un["trans"] = _dot(un["kd"], un["uw"][:, hd:2 * hd], _TN, mode=MODE_STATE)
        un["const"] = _dot(un["kd"], un["uw"][:, 0:hd], _TN, mode=MODE_STATE)
    states = [state_ref[j] for j in range(GDN_GROUP)]
    outs = [[None] * n_chunks for _ in range(GDN_GROUP)]
    for n in range(n_chunks):
        cur = [un for un in units if un["n"] == n]
        for un in cur:
            j = un["j"]
            un["s0"] = states[j]
            states[j] = (states[j] * un["decay"] - _dot(un["trans"], states[j], mode=MODE_STATE)
                         + un["const"])
        for un in cur:
            un["ws"] = _dot(jnp.concatenate([un["uw"][:, hd:2 * hd], un["qg"]], axis=0), un["s0"],
                            mode=MODE_STATE)
        for un in cur:
            v_new = un["uw"][:, 0:hd] - un["ws"][0:c]
            outs[un["j"]][n] = un["ws"][c:2 * c] + _dot(un["scores"][c:2 * c], v_new, mode=MODE_STATE)
    for j in range(GDN_GROUP):
        state_ref[j] = states[j]
    for j in range(GDN_GROUP):
        cols = slice(j * hd, (j + 1) * hd)
        o = jnp.concatenate(outs[j], axis=0)
        o = o * lax.rsqrt(jnp.mean(o * o, axis=-1, keepdims=True) + RMS_EPS) * nw
        o_ref[:, cols] = (o * _silu(z_ref[:, cols])).astype(o_ref.dtype)


def _gdn(proj, conv_w, a_log, dt_bias, norm_w, tile=512):
    s = proj.shape[0]
    width = GDN_GROUP * GDN_DIM
    per = LANE // GDN_DIM * GDN_GROUP
    blk = lambda col: pl.BlockSpec((tile, width), lambda h, t: (t, col // per + h))
    cblk = lambda col: pl.BlockSpec((GDN_CONV, width), lambda h, t: (0, col // per + h))
    smem = pl.BlockSpec(memory_space=pltpu.SMEM)
    return pl.pallas_call(
        functools.partial(_gdn_body, tile=tile),
        grid=(GDN_HEADS // GDN_GROUP, s // tile),
        in_specs=[smem, smem,
                  blk(COL_GDN_Q), blk(COL_GDN_K), blk(COL_GDN_V), blk(COL_GDN_Z),
                  pl.BlockSpec((tile, LANE), lambda h, t: (t, COL_GDN_BA)),
                  cblk(0), cblk(GDN_HEADS), cblk(2 * GDN_HEADS),
                  pl.BlockSpec((1, GDN_DIM), lambda h, t: (0, 0))],
        out_specs=pl.BlockSpec((tile, width), lambda h, t: (t, h)),
        out_shape=jax.ShapeDtypeStruct((s, GDN_WIDTH), BF16),
        scratch_shapes=[pltpu.VMEM((3, tile + 8, width), F32),
                        pltpu.VMEM((GDN_GROUP, GDN_DIM, GDN_DIM), F32)],
        compiler_params=_cparams(("parallel", "arbitrary")),
        name="gdn",
    )(a_log, dt_bias, proj, proj, proj, proj, proj, conv_w, conv_w, conv_w, norm_w.reshape(1, GDN_DIM))


def _t5_bucket_table():
    exact = NUM_BUCKETS // 2
    i = np.arange(SWA_BLK)[:, None]
    j = np.arange(2 * SWA_BLK)[None, :]
    steps = np.maximum(i + SWA_BLK - j, 0)
    tables = []
    for _, dilation in SWA_PATTERNS:
        dist = steps * dilation
        d = np.maximum(dist, 1).astype(np.float32)
        ratio = (np.log(d / np.float32(exact)) / np.float32(math.log(MAX_DISTANCE / exact))
                 * np.float32(NUM_BUCKETS - exact)).astype(np.float32)
        log_b = exact + ratio.astype(np.int32)
        tables.append(np.where(dist < exact, dist, np.minimum(log_b, NUM_BUCKETS - 1)))
    return np.stack(tables).astype(np.int32)


def _swa_body(rb_ref, bkt_ref, q_ref, kc_ref, kp_ref, vc_ref, vp_ref, o_ref,
              bias_ref, kbuf, vbuf, o_scr, m_scr, l_scr, *, tile):
    h = pl.program_id(0)
    t = pl.program_id(1)
    blk = SWA_BLK

    @pl.when(t == 0)
    def _bias():
        for p in range(len(SWA_PATTERNS)):
            bkt = bkt_ref[p]
            bias = jnp.zeros((blk, 2 * blk), F32)
            for b in range(NUM_BUCKETS):
                bias = jnp.where(bkt == b, rb_ref[b, h], bias)
            bias_ref[p] = bias

    kbuf[0:tile, :] = kp_ref[...]
    kbuf[tile:2 * tile, :] = kc_ref[...]
    vbuf[0:tile, :] = vp_ref[...]
    vbuf[tile:2 * tile, :] = vc_ref[...]

    qi = lax.broadcasted_iota(jnp.int32, (blk, 2 * blk), 0)
    kj = lax.broadcasted_iota(jnp.int32, (blk, 2 * blk), 1)
    in_band = (kj >= qi) & (kj <= qi + blk)
    scale = SWA_DIM ** -0.5

    band_first = in_band & (kj >= jnp.where(t == 0, blk, 0))
    ones_row = jnp.ones((1, SWA_DIM), F32)

    for p, (window, dil) in enumerate(SWA_PATTERNS):
        span = blk * dil
        bias = bias_ref[p]
        blocks = [(res + n * span, n) for n in range(tile // span) for res in range(dil)]
        for g0 in range(0, len(blocks), SWA_GROUP):
            group = blocks[g0:g0 + SWA_GROUP]
            scores = []
            for start, n in group:
                qb = q_ref[pl.ds(start, blk, stride=dil), :] * scale
                kw = kbuf[pl.ds(tile + start - span, 2 * blk, stride=dil), :]
                sc = _dot(qb, kw, _NT) + bias
                scores.append(jnp.where(band_first if n == 0 else in_band, sc, NEG_INF))
            probs = []
            for sc in scores:
                m = jnp.max(sc, axis=-1, keepdims=True)
                pe = jnp.exp(sc - m)
                probs.append((pe, m, jnp.sum(pe, axis=-1, keepdims=True)))
            for (start, n), (pe, m, l) in zip(group, probs):
                vw = vbuf[pl.ds(tile + start - span, 2 * blk, stride=dil), :]
                o_scr[p, pl.ds(start, blk, stride=dil), :] = _dot(pe, vw)
                m_scr[p, pl.ds(start, blk, stride=dil), :] = m * ones_row
                l_scr[p, pl.ds(start, blk, stride=dil), :] = l * ones_row

    m_max = jnp.maximum(jnp.maximum(m_scr[0], m_scr[1]), m_scr[2])
    num = jnp.zeros((tile, SWA_DIM), F32)
    den = jnp.zeros((tile, SWA_DIM), F32)
    for p in range(len(SWA_PATTERNS)):
        sc = jnp.exp(m_scr[p] - m_max)
        num += o_scr[p] * sc
        den += l_scr[p] * sc
    o_ref[...] = (num / den).astype(o_ref.dtype)


def _swa(proj, rel_bias, tile=SWA_TILE):
    s = proj.shape[0]
    n_pat = len(SWA_PATTERNS)
    bkt = jnp.asarray(_t5_bucket_table())
    cur = lambda col: pl.BlockSpec((tile, LANE), lambda h, t: (t, col + h))
    prev = lambda col: pl.BlockSpec((tile, LANE), lambda h, t: (jnp.maximum(t - 1, 0), col + h))
    return pl.pallas_call(
        functools.partial(_swa_body, tile=tile),
        grid=(SWA_HEADS, s // tile),
        in_specs=[pl.BlockSpec(memory_space=pltpu.SMEM),
                  pl.BlockSpec((n_pat, SWA_BLK, 2 * SWA_BLK), lambda h, t: (0, 0, 0)),
                  cur(COL_SWA_Q), cur(COL_SWA_K), prev(COL_SWA_K), cur(COL_SWA_V), prev(COL_SWA_V)],
        out_specs=pl.BlockSpec((tile, LANE), lambda h, t: (t, h)),
        out_shape=jax.ShapeDtypeStruct((s, SWA_WIDTH), BF16),
        scratch_shapes=[pltpu.VMEM((n_pat, SWA_BLK, 2 * SWA_BLK), F32),
                        pltpu.VMEM((2 * tile, SWA_DIM), F32),
                        pltpu.VMEM((2 * tile, SWA_DIM), F32),
                        pltpu.VMEM((n_pat, tile, SWA_DIM), F32),
                        pltpu.VMEM((n_pat, tile, SWA_DIM), F32),
                        pltpu.VMEM((n_pat, tile, SWA_DIM), F32)],
        compiler_params=_cparams(("parallel", "arbitrary")),
        name="swa",
    )(rel_bias, bkt, proj, proj, proj, proj, proj)


PRM_W0, PRM_A0, PRM_KK, PRM_KA, PRM_RK, PRM_LNW, PRM_LNB, PRM_MUR, PRM_MUK, PRM_MUV = range(10)
PRM_ROWS = 16


def _rwkv_body(r_ref, k_ref, v_ref, wd_ref, ad_ref, gd_ref, prm_ref, mul_ref, mug_ref,
               wup_ref, aup_ref, gup_ref, o_ref, xb, xg, state_ref, *, tile):
    c = RWKV_CHUNK
    nd = RWKV_DIM
    t = pl.program_id(1)

    @pl.when(t == 0)
    def _reset():
        state_ref[...] = jnp.zeros_like(state_ref)
        xb[:, 0:8, :] = jnp.zeros((5, 8, LANE), F32)
        xg[0:8, :] = jnp.zeros((8, GATE_LORA_PAD), F32)

    prm = prm_ref[...]
    row = lambda i: prm[i:i + 1, :]

    def shifted(idx, src, mu):
        xb[idx, 8:8 + tile, :] = src[...]
        cur = xb[idx, 8:8 + tile, :]
        prev = xb[idx, 7:7 + tile, :]
        xb[idx, 0:8, :] = xb[idx, tile:tile + 8, :]
        return cur + mu * (prev - cur)

    r = shifted(0, r_ref, row(PRM_MUR))
    k = shifted(1, k_ref, row(PRM_MUK))
    v = shifted(2, v_ref, row(PRM_MUV))
    wd = shifted(3, wd_ref, mul_ref[0:1, :])
    ad = shifted(4, ad_ref, mul_ref[1:2, :])
    xg[8:8 + tile, :] = gd_ref[...]
    gcur = xg[8:8 + tile, :]
    gprev = xg[7:7 + tile, :]
    xg[0:8, :] = xg[tile:tile + 8, :]
    gd = gcur + mug_ref[...] * (gprev - gcur)

    w_log = -_softplus(-(row(PRM_W0) + _dot(jnp.tanh(wd), wup_ref[...]))) - 0.5
    log_w = -jnp.exp(w_log)
    a_gate = _sigmoid(row(PRM_A0) + _dot(ad, aup_ref[...]))
    g_gate = _dot(_sigmoid(gd), gup_ref[...])
    kx = k * row(PRM_KK)
    k_mod = k * (1.0 + (a_gate - 1.0) * row(PRM_KA))

    ri = lax.broadcasted_iota(jnp.int32, (c, c), 0)
    ci = lax.broadcasted_iota(jnp.int32, (c, c), 1)
    lower = jnp.where(ci <= ri, 1.0, 0.0).astype(F32)
    cum = jnp.concatenate([_dot(lower, log_w[n * c:(n + 1) * c], mode=MODE_CUMSUM)
                           for n in range(tile // c)], axis=0)

    r2 = lax.broadcasted_iota(jnp.int32, (2 * c, 2 * c), 0)
    c2 = lax.broadcasted_iota(jnp.int32, (2 * c, 2 * c), 1)
    col_in = jnp.where(c2 < c, c2, c2 - c)
    band = col_in <= jnp.where(r2 < c, r2 - 1, r2 - c)

    n_heads = LANE // nd
    n_chunks = tile // c
    lane = lax.broadcasted_iota(jnp.int32, (1, LANE), 1)
    head_masks = [(lane // nd) == hh for hh in range(n_heads)]

    def head_sum(x):
        out = jnp.zeros_like(x)
        for msk in head_masks:
            out = jnp.where(msk, jnp.sum(jnp.where(msk, x, 0.0), axis=-1, keepdims=True), out)
        return out

    kk = kx * lax.rsqrt(head_sum(kx * kx) + 1e-12)
    b_vec = kk * a_gate
    c_last = jnp.concatenate(
        [jnp.broadcast_to(cum[n * c + c - 1:n * c + c, :], (c, LANE)) for n in range(n_chunks)], axis=0)
    p_out = jnp.exp(-cum)
    a_t = -kk * jnp.exp(cum - log_w)
    r_t = r * jnp.exp(cum)
    k_t = k_mod * p_out
    b_t = b_vec * p_out
    decay_end = jnp.exp(c_last - cum)
    k_e = k_mod * decay_end
    b_e = b_vec * decay_end
    decay_all = jnp.exp(c_last)

    units = []
    for hh in range(n_heads):
        lanes = slice(hh * nd, (hh + 1) * nd)
        for n in range(n_chunks):
            rows = slice(n * c, (n + 1) * c)
            units.append(dict(hh=hh, n=n, a=a_t[rows, lanes], r=r_t[rows, lanes], v=v[rows, lanes],
                              ar=jnp.concatenate([a_t[rows, lanes], r_t[rows, lanes]], axis=0),
                              bk=jnp.concatenate([b_t[rows, lanes], k_t[rows, lanes]], axis=0),
                              bk_e=jnp.concatenate([b_e[rows, lanes], k_e[rows, lanes]], axis=0),
                              decay=decay_all[n * c:n * c + 1, lanes]))
    for un in units:
        un["mm"] = jnp.where(band, _dot(un["ar"], un["bk"], _NT), 0.0)
    t_invs = _unit_lower_inverses([un["mm"][0:c, 0:c] for un in units], c)
    for un in units:
        un["mv"] = _dot(un["mm"][0:c, c:2 * c], un["v"], mode=MODE_SOLVE)
    for un, t_inv in zip(units, t_invs):
        un["wu"] = _dot(t_inv, jnp.concatenate([un["a"], un["mv"]], axis=1), mode=MODE_SOLVE)

    for un in units:
        un["trans"] = _dot(un["wu"][:, 0:nd], un["bk_e"][0:c], _TN, mode=MODE_STATE)
        un["const"] = _dot(jnp.concatenate([un["wu"][:, nd:2 * nd], un["v"]], axis=0), un["bk_e"], _TN,
                           mode=MODE_STATE)
    states = [state_ref[hh] for hh in range(n_heads)]
    ys = [[None] * n_chunks for _ in range(n_heads)]
    for n in range(n_chunks):
        cur = [un for un in units if un["n"] == n]
        for un in cur:
            hh = un["hh"]
            un["s0"] = states[hh]
            states[hh] = (states[hh] * un["decay"] + _dot(states[hh], un["trans"], mode=MODE_STATE)
                          + un["const"])
        for un in cur:
            un["sr"] = _dot(jnp.concatenate([un["wu"][:, 0:nd], un["r"]], axis=0), un["s0"], _NT,
                            mode=MODE_STATE)
        for un in cur:
            u = un["sr"][0:c] + un["wu"][:, nd:2 * nd]
            uv = jnp.concatenate([u, un["v"]], axis=0)
            ys[un["hh"]][n] = un["sr"][c:2 * c] + _dot(un["mm"][c:2 * c], uv, mode=MODE_STATE)
    for hh in range(n_heads):
        state_ref[hh] = states[hh]
    y = jnp.concatenate([jnp.concatenate(ys[hh], axis=0) for hh in range(n_heads)], axis=1)
    mean = head_sum(y) * (1.0 / nd)
    var = head_sum(jnp.square(y - mean)) * (1.0 / nd)
    y_n = (y - mean) * lax.rsqrt(var + GN_EPS) * row(PRM_LNW) + row(PRM_LNB)
    bonus = head_sum(r * k_mod * row(PRM_RK)) * v
    o_ref[...] = ((y_n + bonus) * g_gate).astype(o_ref.dtype)


def _rwkv(proj, prm, mu_lora, mu_gate, w_up, a_up, g_up, tile=512):
    s = proj.shape[0]
    pairs = RWKV_WIDTH // LANE
    blk = lambda col: pl.BlockSpec((tile, LANE), lambda h, t: (t, col + h))
    fixed = lambda col: pl.BlockSpec((tile, LANE), lambda h, t: (t, col))
    return pl.pallas_call(
        functools.partial(_rwkv_body, tile=tile),
        grid=(pairs, s // tile),
        in_specs=[blk(COL_R), blk(COL_K), blk(COL_V), fixed(COL_WD), fixed(COL_AD),
                  pl.BlockSpec((tile, GATE_LORA_PAD), lambda h, t: (t, COL_GD * LANE // GATE_LORA_PAD)),
                  pl.BlockSpec((PRM_ROWS, LANE), lambda h, t: (0, h)),
                  pl.BlockSpec((8, LANE), lambda h, t: (0, 0)),
                  pl.BlockSpec((1, GATE_LORA_PAD), lambda h, t: (0, 0)),
                  pl.BlockSpec((DECAY_LORA, LANE), lambda h, t: (0, h)),
                  pl.BlockSpec((ICL_LORA, LANE), lambda h, t: (0, h)),
                  pl.BlockSpec((GATE_LORA_PAD, LANE), lambda h, t: (0, h))],
        out_specs=pl.BlockSpec((tile, LANE), lambda h, t: (t, h)),
        out_shape=jax.ShapeDtypeStruct((s, RWKV_WIDTH), BF16),
        scratch_shapes=[pltpu.VMEM((5, tile + 8, LANE), F32),
                        pltpu.VMEM((tile + 8, GATE_LORA_PAD), F32),
                        pltpu.VMEM((LANE // RWKV_DIM, RWKV_DIM, RWKV_DIM), F32)],
        compiler_params=_cparams(("parallel", "arbitrary")),
        name="rwkv7",
    )(proj, proj, proj, proj, proj, proj, prm, mu_lora, mu_gate, w_up, a_up, g_up)


def _layout_w_in(w):
    d = w.shape[0]
    gw, sw, rw = GDN_WIDTH, SWA_WIDTH, RWKV_WIDTH
    o = 0
    gdn_main = w[:, o:o + 4 * gw]; o += 4 * gw
    gdn_ba = w[:, o:o + 2 * GDN_HEADS]; o += 2 * GDN_HEADS
    swa = w[:, o:o + 3 * sw]; o += 3 * sw
    rkv = w[:, o:o + 3 * rw]; o += 3 * rw
    wd = w[:, o:o + DECAY_LORA]; o += DECAY_LORA
    ad = w[:, o:o + ICL_LORA]; o += ICL_LORA
    gd = w[:, o:o + GATE_LORA]
    z = lambda n: jnp.zeros((d, n), w.dtype)
    out = jnp.concatenate([gdn_main, gdn_ba, z(LANE - 2 * GDN_HEADS), swa, wd, ad, z(LANE),
                           gd, z(GATE_LORA_PAD - GATE_LORA), rkv], axis=1)
    assert out.shape[1] == PROJ_WIDTH
    return out.astype(BF16)


def _layout_rwkv_params(mu, w0, a0, k_k, k_a, r_k, ln_w, ln_b):
    rw = RWKV_WIDTH
    mu_r, mu_k, mu_v = mu[0:rw], mu[rw:2 * rw], mu[2 * rw:3 * rw]
    o = 3 * rw
    mu_wd = mu[o:o + DECAY_LORA]; o += DECAY_LORA
    mu_ad = mu[o:o + ICL_LORA]; o += ICL_LORA
    mu_gd = mu[o:o + GATE_LORA]
    rows = [w0, a0, k_k, k_a, r_k.reshape(rw), ln_w, ln_b, mu_r, mu_k, mu_v]
    prm = jnp.stack(rows + [jnp.zeros((rw,), F32)] * (PRM_ROWS - len(rows))).astype(F32)
    mu_lora = jnp.stack([mu_wd, mu_ad] + [jnp.zeros((LANE,), F32)] * 6).astype(F32)
    mu_gate = jnp.concatenate([mu_gd, jnp.zeros((GATE_LORA_PAD - GATE_LORA,), F32)]).reshape(1, GATE_LORA_PAD)
    return prm, mu_lora, mu_gate


def kernel(x, attn_norm, w_in, gdn_conv, gdn_a_log, gdn_dt_bias, gdn_norm, rwkv_mu, rwkv_w0, rwkv_w_up, rwkv_a0, rwkv_a_up, rwkv_g_up, rwkv_k_k, rwkv_k_a, rwkv_r_k, rwkv_ln_w, rwkv_ln_b, w_out, ffn_norm, w_ffn_gate, w_ffn_up, ffn_conv, ffn_conv_b, w_ffn_down, rel_bias, final_norm):
    batch, seq, d = x.shape
    depth = w_in.shape[0]
    outs = []
    rows_in = [x.reshape(seq, d)] if batch == 1 else [x[b] for b in range(batch)]
    for xb in rows_in:
        for l in range(depth):
            h = _rmsnorm(xb, attn_norm[l], BF16)
            proj = _in_proj(h, _layout_w_in(_to_bf16(w_in, l)))
            o_a = _gdn(proj, gdn_conv[l], gdn_a_log[l], gdn_dt_bias[l], gdn_norm[l])
            o_b = _swa(proj, rel_bias)
            prm, mu_lora, mu_gate = _layout_rwkv_params(
                rwkv_mu[l], rwkv_w0[l], rwkv_a0[l], rwkv_k_k[l], rwkv_k_a[l], rwkv_r_k[l],
                rwkv_ln_w[l], rwkv_ln_b[l])
            g_up = jnp.concatenate(
                [rwkv_g_up[l], jnp.zeros((GATE_LORA_PAD - GATE_LORA, RWKV_WIDTH), F32)], axis=0)
            o_c = _rwkv(proj, prm, mu_lora, mu_gate, rwkv_w_up[l].astype(BF16),
                        rwkv_a_up[l].astype(BF16), g_up.astype(BF16))
            wo = _to_bf16(w_out, l)
            xb = _out_proj(xb, o_a, o_b, o_c, wo[0:GDN_WIDTH], wo[GDN_WIDTH:GDN_WIDTH + SWA_WIDTH],
                           wo[GDN_WIDTH + SWA_WIDTH:])
            h = _rmsnorm(xb, ffn_norm[l], BF16)
            act = _ffn_up(h, _to_bf16(w_ffn_gate, l), _to_bf16(w_ffn_up, l), ffn_conv[l], ffn_conv_b[l])
            xb = _ffn_down(xb, act, _to_bf16(w_ffn_down, l))
        outs.append(_rmsnorm(xb, final_norm, x.dtype))
    return outs[0].reshape(1, seq, d) if batch == 1 else jnp.stack(outs)
```

```python
import functools
import math

import numpy as np
import jax
import jax.numpy as jnp
from jax import lax
from jax.experimental import pallas as pl
from jax.experimental.pallas import tpu as pltpu

F32 = jnp.float32
BF16 = jnp.bfloat16
HIGHEST = lax.Precision.HIGHEST

LANE = 128
D_MODEL = 4096
RMS_EPS = 1e-6
GN_EPS = 64e-5
NEG_INF = -1e30
VMEM_LIMIT = 56 * 1024 * 1024

GDN_DIM = 128
GDN_HEADS = 12
GDN_WIDTH = GDN_HEADS * GDN_DIM
GDN_CONV = 4
GDN_CHUNK = 64
GDN_GROUP = 2
SWA_DIM = 128
SWA_HEADS = 8
SWA_WIDTH = SWA_HEADS * SWA_DIM
SWA_PATTERNS = ((128, 1), (512, 4), (2048, 16))
SWA_BLK = 128
SWA_TILE = 2048
SWA_GROUP = 4
NUM_BUCKETS = 32
MAX_DISTANCE = 2048
RWKV_DIM = 64
RWKV_HEADS = 24
RWKV_WIDTH = RWKV_HEADS * RWKV_DIM
RWKV_CHUNK = 64
DECAY_LORA = 128
ICL_LORA = 128
GATE_LORA = 480
GATE_LORA_PAD = 512
D_FF = 11008
FFN_CONV = 3

COL_GDN_Q, COL_GDN_K, COL_GDN_V, COL_GDN_Z, COL_GDN_BA = 0, 12, 24, 36, 48
COL_SWA_Q, COL_SWA_K, COL_SWA_V = 49, 57, 65
COL_WD, COL_AD = 73, 74
COL_GD = 76
COL_R, COL_K, COL_V = 80, 92, 104
PROJ_BLOCKS = 116
PROJ_WIDTH = PROJ_BLOCKS * LANE


def _cparams(sem):
    return pltpu.CompilerParams(dimension_semantics=sem, vmem_limit_bytes=VMEM_LIMIT)


def _silu(x):
    return x * (1.0 / (1.0 + jnp.exp(-x)))


def _sigmoid(x):
    return 1.0 / (1.0 + jnp.exp(-x))


def _softplus(x):
    return jnp.maximum(x, 0.0) + jnp.log(1.0 + jnp.exp(-jnp.abs(x)))


def _split_bf16(x):
    hi = x.astype(BF16)
    return hi, (x - hi.astype(F32)).astype(BF16)


def _dot(a, b, dims=(((1,), (0,)), ((), ())), mode="bf16"):
    if mode == "f32":
        return lax.dot_general(a.astype(F32), b.astype(F32), dims, precision=HIGHEST,
                               preferred_element_type=F32)
    if mode == "x3":
        a_hi, a_lo = _split_bf16(a)
        b_hi, b_lo = _split_bf16(b)
        mm = lambda p, q: lax.dot_general(p, q, dims, preferred_element_type=F32)
        return mm(a_hi, b_hi) + (mm(a_hi, b_lo) + mm(a_lo, b_hi))
    return lax.dot_general(a.astype(BF16), b.astype(BF16), dims, preferred_element_type=F32)


MODE_CUMSUM = "f32"
MODE_INV = "bf16"
MODE_SOLVE = "bf16"
MODE_STATE = "bf16"


_NT = (((1,), (1,)), ((), ()))
_TN = (((0,), (0,)), ((), ()))


def _rmsnorm_body(x_ref, w_ref, o_ref):
    x = x_ref[...]
    ms = jnp.mean(x * x, axis=-1, keepdims=True)
    o_ref[...] = (x * lax.rsqrt(ms + RMS_EPS) * w_ref[...]).astype(o_ref.dtype)


def _rmsnorm(x, w, out_dtype, tm=512):
    s, d = x.shape
    return pl.pallas_call(
        _rmsnorm_body,
        grid=(s // tm,),
        in_specs=[pl.BlockSpec((tm, d), lambda i: (i, 0)),
                  pl.BlockSpec((1, d), lambda i: (0, 0))],
        out_specs=pl.BlockSpec((tm, d), lambda i: (i, 0)),
        out_shape=jax.ShapeDtypeStruct((s, d), out_dtype),
        compiler_params=_cparams(("parallel",)),
        name="rmsnorm",
    )(x, w.reshape(1, d))


def _cast_body(w_ref, o_ref):
    o_ref[...] = w_ref[...].astype(o_ref.dtype)


def _to_bf16(w, layer, rows=256):
    _, r, c = w.shape
    return pl.pallas_call(
        _cast_body,
        grid=(r // rows,),
        in_specs=[pl.BlockSpec((None, rows, c), lambda i: (layer, i, 0))],
        out_specs=pl.BlockSpec((rows, c), lambda i: (i, 0)),
        out_shape=jax.ShapeDtypeStruct((r, c), BF16),
        compiler_params=_cparams(("parallel",)),
        name="to_bf16",
    )(w)


def _matmul_body(a_ref, b_ref, o_ref):
    o_ref[...] = jnp.dot(a_ref[...], b_ref[...], preferred_element_type=F32).astype(o_ref.dtype)


def _in_proj(h, w, tm=1024, tn=512):
    s, k = h.shape
    n = w.shape[1]
    return pl.pallas_call(
        _matmul_body,
        grid=(s // tm, n // tn),
        in_specs=[pl.BlockSpec((tm, k), lambda i, j: (i, 0)),
                  pl.BlockSpec((k, tn), lambda i, j: (0, j))],
        out_specs=pl.BlockSpec((tm, tn), lambda i, j: (i, j)),
        out_shape=jax.ShapeDtypeStruct((s, n), F32),
        compiler_params=_cparams(("parallel", "arbitrary")),
        name="in_proj",
    )(h, w)


def _out_proj_body(x_ref, a_ref, b_ref, c_ref, wa_ref, wb_ref, wc_ref, o_ref):
    acc = jnp.dot(a_ref[...], wa_ref[...], preferred_element_type=F32)
    acc += jnp.dot(b_ref[...], wb_ref[...], preferred_element_type=F32)
    acc += jnp.dot(c_ref[...], wc_ref[...], preferred_element_type=F32)
    o_ref[...] = x_ref[...] + acc


def _out_proj(x, oa, ob, oc, wa, wb, wc, tm=1024, tn=512):
    s, d = x.shape
    ka, kb, kc = oa.shape[1], ob.shape[1], oc.shape[1]
    return pl.pallas_call(
        _out_proj_body,
        grid=(s // tm, d // tn),
        in_specs=[pl.BlockSpec((tm, tn), lambda i, j: (i, j)),
                  pl.BlockSpec((tm, ka), lambda i, j: (i, 0)),
                  pl.BlockSpec((tm, kb), lambda i, j: (i, 0)),
                  pl.BlockSpec((tm, kc), lambda i, j: (i, 0)),
                  pl.BlockSpec((ka, tn), lambda i, j: (0, j)),
                  pl.BlockSpec((kb, tn), lambda i, j: (0, j)),
                  pl.BlockSpec((kc, tn), lambda i, j: (0, j))],
        out_specs=pl.BlockSpec((tm, tn), lambda i, j: (i, j)),
        out_shape=jax.ShapeDtypeStruct((s, d), F32),
        compiler_params=_cparams(("parallel", "arbitrary")),
        name="out_proj",
    )(x, oa, ob, oc, wa, wb, wc)


FFN_HALO = 16


def _ffn_up_body(h_ref, halo_ref, wg_ref, wu_ref, cw_ref, cb_ref, o_ref, hbuf, *, tm):
    i = pl.program_id(0)

    @pl.when(pl.program_id(1) == 0)
    def _stage():
        halo = halo_ref[...]
        hbuf[0:FFN_HALO, :] = jnp.where(i == 0, jnp.zeros_like(halo), halo)
        hbuf[FFN_HALO:FFN_HALO + tm, :] = h_ref[...]

    g = jnp.dot(hbuf[...], wg_ref[...], preferred_element_type=F32)
    u = jnp.dot(hbuf[FFN_HALO:FFN_HALO + tm, :], wu_ref[...], preferred_element_type=F32)
    cw = cw_ref[...]
    conv = (cw[0:1, :] * pltpu.roll(g, 2, 0)[FFN_HALO:, :]
            + cw[1:2, :] * pltpu.roll(g, 1, 0)[FFN_HALO:, :]
            + cw[2:3, :] * g[FFN_HALO:, :]) + cb_ref[...]
    o_ref[...] = (_silu(conv) * u).astype(o_ref.dtype)


def _ffn_up(h, wg, wu, cw, cb, tm=1024, tn=256):
    s, k = h.shape
    f = wg.shape[1]
    halo_blocks = tm // FFN_HALO
    return pl.pallas_call(
        functools.partial(_ffn_up_body, tm=tm),
        grid=(s // tm, f // tn),
        in_specs=[pl.BlockSpec((tm, k), lambda i, j: (i, 0)),
                  pl.BlockSpec((FFN_HALO, k), lambda i, j: (jnp.maximum(i * halo_blocks - 1, 0), 0)),
                  pl.BlockSpec((k, tn), lambda i, j: (0, j)),
                  pl.BlockSpec((k, tn), lambda i, j: (0, j)),
                  pl.BlockSpec((FFN_CONV, tn), lambda i, j: (0, j)),
                  pl.BlockSpec((1, tn), lambda i, j: (0, j))],
        out_specs=pl.BlockSpec((tm, tn), lambda i, j: (i, j)),
        out_shape=jax.ShapeDtypeStruct((s, f), BF16),
        scratch_shapes=[pltpu.VMEM((tm + FFN_HALO, k), BF16)],
        compiler_params=_cparams(("parallel", "arbitrary")),
        name="ffn_up",
    )(h, h, wg, wu, cw, cb.reshape(1, f))


def _ffn_down_body(x_ref, a_ref, w_ref, o_ref, acc_ref):
    kk = pl.program_id(2)

    @pl.when(kk == 0)
    def _init():
        acc_ref[...] = x_ref[...]

    acc_ref[...] += jnp.dot(a_ref[...], w_ref[...], preferred_element_type=F32)

    @pl.when(kk == pl.num_programs(2) - 1)
    def _done():
        o_ref[...] = acc_ref[...]


def _ffn_down(x, act, w, tm=1024, tn=512, ksplit=2):
    s, d = x.shape
    f = act.shape[1]
    tk = f // ksplit
    return pl.pallas_call(
        _ffn_down_body,
        grid=(s // tm, d // tn, ksplit),
        in_specs=[pl.BlockSpec((tm, tn), lambda i, j, kk: (i, j)),
                  pl.BlockSpec((tm, tk), lambda i, j, kk: (i, kk)),
                  pl.BlockSpec((tk, tn), lambda i, j, kk: (kk, j))],
        out_specs=pl.BlockSpec((tm, tn), lambda i, j, kk: (i, j)),
        out_shape=jax.ShapeDtypeStruct((s, d), F32),
        scratch_shapes=[pltpu.VMEM((tm, tn), F32)],
        compiler_params=_cparams(("parallel", "arbitrary", "arbitrary")),
        name="ffn_down",
    )(x, act, w)


def _unit_lower_inverses(n_mats, size):
    row = lax.broadcasted_iota(jnp.int32, (size, size), 0)
    col = lax.broadcasted_iota(jnp.int32, (size, size), 1)
    eye = jnp.where(row == col, 1.0, 0.0).astype(F32)
    invs = [eye + n for n in n_mats]
    powers = [_dot(n, n, mode=MODE_INV) for n in n_mats]
    steps = int(math.log2(size)) - 1
    for s in range(steps):
        last = s == steps - 1
        for i, (inv, p) in enumerate(zip(invs, powers)):
            if last:
                invs[i] = inv + _dot(inv, p, mode=MODE_INV)
            else:
                both = _dot(jnp.concatenate([inv, p], axis=0), p, mode=MODE_INV)
                invs[i] = inv + both[0:size]
                powers[i] = both[size:2 * size]
    return invs


def _gdn_body(alog_ref, dtb_ref, q_ref, k_ref, v_ref, z_ref, ba_ref, cq_ref, ck_ref, cv_ref,
              nw_ref, o_ref, xbuf, state_ref, *, tile):
    c = GDN_CHUNK
    hd = GDN_DIM
    width = GDN_GROUP * hd
    n_chunks = tile // c
    t = pl.program_id(1)

    @pl.when(t == 0)
    def _reset():
        state_ref[...] = jnp.zeros_like(state_ref)
        xbuf[:, 0:8, :] = jnp.zeros((3, 8, width), F32)

    convs = []
    for idx, (src, cw_ref) in enumerate(((q_ref, cq_ref), (k_ref, ck_ref), (v_ref, cv_ref))):
        xbuf[idx, 8:8 + tile, :] = src[...]
        cw = cw_ref[...]
        acc = cw[0:1, :] * xbuf[idx, 5:5 + tile, :]
        acc += cw[1:2, :] * xbuf[idx, 6:6 + tile, :]
        acc += cw[2:3, :] * xbuf[idx, 7:7 + tile, :]
        acc += cw[3:4, :] * xbuf[idx, 8:8 + tile, :]
        convs.append(_silu(acc))
        xbuf[idx, 0:8, :] = xbuf[idx, tile:tile + 8, :]
    q_all, k_all, v_all = convs

    ba = ba_ref[...]
    lane = lax.broadcasted_iota(jnp.int32, (1, LANE), 1)
    ri = lax.broadcasted_iota(jnp.int32, (c, c), 0)
    ci = lax.broadcasted_iota(jnp.int32, (c, c), 1)
    lower = jnp.where(ci <= ri, 1.0, 0.0).astype(F32)
    upper = jnp.where(ri <= ci, 1.0, 0.0).astype(F32)
    r2 = lax.broadcasted_iota(jnp.int32, (2 * c, c), 0)
    c2 = lax.broadcasted_iota(jnp.int32, (2 * c, c), 1)
    band = c2 <= jnp.where(r2 < c, r2 - 1, r2 - c)
    nw = nw_ref[...]

    units = []
    for j in range(GDN_GROUP):
        h = pl.program_id(0) * GDN_GROUP + j
        cols = slice(j * hd, (j + 1) * hd)
        q, k, v = q_all[:, cols], k_all[:, cols], v_all[:, cols]
        q = q * lax.rsqrt(jnp.sum(q * q, axis=-1, keepdims=True) + 1e-6) * (hd ** -0.5)
        k = k * lax.rsqrt(jnp.sum(k * k, axis=-1, keepdims=True) + 1e-6)
        b_col = jnp.sum(jnp.where(lane == h, ba, 0.0), axis=-1, keepdims=True)
        a_col = jnp.sum(jnp.where(lane == h + GDN_HEADS, ba, 0.0), axis=-1, keepdims=True)
        beta = _sigmoid(b_col)
        g = -jnp.exp(alog_ref[h]) * _softplus(a_col + dtb_ref[h])
        g_b = g * jnp.ones((1, hd), F32)
        g_col = jnp.concatenate([_dot(lower, g_b[n * c:(n + 1) * c], mode=MODE_CUMSUM)
                                 for n in range(n_chunks)], axis=0)
        g_row = jnp.concatenate([_dot(jnp.ones((8, c), F32), g_b[n * c:(n + 1) * c, 0:c] * upper,
                                      mode=MODE_CUMSUM) for n in range(n_chunks)], axis=1)
        exp_g = jnp.exp(g_col)
        kb = k * beta
        vb = v * beta
        kbe = kb * exp_g
        qg = q * exp_g
        for n in range(n_chunks):
            rows = slice(n * c, (n + 1) * c)
            g_last = g_col[n * c + c - 1:n * c + c, :]
            gamma = jnp.exp(jnp.minimum(g_col[rows, 0:c] - g_row[0:1, n * c:(n + 1) * c], 0.0))
            units.append(dict(j=j, n=n, k=k[rows], kq=jnp.concatenate([kb[rows], q[rows]], axis=0),
                              rhs=jnp.concatenate([vb[rows], kbe[rows]], axis=1), qg=qg[rows],
                              kd=k[rows] * jnp.exp(g_last - g_col[rows]), decay=jnp.exp(g_last),
                              gamma2=jnp.concatenate([gamma, gamma], axis=0)))
    for un in units:
        un["scores"] = jnp.where(band, _dot(un["kq"], un["k"], _NT) * un["gamma2"], 0.0)
    t_invs = _unit_lower_inverses([-un["scores"][0:c] for un in units], c)
    for un, t_inv in zip(units, t_invs):
        un["uw"] = _dot(t_inv, un["rhs"], mode=MODE_SOLVE)

    for un in units:
        un["trans"] = _dot(un["kd"], un["uw"][:, hd:2 * hd], _TN, mode=MODE_STATE)
        un["const"] = _dot(un["kd"], un["uw"][:, 0:hd], _TN, mode=MODE_STATE)
    states = [state_ref[j] for j in range(GDN_GROUP)]
    for n in range(n_chunks):
        for un in units:
            if un["n"] != n:
                continue
            j = un["j"]
            un["s0"] = states[j]
            states[j] = (states[j] * un["decay"] - _dot(un["trans"], states[j], mode=MODE_STATE)
                         + un["const"])
    for j in range(GDN_GROUP):
        state_ref[j] = states[j]
    outs = [[None] * n_chunks for _ in range(GDN_GROUP)]
    for un in units:
        un["ws"] = _dot(jnp.concatenate([un["uw"][:, hd:2 * hd], un["qg"]], axis=0), un["s0"],
                        mode=MODE_STATE)
    for un in units:
        v_new = un["uw"][:, 0:hd] - un["ws"][0:c]
        outs[un["j"]][un["n"]] = un["ws"][c:2 * c] + _dot(un["scores"][c:2 * c], v_new, mode=MODE_STATE)
    for j in range(GDN_GROUP):
        cols = slice(j * hd, (j + 1) * hd)
        o = jnp.concatenate(outs[j], axis=0)
        o = o * lax.rsqrt(jnp.mean(o * o, axis=-1, keepdims=True) + RMS_EPS) * nw
        o_ref[:, cols] = (o * _silu(z_ref[:, cols])).astype(o_ref.dtype)


def _gdn(proj, conv_w, a_log, dt_bias, norm_w, tile=512):
    s = proj.shape[0]
    width = GDN_GROUP * GDN_DIM
    per = LANE // GDN_DIM * GDN_GROUP
    blk = lambda col: pl.BlockSpec((tile, width), lambda h, t: (t, col // per + h))
    cblk = lambda col: pl.BlockSpec((GDN_CONV, width), lambda h, t: (0, col // per + h))
    smem = pl.BlockSpec(memory_space=pltpu.SMEM)
    return pl.pallas_call(
        functools.partial(_gdn_body, tile=tile),
        grid=(GDN_HEADS // GDN_GROUP, s // tile),
        in_specs=[smem, smem,
                  blk(COL_GDN_Q), blk(COL_GDN_K), blk(COL_GDN_V), blk(COL_GDN_Z),
                  pl.BlockSpec((tile, LANE), lambda h, t: (t, COL_GDN_BA)),
                  cblk(0), cblk(GDN_HEADS), cblk(2 * GDN_HEADS),
                  pl.BlockSpec((1, GDN_DIM), lambda h, t: (0, 0))],
        out_specs=pl.BlockSpec((tile, width), lambda h, t: (t, h)),
        out_shape=jax.ShapeDtypeStruct((s, GDN_WIDTH), BF16),
        scratch_shapes=[pltpu.VMEM((3, tile + 8, width), F32),
                        pltpu.VMEM((GDN_GROUP, GDN_DIM, GDN_DIM), F32)],
        compiler_params=_cparams(("parallel", "arbitrary")),
        name="gdn",
    )(a_log, dt_bias, proj, proj, proj, proj, proj, conv_w, conv_w, conv_w, norm_w.reshape(1, GDN_DIM))


def _t5_bucket_table():
    exact = NUM_BUCKETS // 2
    i = np.arange(SWA_BLK)[:, None]
    j = np.arange(2 * SWA_BLK)[None, :]
    steps = np.maximum(i + SWA_BLK - j, 0)
    tables = []
    for _, dilation in SWA_PATTERNS:
        dist = steps * dilation
        d = np.maximum(dist, 1).astype(np.float32)
        ratio = (np.log(d / np.float32(exact)) / np.float32(math.log(MAX_DISTANCE / exact))
                 * np.float32(NUM_BUCKETS - exact)).astype(np.float32)
        log_b = exact + ratio.astype(np.int32)
        tables.append(np.where(dist < exact, dist, np.minimum(log_b, NUM_BUCKETS - 1)))
    return np.stack(tables).astype(np.int32)


def _swa_body(rb_ref, bkt_ref, q_ref, kc_ref, kp_ref, vc_ref, vp_ref, o_ref,
              bias_ref, kbuf, vbuf, o_scr, m_scr, l_scr, *, tile):
    h = pl.program_id(0)
    t = pl.program_id(1)
    blk = SWA_BLK

    @pl.when(t == 0)
    def _bias():
        for p in range(len(SWA_PATTERNS)):
            bkt = bkt_ref[p]
            bias = jnp.zeros((blk, 2 * blk), F32)
            for b in range(NUM_BUCKETS):
                bias = jnp.where(bkt == b, rb_ref[b, h], bias)
            bias_ref[p] = bias

    kbuf[0:tile, :] = kp_ref[...]
    kbuf[tile:2 * tile, :] = kc_ref[...]
    vbuf[0:tile, :] = vp_ref[...]
    vbuf[tile:2 * tile, :] = vc_ref[...]

    qi = lax.broadcasted_iota(jnp.int32, (blk, 2 * blk), 0)
    kj = lax.broadcasted_iota(jnp.int32, (blk, 2 * blk), 1)
    in_band = (kj >= qi) & (kj <= qi + blk)
    scale = SWA_DIM ** -0.5

    band_first = in_band & (kj >= jnp.where(t == 0, blk, 0))
    ones_row = jnp.ones((1, SWA_DIM), F32)

    for p, (window, dil) in enumerate(SWA_PATTERNS):
        span = blk * dil
        bias = bias_ref[p]
        blocks = [(res + n * span, n) for n in range(tile // span) for res in range(dil)]
        for g0 in range(0, len(blocks), SWA_GROUP):
            group = blocks[g0:g0 + SWA_GROUP]
            scores = []
            for start, n in group:
                qb = q_ref[pl.ds(start, blk, stride=dil), :] * scale
                kw = kbuf[pl.ds(tile + start - span, 2 * blk, stride=dil), :]
                sc = _dot(qb, kw, _NT) + bias
                scores.append(jnp.where(band_first if n == 0 else in_band, sc, NEG_INF))
            probs = []
            for sc in scores:
                m = jnp.max(sc, axis=-1, keepdims=True)
                pe = jnp.exp(sc - m)
                probs.append((pe, m, jnp.sum(pe, axis=-1, keepdims=True)))
            for (start, n), (pe, m, l) in zip(group, probs):
                vw = vbuf[pl.ds(tile + start - span, 2 * blk, stride=dil), :]
                o_scr[p, pl.ds(start, blk, stride=dil), :] = _dot(pe, vw)
                m_scr[p, pl.ds(start, blk, stride=dil), :] = m * ones_row
                l_scr[p, pl.ds(start, blk, stride=dil), :] = l * ones_row

    m_max = jnp.maximum(jnp.maximum(m_scr[0], m_scr[1]), m_scr[2])
    num = jnp.zeros((tile, SWA_DIM), F32)
    den = jnp.zeros((tile, SWA_DIM), F32)
    for p in range(len(SWA_PATTERNS)):
        sc = jnp.exp(m_scr[p] - m_max)
        num += o_scr[p] * sc
        den += l_scr[p] * sc
    o_ref[...] = (num / den).astype(o_ref.dtype)


def _swa(proj, rel_bias, tile=SWA_TILE):
    s = proj.shape[0]
    n_pat = len(SWA_PATTERNS)
    bkt = jnp.asarray(_t5_bucket_table())
    cur = lambda col: pl.BlockSpec((tile, LANE), lambda h, t: (t, col + h))
    prev = lambda col: pl.BlockSpec((tile, LANE), lambda h, t: (jnp.maximum(t - 1, 0), col + h))
    return pl.pallas_call(
        functools.partial(_swa_body, tile=tile),
        grid=(SWA_HEADS, s // tile),
        in_specs=[pl.BlockSpec(memory_space=pltpu.SMEM),
                  pl.BlockSpec((n_pat, SWA_BLK, 2 * SWA_BLK), lambda h, t: (0, 0, 0)),
                  cur(COL_SWA_Q), cur(COL_SWA_K), prev(COL_SWA_K), cur(COL_SWA_V), prev(COL_SWA_V)],
        out_specs=pl.BlockSpec((tile, LANE), lambda h, t: (t, h)),
        out_shape=jax.ShapeDtypeStruct((s, SWA_WIDTH), BF16),
        scratch_shapes=[pltpu.VMEM((n_pat, SWA_BLK, 2 * SWA_BLK), F32),
                        pltpu.VMEM((2 * tile, SWA_DIM), F32),
                        pltpu.VMEM((2 * tile, SWA_DIM), F32),
                        pltpu.VMEM((n_pat, tile, SWA_DIM), F32),
                        pltpu.VMEM((n_pat, tile, SWA_DIM), F32),
                        pltpu.VMEM((n_pat, tile, SWA_DIM), F32)],
        compiler_params=_cparams(("parallel", "arbitrary")),
        name="swa",
    )(rel_bias, bkt, proj, proj, proj, proj, proj)


PRM_W0, PRM_A0, PRM_KK, PRM_KA, PRM_RK, PRM_LNW, PRM_LNB, PRM_MUR, PRM_MUK, PRM_MUV = range(10)
PRM_ROWS = 16


def _rwkv_body(r_ref, k_ref, v_ref, wd_ref, ad_ref, gd_ref, prm_ref, mul_ref, mug_ref,
               wup_ref, aup_ref, gup_ref, o_ref, xb, xg, state_ref, *, tile):
    c = RWKV_CHUNK
    nd = RWKV_DIM
    t = pl.program_id(1)

    @pl.when(t == 0)
    def _reset():
        state_ref[...] = jnp.zeros_like(state_ref)
        xb[:, 0:8, :] = jnp.zeros((5, 8, LANE), F32)
        xg[0:8, :] = jnp.zeros((8, GATE_LORA_PAD), F32)

    prm = prm_ref[...]
    row = lambda i: prm[i:i + 1, :]

    def shifted(idx, src, mu):
        xb[idx, 8:8 + tile, :] = src[...]
        cur = xb[idx, 8:8 + tile, :]
        prev = xb[idx, 7:7 + tile, :]
        xb[idx, 0:8, :] = xb[idx, tile:tile + 8, :]
        return cur + mu * (prev - cur)

    r = shifted(0, r_ref, row(PRM_MUR))
    k = shifted(1, k_ref, row(PRM_MUK))
    v = shifted(2, v_ref, row(PRM_MUV))
    wd = shifted(3, wd_ref, mul_ref[0:1, :])
    ad = shifted(4, ad_ref, mul_ref[1:2, :])
    xg[8:8 + tile, :] = gd_ref[...]
    gcur = xg[8:8 + tile, :]
    gprev = xg[7:7 + tile, :]
    xg[0:8, :] = xg[tile:tile + 8, :]
    gd = gcur + mug_ref[...] * (gprev - gcur)

    w_log = -_softplus(-(row(PRM_W0) + _dot(jnp.tanh(wd), wup_ref[...]))) - 0.5
    log_w = -jnp.exp(w_log)
    a_gate = _sigmoid(row(PRM_A0) + _dot(ad, aup_ref[...]))
    g_gate = _dot(_sigmoid(gd), gup_ref[...])
    kx = k * row(PRM_KK)
    k_mod = k * (1.0 + (a_gate - 1.0) * row(PRM_KA))

    ri = lax.broadcasted_iota(jnp.int32, (c, c), 0)
    ci = lax.broadcasted_iota(jnp.int32, (c, c), 1)
    lower = jnp.where(ci <= ri, 1.0, 0.0).astype(F32)
    cum = jnp.concatenate([_dot(lower, log_w[n * c:(n + 1) * c], mode=MODE_CUMSUM)
                           for n in range(tile // c)], axis=0)

    r2 = lax.broadcasted_iota(jnp.int32, (2 * c, 2 * c), 0)
    c2 = lax.broadcasted_iota(jnp.int32, (2 * c, 2 * c), 1)
    col_in = jnp.where(c2 < c, c2, c2 - c)
    band = col_in <= jnp.where(r2 < c, r2 - 1, r2 - c)

    n_heads = LANE // nd
    n_chunks = tile // c
    lane = lax.broadcasted_iota(jnp.int32, (1, LANE), 1)
    head_masks = [(lane // nd) == hh for hh in range(n_heads)]

    def head_sum(x):
        out = jnp.zeros_like(x)
        for msk in head_masks:
            out = jnp.where(msk, jnp.sum(jnp.where(msk, x, 0.0), axis=-1, keepdims=True), out)
        return out

    kk = kx * lax.rsqrt(head_sum(kx * kx) + 1e-12)
    b_vec = kk * a_gate
    c_last = jnp.concatenate(
        [jnp.broadcast_to(cum[n * c + c - 1:n * c + c, :], (c, LANE)) for n in range(n_chunks)], axis=0)
    p_out = jnp.exp(-cum)
    a_t = -kk * jnp.exp(cum - log_w)
    r_t = r * jnp.exp(cum)
    k_t = k_mod * p_out
    b_t = b_vec * p_out
    decay_end = jnp.exp(c_last - cum)
    k_e = k_mod * decay_end
    b_e = b_vec * decay_end
    decay_all = jnp.exp(c_last)

    units = []
    for hh in range(n_heads):
        lanes = slice(hh * nd, (hh + 1) * nd)
        for n in range(n_chunks):
            rows = slice(n * c, (n + 1) * c)
            units.append(dict(hh=hh, n=n, a=a_t[rows, lanes], r=r_t[rows, lanes], v=v[rows, lanes],
                              ar=jnp.concatenate([a_t[rows, lanes], r_t[rows, lanes]], axis=0),
                              bk=jnp.concatenate([b_t[rows, lanes], k_t[rows, lanes]], axis=0),
                              bk_e=jnp.concatenate([b_e[rows, lanes], k_e[rows, lanes]], axis=0),
                              decay=decay_all[n * c:n * c + 1, lanes]))
    for un in units:
        un["mm"] = jnp.where(band, _dot(un["ar"], un["bk"], _NT), 0.0)
    t_invs = _unit_lower_inverses([un["mm"][0:c, 0:c] for un in units], c)
    for un in units:
        un["mv"] = _dot(un["mm"][0:c, c:2 * c], un["v"], mode=MODE_SOLVE)
    for un, t_inv in zip(units, t_invs):
        un["wu"] = _dot(t_inv, jnp.concatenate([un["a"], un["mv"]], axis=1), mode=MODE_SOLVE)

    for un in units:
        un["trans"] = _dot(un["wu"][:, 0:nd], un["bk_e"][0:c], _TN, mode=MODE_STATE)
        un["const"] = _dot(jnp.concatenate([un["wu"][:, nd:2 * nd], un["v"]], axis=0), un["bk_e"], _TN,
                           mode=MODE_STATE)
    states = [state_ref[hh] for hh in range(n_heads)]
    for n in range(n_chunks):
        for un in units:
            if un["n"] != n:
                continue
            hh = un["hh"]
            un["s0"] = states[hh]
            states[hh] = (states[hh] * un["decay"] + _dot(states[hh], un["trans"], mode=MODE_STATE)
                          + un["const"])
    for hh in range(n_heads):
        state_ref[hh] = states[hh]
    ys = [[None] * n_chunks for _ in range(n_heads)]
    for un in units:
        un["sr"] = _dot(jnp.concatenate([un["wu"][:, 0:nd], un["r"]], axis=0), un["s0"], _NT,
                        mode=MODE_STATE)
    for un in units:
        u = un["sr"][0:c] + un["wu"][:, nd:2 * nd]
        uv = jnp.concatenate([u, un["v"]], axis=0)
        ys[un["hh"]][un["n"]] = un["sr"][c:2 * c] + _dot(un["mm"][c:2 * c], uv, mode=MODE_STATE)
    y = jnp.concatenate([jnp.concatenate(ys[hh], axis=0) for hh in range(n_heads)], axis=1)
    mean = head_sum(y) * (1.0 / nd)
    var = head_sum(jnp.square(y - mean)) * (1.0 / nd)
    y_n = (y - mean) * lax.rsqrt(var + GN_EPS) * row(PRM_LNW) + row(PRM_LNB)
    bonus = head_sum(r * k_mod * row(PRM_RK)) * v
    o_ref[...] = ((y_n + bonus) * g_gate).astype(o_ref.dtype)


def _rwkv(proj, prm, mu_lora, mu_gate, w_up, a_up, g_up, tile=512):
    s = proj.shape[0]
    pairs = RWKV_WIDTH // LANE
    blk = lambda col: pl.BlockSpec((tile, LANE), lambda h, t: (t, col + h))
    fixed = lambda col: pl.BlockSpec((tile, LANE), lambda h, t: (t, col))
    return pl.pallas_call(
        functools.partial(_rwkv_body, tile=tile),
        grid=(pairs, s // tile),
        in_specs=[blk(COL_R), blk(COL_K), blk(COL_V), fixed(COL_WD), fixed(COL_AD),
                  pl.BlockSpec((tile, GATE_LORA_PAD), lambda h, t: (t, COL_GD * LANE // GATE_LORA_PAD)),
                  pl.BlockSpec((PRM_ROWS, LANE), lambda h, t: (0, h)),
                  pl.BlockSpec((8, LANE), lambda h, t: (0, 0)),
                  pl.BlockSpec((1, GATE_LORA_PAD), lambda h, t: (0, 0)),
                  pl.BlockSpec((DECAY_LORA, LANE), lambda h, t: (0, h)),
                  pl.BlockSpec((ICL_LORA, LANE), lambda h, t: (0, h)),
                  pl.BlockSpec((GATE_LORA_PAD, LANE), lambda h, t: (0, h))],
        out_specs=pl.BlockSpec((tile, LANE), lambda h, t: (t, h)),
        out_shape=jax.ShapeDtypeStruct((s, RWKV_WIDTH), BF16),
        scratch_shapes=[pltpu.VMEM((5, tile + 8, LANE), F32),
                        pltpu.VMEM((tile + 8, GATE_LORA_PAD), F32),
                        pltpu.VMEM((LANE // RWKV_DIM, RWKV_DIM, RWKV_DIM), F32)],
        compiler_params=_cparams(("parallel", "arbitrary")),
        name="rwkv7",
    )(proj, proj, proj, proj, proj, proj, prm, mu_lora, mu_gate, w_up, a_up, g_up)


def _layout_w_in(w):
    d = w.shape[0]
    gw, sw, rw = GDN_WIDTH, SWA_WIDTH, RWKV_WIDTH
    o = 0
    gdn_main = w[:, o:o + 4 * gw]; o += 4 * gw
    gdn_ba = w[:, o:o + 2 * GDN_HEADS]; o += 2 * GDN_HEADS
    swa = w[:, o:o + 3 * sw]; o += 3 * sw
    rkv = w[:, o:o + 3 * rw]; o += 3 * rw
    wd = w[:, o:o + DECAY_LORA]; o += DECAY_LORA
    ad = w[:, o:o + ICL_LORA]; o += ICL_LORA
    gd = w[:, o:o + GATE_LORA]
    z = lambda n: jnp.zeros((d, n), w.dtype)
    out = jnp.concatenate([gdn_main, gdn_ba, z(LANE - 2 * GDN_HEADS), swa, wd, ad, z(LANE),
                           gd, z(GATE_LORA_PAD - GATE_LORA), rkv], axis=1)
    assert out.shape[1] == PROJ_WIDTH
    return out.astype(BF16)


def _layout_rwkv_params(mu, w0, a0, k_k, k_a, r_k, ln_w, ln_b):
    rw = RWKV_WIDTH
    mu_r, mu_k, mu_v = mu[0:rw], mu[rw:2 * rw], mu[2 * rw:3 * rw]
    o = 3 * rw
    mu_wd = mu[o:o + DECAY_LORA]; o += DECAY_LORA
    mu_ad = mu[o:o + ICL_LORA]; o += ICL_LORA
    mu_gd = mu[o:o + GATE_LORA]
    rows = [w0, a0, k_k, k_a, r_k.reshape(rw), ln_w, ln_b, mu_r, mu_k, mu_v]
    prm = jnp.stack(rows + [jnp.zeros((rw,), F32)] * (PRM_ROWS - len(rows))).astype(F32)
    mu_lora = jnp.stack([mu_wd, mu_ad] + [jnp.zeros((LANE,), F32)] * 6).astype(F32)
    mu_gate = jnp.concatenate([mu_gd, jnp.zeros((GATE_LORA_PAD - GATE_LORA,), F32)]).reshape(1, GATE_LORA_PAD)
    return prm, mu_lora, mu_gate


def kernel(x, attn_norm, w_in, gdn_conv, gdn_a_log, gdn_dt_bias, gdn_norm, rwkv_mu, rwkv_w0, rwkv_w_up, rwkv_a0, rwkv_a_up, rwkv_g_up, rwkv_k_k, rwkv_k_a, rwkv_r_k, rwkv_ln_w, rwkv_ln_b, w_out, ffn_norm, w_ffn_gate, w_ffn_up, ffn_conv, ffn_conv_b, w_ffn_down, rel_bias, final_norm):
    batch, seq, d = x.shape
    depth = w_in.shape[0]
    outs = []
    rows_in = [x.reshape(seq, d)] if batch == 1 else [x[b] for b in range(batch)]
    for xb in rows_in:
        for l in range(depth):
            h = _rmsnorm(xb, attn_norm[l], BF16)
            proj = _in_proj(h, _layout_w_in(w_in[l]))
            o_a = _gdn(proj, gdn_conv[l], gdn_a_log[l], gdn_dt_bias[l], gdn_norm[l])
            o_b = _swa(proj, rel_bias)
            prm, mu_lora, mu_gate = _layout_rwkv_params(
                rwkv_mu[l], rwkv_w0[l], rwkv_a0[l], rwkv_k_k[l], rwkv_k_a[l], rwkv_r_k[l],
                rwkv_ln_w[l], rwkv_ln_b[l])
            g_up = jnp.concatenate(
                [rwkv_g_up[l], jnp.zeros((GATE_LORA_PAD - GATE_LORA, RWKV_WIDTH), F32)], axis=0)
            o_c = _rwkv(proj, prm, mu_lora, mu_gate, rwkv_w_up[l].astype(BF16),
                        rwkv_a_up[l].astype(BF16), g_up.astype(BF16))
            wo = _to_bf16(w_out, l)
            xb = _out_proj(xb, o_a, o_b, o_c, wo[0:GDN_WIDTH], wo[GDN_WIDTH:GDN_WIDTH + SWA_WIDTH],
                           wo[GDN_WIDTH + SWA_WIDTH:])
            h = _rmsnorm(xb, ffn_norm[l], BF16)
            act = _ffn_up(h, _to_bf16(w_ffn_gate, l), _to_bf16(w_ffn_up, l), ffn_conv[l], ffn_conv_b[l])
            xb = _ffn_down(xb, act, _to_bf16(w_ffn_down, l))
        outs.append(_rmsnorm(xb, final_norm, x.dtype))
    return outs[0].reshape(1, seq, d) if batch == 1 else jnp.stack(outs)
```

```python
import functools
import math

import numpy as np
import jax
import jax.numpy as jnp
from jax import lax
from jax.experimental import pallas as pl
from jax.experimental.pallas import tpu as pltpu

F32 = jnp.float32
BF16 = jnp.bfloat16
HIGHEST = lax.Precision.HIGHEST

LANE = 128
D_MODEL = 4096
RMS_EPS = 1e-6
GN_EPS = 64e-5
NEG_INF = -1e30
VMEM_LIMIT = 56 * 1024 * 1024

GDN_DIM = 128
GDN_HEADS = 12
GDN_WIDTH = GDN_HEADS * GDN_DIM
GDN_CONV = 4
GDN_CHUNK = 64
GDN_GROUP = 2
SWA_DIM = 128
SWA_HEADS = 8
SWA_WIDTH = SWA_HEADS * SWA_DIM
SWA_PATTERNS = ((128, 1), (512, 4), (2048, 16))
SWA_BLK = 128
SWA_TILE = 2048
SWA_GROUP = 4
NUM_BUCKETS = 32
MAX_DISTANCE = 2048
RWKV_DIM = 64
RWKV_HEADS = 24
RWKV_WIDTH = RWKV_HEADS * RWKV_DIM
RWKV_CHUNK = 64
DECAY_LORA = 128
ICL_LORA = 128
GATE_LORA = 480
GATE_LORA_PAD = 512
D_FF = 11008
FFN_CONV = 3

COL_GDN_Q, COL_GDN_K, COL_GDN_V, COL_GDN_Z, COL_GDN_BA = 0, 12, 24, 36, 48
COL_SWA_Q, COL_SWA_K, COL_SWA_V = 49, 57, 65
COL_WD, COL_AD = 73, 74
COL_GD = 76
COL_R, COL_K, COL_V = 80, 92, 104
PROJ_BLOCKS = 116
PROJ_WIDTH = PROJ_BLOCKS * LANE


def _cparams(sem):
    return pltpu.CompilerParams(dimension_semantics=sem, vmem_limit_bytes=VMEM_LIMIT)


def _silu(x):
    return x * (1.0 / (1.0 + jnp.exp(-x)))


def _sigmoid(x):
    return 1.0 / (1.0 + jnp.exp(-x))


def _softplus(x):
    return jnp.maximum(x, 0.0) + jnp.log(1.0 + jnp.exp(-jnp.abs(x)))


def _split_bf16(x):
    hi = x.astype(BF16)
    return hi, (x - hi.astype(F32)).astype(BF16)


def _dot(a, b, dims=(((1,), (0,)), ((), ())), mode="bf16"):
    if mode == "f32":
        return lax.dot_general(a.astype(F32), b.astype(F32), dims, precision=HIGHEST,
                               preferred_element_type=F32)
    if mode == "x3":
        a_hi, a_lo = _split_bf16(a)
        b_hi, b_lo = _split_bf16(b)
        mm = lambda p, q: lax.dot_general(p, q, dims, preferred_element_type=F32)
        return mm(a_hi, b_hi) + (mm(a_hi, b_lo) + mm(a_lo, b_hi))
    return lax.dot_general(a.astype(BF16), b.astype(BF16), dims, preferred_element_type=F32)


MODE_CUMSUM = "f32"
MODE_INV = "bf16"
MODE_SOLVE = "bf16"
MODE_STATE = "bf16"


_NT = (((1,), (1,)), ((), ()))
_TN = (((0,), (0,)), ((), ()))


def _rmsnorm_body(x_ref, w_ref, o_ref):
    x = x_ref[...]
    ms = jnp.mean(x * x, axis=-1, keepdims=True)
    o_ref[...] = (x * lax.rsqrt(ms + RMS_EPS) * w_ref[...]).astype(o_ref.dtype)


def _rmsnorm(x, w, out_dtype, tm=512):
    s, d = x.shape
    return pl.pallas_call(
        _rmsnorm_body,
        grid=(s // tm,),
        in_specs=[pl.BlockSpec((tm, d), lambda i: (i, 0)),
                  pl.BlockSpec((1, d), lambda i: (0, 0))],
        out_specs=pl.BlockSpec((tm, d), lambda i: (i, 0)),
        out_shape=jax.ShapeDtypeStruct((s, d), out_dtype),
        compiler_params=_cparams(("parallel",)),
        name="rmsnorm",
    )(x, w.reshape(1, d))


def _cast_body(w_ref, o_ref):
    o_ref[...] = w_ref[...].astype(o_ref.dtype)


def _to_bf16(w, layer, rows=256):
    _, r, c = w.shape
    return pl.pallas_call(
        _cast_body,
        grid=(r // rows,),
        in_specs=[pl.BlockSpec((None, rows, c), lambda i: (layer, i, 0))],
        out_specs=pl.BlockSpec((rows, c), lambda i: (i, 0)),
        out_shape=jax.ShapeDtypeStruct((r, c), BF16),
        compiler_params=_cparams(("parallel",)),
        name="to_bf16",
    )(w)


def _w_in_segments():
    gw, sw, rw = GDN_WIDTH, SWA_WIDTH, RWKV_WIDTH
    src = 0
    segs = []
    for width, dst in ((4 * gw, COL_GDN_Q), (2 * GDN_HEADS, COL_GDN_BA), (3 * sw, COL_SWA_Q),
                       (3 * rw, COL_R), (DECAY_LORA, COL_WD), (ICL_LORA, COL_AD), (GATE_LORA, COL_GD)):
        segs.append((src, width, dst * LANE))
        src += width
    return segs, src


def _layout_w_in_body(w_ref, o_ref):
    o_ref[...] = jnp.zeros_like(o_ref)
    for src, width, dst in _w_in_segments()[0]:
        o_ref[:, dst:dst + width] = w_ref[:, src:src + width].astype(o_ref.dtype)


def _layout_w_in_pallas(w, layer, rows=32):
    _, r, c = w.shape
    assert c == _w_in_segments()[1]
    return pl.pallas_call(
        _layout_w_in_body,
        grid=(r // rows,),
        in_specs=[pl.BlockSpec((None, rows, c), lambda i: (layer, i, 0))],
        out_specs=pl.BlockSpec((rows, PROJ_WIDTH), lambda i: (i, 0)),
        out_shape=jax.ShapeDtypeStruct((r, PROJ_WIDTH), BF16),
        compiler_params=_cparams(("parallel",)),
        name="layout_w_in",
    )(w)


def _matmul_body(a_ref, b_ref, o_ref):
    o_ref[...] = jnp.dot(a_ref[...], b_ref[...], preferred_element_type=F32).astype(o_ref.dtype)


def _in_proj(h, w, tm=1024, tn=512):
    s, k = h.shape
    n = w.shape[1]
    return pl.pallas_call(
        _matmul_body,
        grid=(s // tm, n // tn),
        in_specs=[pl.BlockSpec((tm, k), lambda i, j: (i, 0)),
                  pl.BlockSpec((k, tn), lambda i, j: (0, j))],
        out_specs=pl.BlockSpec((tm, tn), lambda i, j: (i, j)),
        out_shape=jax.ShapeDtypeStruct((s, n), F32),
        compiler_params=_cparams(("parallel", "arbitrary")),
        name="in_proj",
    )(h, w)


def _out_proj_body(x_ref, a_ref, b_ref, c_ref, wa_ref, wb_ref, wc_ref, o_ref):
    acc = jnp.dot(a_ref[...], wa_ref[...], preferred_element_type=F32)
    acc += jnp.dot(b_ref[...], wb_ref[...], preferred_element_type=F32)
    acc += jnp.dot(c_ref[...], wc_ref[...], preferred_element_type=F32)
    o_ref[...] = x_ref[...] + acc


def _out_proj(x, oa, ob, oc, wa, wb, wc, tm=1024, tn=512):
    s, d = x.shape
    ka, kb, kc = oa.shape[1], ob.shape[1], oc.shape[1]
    return pl.pallas_call(
        _out_proj_body,
        grid=(s // tm, d // tn),
        in_specs=[pl.BlockSpec((tm, tn), lambda i, j: (i, j)),
                  pl.BlockSpec((tm, ka), lambda i, j: (i, 0)),
                  pl.BlockSpec((tm, kb), lambda i, j: (i, 0)),
                  pl.BlockSpec((tm, kc), lambda i, j: (i, 0)),
                  pl.BlockSpec((ka, tn), lambda i, j: (0, j)),
                  pl.BlockSpec((kb, tn), lambda i, j: (0, j)),
                  pl.BlockSpec((kc, tn), lambda i, j: (0, j))],
        out_specs=pl.BlockSpec((tm, tn), lambda i, j: (i, j)),
        out_shape=jax.ShapeDtypeStruct((s, d), F32),
        compiler_params=_cparams(("parallel", "arbitrary")),
        name="out_proj",
    )(x, oa, ob, oc, wa, wb, wc)


FFN_HALO = 16


def _ffn_up_body(h_ref, halo_ref, wg_ref, wu_ref, cw_ref, cb_ref, o_ref, hbuf, *, tm):
    i = pl.program_id(0)

    @pl.when(pl.program_id(1) == 0)
    def _stage():
        halo = halo_ref[...]
        hbuf[0:FFN_HALO, :] = jnp.where(i == 0, jnp.zeros_like(halo), halo)
        hbuf[FFN_HALO:FFN_HALO + tm, :] = h_ref[...]

    g = jnp.dot(hbuf[...], wg_ref[...], preferred_element_type=F32)
    u = jnp.dot(hbuf[FFN_HALO:FFN_HALO + tm, :], wu_ref[...], preferred_element_type=F32)
    cw = cw_ref[...]
    conv = (cw[0:1, :] * pltpu.roll(g, 2, 0)[FFN_HALO:, :]
            + cw[1:2, :] * pltpu.roll(g, 1, 0)[FFN_HALO:, :]
            + cw[2:3, :] * g[FFN_HALO:, :]) + cb_ref[...]
    o_ref[...] = (_silu(conv) * u).astype(o_ref.dtype)


def _ffn_up(h, wg, wu, cw, cb, tm=1024, tn=256):
    s, k = h.shape
    f = wg.shape[1]
    halo_blocks = tm // FFN_HALO
    return pl.pallas_call(
        functools.partial(_ffn_up_body, tm=tm),
        grid=(s // tm, f // tn),
        in_specs=[pl.BlockSpec((tm, k), lambda i, j: (i, 0)),
                  pl.BlockSpec((FFN_HALO, k), lambda i, j: (jnp.maximum(i * halo_blocks - 1, 0), 0)),
                  pl.BlockSpec((k, tn), lambda i, j: (0, j)),
                  pl.BlockSpec((k, tn), lambda i, j: (0, j)),
                  pl.BlockSpec((FFN_CONV, tn), lambda i, j: (0, j)),
                  pl.BlockSpec((1, tn), lambda i, j: (0, j))],
        out_specs=pl.BlockSpec((tm, tn), lambda i, j: (i, j)),
        out_shape=jax.ShapeDtypeStruct((s, f), BF16),
        scratch_shapes=[pltpu.VMEM((tm + FFN_HALO, k), BF16)],
        compiler_params=_cparams(("parallel", "arbitrary")),
        name="ffn_up",
    )(h, h, wg, wu, cw, cb.reshape(1, f))


def _ffn_down_body(x_ref, a_ref, w_ref, o_ref, acc_ref):
    kk = pl.program_id(2)

    @pl.when(kk == 0)
    def _init():
        acc_ref[...] = x_ref[...]

    acc_ref[...] += jnp.dot(a_ref[...], w_ref[...], preferred_element_type=F32)

    @pl.when(kk == pl.num_programs(2) - 1)
    def _done():
        o_ref[...] = acc_ref[...]


def _ffn_down(x, act, w, tm=1024, tn=512, ksplit=2):
    s, d = x.shape
    f = act.shape[1]
    tk = f // ksplit
    return pl.pallas_call(
        _ffn_down_body,
        grid=(s // tm, d // tn, ksplit),
        in_specs=[pl.BlockSpec((tm, tn), lambda i, j, kk: (i, j)),
                  pl.BlockSpec((tm, tk), lambda i, j, kk: (i, kk)),
                  pl.BlockSpec((tk, tn), lambda i, j, kk: (kk, j))],
        out_specs=pl.BlockSpec((tm, tn), lambda i, j, kk: (i, j)),
        out_shape=jax.ShapeDtypeStruct((s, d), F32),
        scratch_shapes=[pltpu.VMEM((tm, tn), F32)],
        compiler_params=_cparams(("parallel", "arbitrary", "arbitrary")),
        name="ffn_down",
    )(x, act, w)


def _unit_lower_inverses(n_mats, size):
    row = lax.broadcasted_iota(jnp.int32, (size, size), 0)
    col = lax.broadcasted_iota(jnp.int32, (size, size), 1)
    eye = jnp.where(row == col, 1.0, 0.0).astype(F32)
    invs = [eye + n for n in n_mats]
    powers = [_dot(n, n, mode=MODE_INV) for n in n_mats]
    steps = int(math.log2(size)) - 1
    for s in range(steps):
        last = s == steps - 1
        for i, (inv, p) in enumerate(zip(invs, powers)):
            if last:
                invs[i] = inv + _dot(inv, p, mode=MODE_INV)
            else:
                both = _dot(jnp.concatenate([inv, p], axis=0), p, mode=MODE_INV)
                invs[i] = inv + both[0:size]
                powers[i] = both[size:2 * size]
    return invs


def _gdn_body(alog_ref, dtb_ref, q_ref, k_ref, v_ref, z_ref, ba_ref, cq_ref, ck_ref, cv_ref,
              nw_ref, o_ref, xbuf, state_ref, *, tile):
    c = GDN_CHUNK
    hd = GDN_DIM
    width = GDN_GROUP * hd
    n_chunks = tile // c
    t = pl.program_id(1)

    @pl.when(t == 0)
    def _reset():
        state_ref[...] = jnp.zeros_like(state_ref)
        xbuf[:, 0:8, :] = jnp.zeros((3, 8, width), F32)

    convs = []
    for idx, (src, cw_ref) in enumerate(((q_ref, cq_ref), (k_ref, ck_ref), (v_ref, cv_ref))):
        xbuf[idx, 8:8 + tile, :] = src[...]
        cw = cw_ref[...]
        acc = cw[0:1, :] * xbuf[idx, 5:5 + tile, :]
        acc += cw[1:2, :] * xbuf[idx, 6:6 + tile, :]
        acc += cw[2:3, :] * xbuf[idx, 7:7 + tile, :]
        acc += cw[3:4, :] * xbuf[idx, 8:8 + tile, :]
        convs.append(_silu(acc))
        xbuf[idx, 0:8, :] = xbuf[idx, tile:tile + 8, :]
    q_all, k_all, v_all = convs

    ba = ba_ref[...]
    lane = lax.broadcasted_iota(jnp.int32, (1, LANE), 1)
    ri = lax.broadcasted_iota(jnp.int32, (c, c), 0)
    ci = lax.broadcasted_iota(jnp.int32, (c, c), 1)
    lower = jnp.where(ci <= ri, 1.0, 0.0).astype(F32)
    upper = jnp.where(ri <= ci, 1.0, 0.0).astype(F32)
    r2 = lax.broadcasted_iota(jnp.int32, (2 * c, c), 0)
    c2 = lax.broadcasted_iota(jnp.int32, (2 * c, c), 1)
    band = c2 <= jnp.where(r2 < c, r2 - 1, r2 - c)
    nw = nw_ref[...]

    units = []
    for j in range(GDN_GROUP):
        h = pl.program_id(0) * GDN_GROUP + j
        cols = slice(j * hd, (j + 1) * hd)
        q, k, v = q_all[:, cols], k_all[:, cols], v_all[:, cols]
        q = q * lax.rsqrt(jnp.sum(q * q, axis=-1, keepdims=True) + 1e-6) * (hd ** -0.5)
        k = k * lax.rsqrt(jnp.sum(k * k, axis=-1, keepdims=True) + 1e-6)
        b_col = jnp.sum(jnp.where(lane == h, ba, 0.0), axis=-1, keepdims=True)
        a_col = jnp.sum(jnp.where(lane == h + GDN_HEADS, ba, 0.0), axis=-1, keepdims=True)
        beta = _sigmoid(b_col)
        g = -jnp.exp(alog_ref[h]) * _softplus(a_col + dtb_ref[h])
        g_b = g * jnp.ones((1, hd), F32)
        g_col = jnp.concatenate([_dot(lower, g_b[n * c:(n + 1) * c], mode=MODE_CUMSUM)
                                 for n in range(n_chunks)], axis=0)
        g_row = jnp.concatenate([_dot(jnp.ones((8, c), F32), g_b[n * c:(n + 1) * c, 0:c] * upper,
                                      mode=MODE_CUMSUM) for n in range(n_chunks)], axis=1)
        exp_g = jnp.exp(g_col)
        kb = k * beta
        vb = v * beta
        kbe = kb * exp_g
        qg = q * exp_g
        for n in range(n_chunks):
            rows = slice(n * c, (n + 1) * c)
            g_last = g_col[n * c + c - 1:n * c + c, :]
            gamma = jnp.exp(jnp.minimum(g_col[rows, 0:c] - g_row[0:1, n * c:(n + 1) * c], 0.0))
            units.append(dict(j=j, n=n, k=k[rows], kq=jnp.concatenate([kb[rows], q[rows]], axis=0),
                              rhs=jnp.concatenate([vb[rows], kbe[rows]], axis=1), qg=qg[rows],
                              kd=k[rows] * jnp.exp(g_last - g_col[rows]), decay=jnp.exp(g_last),
                              gamma2=jnp.concatenate([gamma, gamma], axis=0)))
    for un in units:
        un["scores"] = jnp.where(band, _dot(un["kq"], un["k"], _NT) * un["gamma2"], 0.0)
    t_invs = _unit_lower_inverses([-un["scores"][0:c] for un in units], c)
    for un, t_inv in zip(units, t_invs):
        un["uw"] = _dot(t_inv, un["rhs"], mode=MODE_SOLVE)

    for un in units:
        un["trans"] = _dot(un["kd"], un["uw"][:, hd:2 * hd], _TN, mode=MODE_STATE)
        un["const"] = _dot(un["kd"], un["uw"][:, 0:hd], _TN, mode=MODE_STATE)
    states = [state_ref[j] for j in range(GDN_GROUP)]
    for n in range(n_chunks):
        for un in units:
            if un["n"] != n:
                continue
            j = un["j"]
            un["s0"] = states[j]
            states[j] = (states[j] * un["decay"] - _dot(un["trans"], states[j], mode=MODE_STATE)
                         + un["const"])
    for j in range(GDN_GROUP):
        state_ref[j] = states[j]
    outs = [[None] * n_chunks for _ in range(GDN_GROUP)]
    for un in units:
        un["ws"] = _dot(jnp.concatenate([un["uw"][:, hd:2 * hd], un["qg"]], axis=0), un["s0"],
                        mode=MODE_STATE)
    for un in units:
        v_new = un["uw"][:, 0:hd] - un["ws"][0:c]
        outs[un["j"]][un["n"]] = un["ws"][c:2 * c] + _dot(un["scores"][c:2 * c], v_new, mode=MODE_STATE)
    for j in range(GDN_GROUP):
        cols = slice(j * hd, (j + 1) * hd)
        o = jnp.concatenate(outs[j], axis=0)
        o = o * lax.rsqrt(jnp.mean(o * o, axis=-1, keepdims=True) + RMS_EPS) * nw
        o_ref[:, cols] = (o * _silu(z_ref[:, cols])).astype(o_ref.dtype)


def _gdn(proj, conv_w, a_log, dt_bias, norm_w, tile=512):
    s = proj.shape[0]
    width = GDN_GROUP * GDN_DIM
    per = LANE // GDN_DIM * GDN_GROUP
    blk = lambda col: pl.BlockSpec((tile, width), lambda h, t: (t, col // per + h))
    cblk = lambda col: pl.BlockSpec((GDN_CONV, width), lambda h, t: (0, col // per + h))
    smem = pl.BlockSpec(memory_space=pltpu.SMEM)
    return pl.pallas_call(
        functools.partial(_gdn_body, tile=tile),
        grid=(GDN_HEADS // GDN_GROUP, s // tile),
        in_specs=[smem, smem,
                  blk(COL_GDN_Q), blk(COL_GDN_K), blk(COL_GDN_V), blk(COL_GDN_Z),
                  pl.BlockSpec((tile, LANE), lambda h, t: (t, COL_GDN_BA)),
                  cblk(0), cblk(GDN_HEADS), cblk(2 * GDN_HEADS),
                  pl.BlockSpec((1, GDN_DIM), lambda h, t: (0, 0))],
        out_specs=pl.BlockSpec((tile, width), lambda h, t: (t, h)),
        out_shape=jax.ShapeDtypeStruct((s, GDN_WIDTH), BF16),
        scratch_shapes=[pltpu.VMEM((3, tile + 8, width), F32),
                        pltpu.VMEM((GDN_GROUP, GDN_DIM, GDN_DIM), F32)],
        compiler_params=_cparams(("parallel", "arbitrary")),
        name="gdn",
    )(a_log, dt_bias, proj, proj, proj, proj, proj, conv_w, conv_w, conv_w, norm_w.reshape(1, GDN_DIM))


def _t5_bucket_table():
    exact = NUM_BUCKETS // 2
    i = np.arange(SWA_BLK)[:, None]
    j = np.arange(2 * SWA_BLK)[None, :]
    steps = np.maximum(i + SWA_BLK - j, 0)
    tables = []
    for _, dilation in SWA_PATTERNS:
        dist = steps * dilation
        d = np.maximum(dist, 1).astype(np.float32)
        ratio = (np.log(d / np.float32(exact)) / np.float32(math.log(MAX_DISTANCE / exact))
                 * np.float32(NUM_BUCKETS - exact)).astype(np.float32)
        log_b = exact + ratio.astype(np.int32)
        tables.append(np.where(dist < exact, dist, np.minimum(log_b, NUM_BUCKETS - 1)))
    return np.stack(tables).astype(np.int32)


def _swa_body(rb_ref, bkt_ref, q_ref, kc_ref, kp_ref, vc_ref, vp_ref, o_ref,
              bias_ref, kbuf, vbuf, o_scr, m_scr, l_scr, *, tile):
    h = pl.program_id(0)
    t = pl.program_id(1)
    blk = SWA_BLK

    @pl.when(t == 0)
    def _bias():
        for p in range(len(SWA_PATTERNS)):
            bkt = bkt_ref[p]
            bias = jnp.zeros((blk, 2 * blk), F32)
            for b in range(NUM_BUCKETS):
                bias = jnp.where(bkt == b, rb_ref[b, h], bias)
            bias_ref[p] = bias

    kbuf[0:tile, :] = kp_ref[...]
    kbuf[tile:2 * tile, :] = kc_ref[...]
    vbuf[0:tile, :] = vp_ref[...]
    vbuf[tile:2 * tile, :] = vc_ref[...]

    qi = lax.broadcasted_iota(jnp.int32, (blk, 2 * blk), 0)
    kj = lax.broadcasted_iota(jnp.int32, (blk, 2 * blk), 1)
    in_band = (kj >= qi) & (kj <= qi + blk)
    scale = SWA_DIM ** -0.5

    band_first = in_band & (kj >= jnp.where(t == 0, blk, 0))
    ones_row = jnp.ones((1, SWA_DIM), F32)

    for p, (window, dil) in enumerate(SWA_PATTERNS):
        span = blk * dil
        bias = bias_ref[p]
        blocks = [(res + n * span, n) for n in range(tile // span) for res in range(dil)]
        for g0 in range(0, len(blocks), SWA_GROUP):
            group = blocks[g0:g0 + SWA_GROUP]
            scores = []
            for start, n in group:
                qb = q_ref[pl.ds(start, blk, stride=dil), :] * scale
                kw = kbuf[pl.ds(tile + start - span, 2 * blk, stride=dil), :]
                sc = _dot(qb, kw, _NT) + bias
                scores.append(jnp.where(band_first if n == 0 else in_band, sc, NEG_INF))
            probs = []
            for sc in scores:
                m = jnp.max(sc, axis=-1, keepdims=True)
                pe = jnp.exp(sc - m)
                probs.append((pe, m, jnp.sum(pe, axis=-1, keepdims=True)))
            for (start, n), (pe, m, l) in zip(group, probs):
                vw = vbuf[pl.ds(tile + start - span, 2 * blk, stride=dil), :]
                o_scr[p, pl.ds(start, blk, stride=dil), :] = _dot(pe, vw)
                m_scr[p, pl.ds(start, blk, stride=dil), :] = m * ones_row
                l_scr[p, pl.ds(start, blk, stride=dil), :] = l * ones_row

    m_max = jnp.maximum(jnp.maximum(m_scr[0], m_scr[1]), m_scr[2])
    num = jnp.zeros((tile, SWA_DIM), F32)
    den = jnp.zeros((tile, SWA_DIM), F32)
    for p in range(len(SWA_PATTERNS)):
        sc = jnp.exp(m_scr[p] - m_max)
        num += o_scr[p] * sc
        den += l_scr[p] * sc
    o_ref[...] = (num / den).astype(o_ref.dtype)


def _swa(proj, rel_bias, tile=SWA_TILE):
    s = proj.shape[0]
    n_pat = len(SWA_PATTERNS)
    bkt = jnp.asarray(_t5_bucket_table())
    cur = lambda col: pl.BlockSpec((tile, LANE), lambda h, t: (t, col + h))
    prev = lambda col: pl.BlockSpec((tile, LANE), lambda h, t: (jnp.maximum(t - 1, 0), col + h))
    return pl.pallas_call(
        functools.partial(_swa_body, tile=tile),
        grid=(SWA_HEADS, s // tile),
        in_specs=[pl.BlockSpec(memory_space=pltpu.SMEM),
                  pl.BlockSpec((n_pat, SWA_BLK, 2 * SWA_BLK), lambda h, t: (0, 0, 0)),
                  cur(COL_SWA_Q), cur(COL_SWA_K), prev(COL_SWA_K), cur(COL_SWA_V), prev(COL_SWA_V)],
        out_specs=pl.BlockSpec((tile, LANE), lambda h, t: (t, h)),
        out_shape=jax.ShapeDtypeStruct((s, SWA_WIDTH), BF16),
        scratch_shapes=[pltpu.VMEM((n_pat, SWA_BLK, 2 * SWA_BLK), F32),
                        pltpu.VMEM((2 * tile, SWA_DIM), F32),
                        pltpu.VMEM((2 * tile, SWA_DIM), F32),
                        pltpu.VMEM((n_pat, tile, SWA_DIM), F32),
                        pltpu.VMEM((n_pat, tile, SWA_DIM), F32),
                        pltpu.VMEM((n_pat, tile, SWA_DIM), F32)],
        compiler_params=_cparams(("parallel", "arbitrary")),
        name="swa",
    )(rel_bias, bkt, proj, proj, proj, proj, proj)


PRM_W0, PRM_A0, PRM_KK, PRM_KA, PRM_RK, PRM_LNW, PRM_LNB, PRM_MUR, PRM_MUK, PRM_MUV = range(10)
PRM_ROWS = 16


LORA_IN_WIDTH = DECAY_LORA + ICL_LORA + GATE_LORA_PAD


def _rwkv_lora_in_body(wd_ref, ad_ref, gd_ref, mul_ref, mug_ref, o_ref, xb, xg, *, tile):
    @pl.when(pl.program_id(0) == 0)
    def _reset():
        xb[:, 0:8, :] = jnp.zeros((2, 8, LANE), F32)
        xg[0:8, :] = jnp.zeros((8, GATE_LORA_PAD), F32)

    def shifted(buf, src, mu):
        buf[8:8 + tile, :] = src[...]
        cur = buf[8:8 + tile, :]
        prev = buf[7:7 + tile, :]
        buf[0:8, :] = buf[tile:tile + 8, :]
        return cur + mu * (prev - cur)

    wd = shifted(xb.at[0], wd_ref, mul_ref[0:1, :])
    ad = shifted(xb.at[1], ad_ref, mul_ref[1:2, :])
    gd = shifted(xg, gd_ref, mug_ref[...])
    o_ref[:, 0:DECAY_LORA] = jnp.tanh(wd).astype(o_ref.dtype)
    o_ref[:, DECAY_LORA:DECAY_LORA + ICL_LORA] = ad.astype(o_ref.dtype)
    o_ref[:, DECAY_LORA + ICL_LORA:] = _sigmoid(gd).astype(o_ref.dtype)


def _rwkv_lora_in(proj, mu_lora, mu_gate, tile=1024):
    s = proj.shape[0]
    fixed = lambda col: pl.BlockSpec((tile, LANE), lambda t: (t, col))
    return pl.pallas_call(
        functools.partial(_rwkv_lora_in_body, tile=tile),
        grid=(s // tile,),
        in_specs=[fixed(COL_WD), fixed(COL_AD),
                  pl.BlockSpec((tile, GATE_LORA_PAD), lambda t: (t, COL_GD * LANE // GATE_LORA_PAD)),
                  pl.BlockSpec((8, LANE), lambda t: (0, 0)),
                  pl.BlockSpec((1, GATE_LORA_PAD), lambda t: (0, 0))],
        out_specs=pl.BlockSpec((tile, LORA_IN_WIDTH), lambda t: (t, 0)),
        out_shape=jax.ShapeDtypeStruct((s, LORA_IN_WIDTH), BF16),
        scratch_shapes=[pltpu.VMEM((2, tile + 8, LANE), F32),
                        pltpu.VMEM((tile + 8, GATE_LORA_PAD), F32)],
        compiler_params=_cparams(("arbitrary",)),
        name="rwkv_lora_in",
    )(proj, proj, proj, mu_lora, mu_gate)


def _rwkv_body(r_ref, k_ref, v_ref, lora_ref, prm_ref, wup_ref, aup_ref, gup_ref, o_ref, xb, state_ref,
               *, tile):
    c = RWKV_CHUNK
    nd = RWKV_DIM
    t = pl.program_id(1)

    @pl.when(t == 0)
    def _reset():
        state_ref[...] = jnp.zeros_like(state_ref)
        xb[:, 0:8, :] = jnp.zeros((3, 8, LANE), F32)

    prm = prm_ref[...]
    row = lambda i: prm[i:i + 1, :]

    def shifted(idx, src, mu):
        xb[idx, 8:8 + tile, :] = src[...]
        cur = xb[idx, 8:8 + tile, :]
        prev = xb[idx, 7:7 + tile, :]
        xb[idx, 0:8, :] = xb[idx, tile:tile + 8, :]
        return cur + mu * (prev - cur)

    r = shifted(0, r_ref, row(PRM_MUR))
    k = shifted(1, k_ref, row(PRM_MUK))
    v = shifted(2, v_ref, row(PRM_MUV))

    lora = lora_ref[...]
    w_log = -_softplus(-(row(PRM_W0) + _dot(lora[:, 0:DECAY_LORA], wup_ref[...]))) - 0.5
    log_w = -jnp.exp(w_log)
    a_gate = _sigmoid(row(PRM_A0) + _dot(lora[:, DECAY_LORA:DECAY_LORA + ICL_LORA], aup_ref[...]))
    g_gate = _dot(lora[:, DECAY_LORA + ICL_LORA:], gup_ref[...])
    kx = k * row(PRM_KK)
    k_mod = k * (1.0 + (a_gate - 1.0) * row(PRM_KA))

    ri = lax.broadcasted_iota(jnp.int32, (c, c), 0)
    ci = lax.broadcasted_iota(jnp.int32, (c, c), 1)
    lower = jnp.where(ci <= ri, 1.0, 0.0).astype(F32)
    cum = jnp.concatenate([_dot(lower, log_w[n * c:(n + 1) * c], mode=MODE_CUMSUM)
                           for n in range(tile // c)], axis=0)

    r2 = lax.broadcasted_iota(jnp.int32, (2 * c, 2 * c), 0)
    c2 = lax.broadcasted_iota(jnp.int32, (2 * c, 2 * c), 1)
    col_in = jnp.where(c2 < c, c2, c2 - c)
    band = col_in <= jnp.where(r2 < c, r2 - 1, r2 - c)

    n_heads = LANE // nd
    n_chunks = tile // c
    lane = lax.broadcasted_iota(jnp.int32, (1, LANE), 1)
    head_masks = [(lane // nd) == hh for hh in range(n_heads)]

    def head_sum(x):
        out = jnp.zeros_like(x)
        for msk in head_masks:
            out = jnp.where(msk, jnp.sum(jnp.where(msk, x, 0.0), axis=-1, keepdims=True), out)
        return out

    kk = kx * lax.rsqrt(head_sum(kx * kx) + 1e-12)
    b_vec = kk * a_gate
    c_last = jnp.concatenate(
        [jnp.broadcast_to(cum[n * c + c - 1:n * c + c, :], (c, LANE)) for n in range(n_chunks)], axis=0)
    p_out = jnp.exp(-cum)
    a_t = -kk * jnp.exp(cum - log_w)
    r_t = r * jnp.exp(cum)
    k_t = k_mod * p_out
    b_t = b_vec * p_out
    decay_end = jnp.exp(c_last - cum)
    k_e = k_mod * decay_end
    b_e = b_vec * decay_end
    decay_all = jnp.exp(c_last)

    units = []
    for hh in range(n_heads):
        lanes = slice(hh * nd, (hh + 1) * nd)
        for n in range(n_chunks):
            rows = slice(n * c, (n + 1) * c)
            units.append(dict(hh=hh, n=n, a=a_t[rows, lanes], r=r_t[rows, lanes], v=v[rows, lanes],
                              ar=jnp.concatenate([a_t[rows, lanes], r_t[rows, lanes]], axis=0),
                              bk=jnp.concatenate([b_t[rows, lanes], k_t[rows, lanes]], axis=0),
                              bk_e=jnp.concatenate([b_e[rows, lanes], k_e[rows, lanes]], axis=0),
                              decay=decay_all[n * c:n * c + 1, lanes]))
    for un in units:
        un["mm"] = jnp.where(band, _dot(un["ar"], un["bk"], _NT), 0.0)
    t_invs = _unit_lower_inverses([un["mm"][0:c, 0:c] for un in units], c)
    for un in units:
        un["mv"] = _dot(un["mm"][0:c, c:2 * c], un["v"], mode=MODE_SOLVE)
    for un, t_inv in zip(units, t_invs):
        un["wu"] = _dot(t_inv, jnp.concatenate([un["a"], un["mv"]], axis=1), mode=MODE_SOLVE)

    for un in units:
        un["trans"] = _dot(un["wu"][:, 0:nd], un["bk_e"][0:c], _TN, mode=MODE_STATE)
        un["const"] = _dot(jnp.concatenate([un["wu"][:, nd:2 * nd], un["v"]], axis=0), un["bk_e"], _TN,
                           mode=MODE_STATE)
    states = [state_ref[hh] for hh in range(n_heads)]
    for n in range(n_chunks):
        for un in units:
            if un["n"] != n:
                continue
            hh = un["hh"]
            un["s0"] = states[hh]
            states[hh] = (states[hh] * un["decay"] + _dot(states[hh], un["trans"], mode=MODE_STATE)
                          + un["const"])
    for hh in range(n_heads):
        state_ref[hh] = states[hh]
    ys = [[None] * n_chunks for _ in range(n_heads)]
    for un in units:
        un["sr"] = _dot(jnp.concatenate([un["wu"][:, 0:nd], un["r"]], axis=0), un["s0"], _NT,
                        mode=MODE_STATE)
    for un in units:
        u = un["sr"][0:c] + un["wu"][:, nd:2 * nd]
        uv = jnp.concatenate([u, un["v"]], axis=0)
        ys[un["hh"]][un["n"]] = un["sr"][c:2 * c] + _dot(un["mm"][c:2 * c], uv, mode=MODE_STATE)
    y = jnp.concatenate([jnp.concatenate(ys[hh], axis=0) for hh in range(n_heads)], axis=1)
    mean = head_sum(y) * (1.0 / nd)
    var = head_sum(jnp.square(y - mean)) * (1.0 / nd)
    y_n = (y - mean) * lax.rsqrt(var + GN_EPS) * row(PRM_LNW) + row(PRM_LNB)
    bonus = head_sum(r * k_mod * row(PRM_RK)) * v
    o_ref[...] = ((y_n + bonus) * g_gate).astype(o_ref.dtype)


def _rwkv(proj, prm, mu_lora, mu_gate, w_up, a_up, g_up, tile=512):
    s = proj.shape[0]
    pairs = RWKV_WIDTH // LANE
    lora_in = _rwkv_lora_in(proj, mu_lora, mu_gate)
    blk = lambda col: pl.BlockSpec((tile, LANE), lambda h, t: (t, col + h))
    return pl.pallas_call(
        functools.partial(_rwkv_body, tile=tile),
        grid=(pairs, s // tile),
        in_specs=[blk(COL_R), blk(COL_K), blk(COL_V),
                  pl.BlockSpec((tile, LORA_IN_WIDTH), lambda h, t: (t, 0)),
                  pl.BlockSpec((PRM_ROWS, LANE), lambda h, t: (0, h)),
                  pl.BlockSpec((DECAY_LORA, LANE), lambda h, t: (0, h)),
                  pl.BlockSpec((ICL_LORA, LANE), lambda h, t: (0, h)),
                  pl.BlockSpec((GATE_LORA_PAD, LANE), lambda h, t: (0, h))],
        out_specs=pl.BlockSpec((tile, LANE), lambda h, t: (t, h)),
        out_shape=jax.ShapeDtypeStruct((s, RWKV_WIDTH), BF16),
        scratch_shapes=[pltpu.VMEM((3, tile + 8, LANE), F32),
                        pltpu.VMEM((LANE // RWKV_DIM, RWKV_DIM, RWKV_DIM), F32)],
        compiler_params=_cparams(("parallel", "arbitrary")),
        name="rwkv7",
    )(proj, proj, proj, lora_in, prm, w_up, a_up, g_up)


def _layout_rwkv_params(mu, w0, a0, k_k, k_a, r_k, ln_w, ln_b):
    rw = RWKV_WIDTH
    mu_r, mu_k, mu_v = mu[0:rw], mu[rw:2 * rw], mu[2 * rw:3 * rw]
    o = 3 * rw
    mu_wd = mu[o:o + DECAY_LORA]; o += DECAY_LORA
    mu_ad = mu[o:o + ICL_LORA]; o += ICL_LORA
    mu_gd = mu[o:o + GATE_LORA]
    rows = [w0, a0, k_k, k_a, r_k.reshape(rw), ln_w, ln_b, mu_r, mu_k, mu_v]
    prm = jnp.stack(rows + [jnp.zeros((rw,), F32)] * (PRM_ROWS - len(rows))).astype(F32)
    mu_lora = jnp.stack([mu_wd, mu_ad] + [jnp.zeros((LANE,), F32)] * 6).astype(F32)
    mu_gate = jnp.concatenate([mu_gd, jnp.zeros((GATE_LORA_PAD - GATE_LORA,), F32)]).reshape(1, GATE_LORA_PAD)
    return prm, mu_lora, mu_gate


def kernel(x, attn_norm, w_in, gdn_conv, gdn_a_log, gdn_dt_bias, gdn_norm, rwkv_mu, rwkv_w0, rwkv_w_up, rwkv_a0, rwkv_a_up, rwkv_g_up, rwkv_k_k, rwkv_k_a, rwkv_r_k, rwkv_ln_w, rwkv_ln_b, w_out, ffn_norm, w_ffn_gate, w_ffn_up, ffn_conv, ffn_conv_b, w_ffn_down, rel_bias, final_norm):
    batch, seq, d = x.shape
    depth = w_in.shape[0]
    outs = []
    rows_in = [x.reshape(seq, d)] if batch == 1 else [x[b] for b in range(batch)]
    for xb in rows_in:
        for l in range(depth):
            h = _rmsnorm(xb, attn_norm[l], BF16)
            proj = _in_proj(h, _layout_w_in_pallas(w_in, l))
            o_a = _gdn(proj, gdn_conv[l], gdn_a_log[l], gdn_dt_bias[l], gdn_norm[l])
            o_b = _swa(proj, rel_bias)
            prm, mu_lora, mu_gate = _layout_rwkv_params(
                rwkv_mu[l], rwkv_w0[l], rwkv_a0[l], rwkv_k_k[l], rwkv_k_a[l], rwkv_r_k[l],
                rwkv_ln_w[l], rwkv_ln_b[l])
            g_up = jnp.concatenate(
                [rwkv_g_up[l], jnp.zeros((GATE_LORA_PAD - GATE_LORA, RWKV_WIDTH), F32)], axis=0)
            o_c = _rwkv(proj, prm, mu_lora, mu_gate, rwkv_w_up[l].astype(BF16),
                        rwkv_a_up[l].astype(BF16), g_up.astype(BF16))
            wo = _to_bf16(w_out, l)
            xb = _out_proj(xb, o_a, o_b, o_c, wo[0:GDN_WIDTH], wo[GDN_WIDTH:GDN_WIDTH + SWA_WIDTH],
                           wo[GDN_WIDTH + SWA_WIDTH:])
            h = _rmsnorm(xb, ffn_norm[l], BF16)
            act = _ffn_up(h, _to_bf16(w_ffn_gate, l), _to_bf16(w_ffn_up, l), ffn_conv[l], ffn_conv_b[l])
            xb = _ffn_down(xb, act, _to_bf16(w_ffn_down, l))
        outs.append(_rmsnorm(xb, final_norm, x.dtype))
    return outs[0].reshape(1, seq, d) if batch == 1 else jnp.stack(outs)
```

```python
import functools
import math

import numpy as np
import jax
import jax.numpy as jnp
from jax import lax
from jax.experimental import pallas as pl
from jax.experimental.pallas import tpu as pltpu

F32 = jnp.float32
BF16 = jnp.bfloat16
HIGHEST = lax.Precision.HIGHEST

LANE = 128
D_MODEL = 4096
RMS_EPS = 1e-6
GN_EPS = 64e-5
NEG_INF = -1e30
VMEM_LIMIT = 56 * 1024 * 1024

GDN_DIM = 128
GDN_HEADS = 12
GDN_WIDTH = GDN_HEADS * GDN_DIM
GDN_CONV = 4
GDN_CHUNK = 64
GDN_GROUP = 2
SWA_DIM = 128
SWA_HEADS = 8
SWA_WIDTH = SWA_HEADS * SWA_DIM
SWA_PATTERNS = ((128, 1), (512, 4), (2048, 16))
SWA_BLK = 128
SWA_TILE = 2048
SWA_GROUP = 4
NUM_BUCKETS = 32
MAX_DISTANCE = 2048
RWKV_DIM = 64
RWKV_HEADS = 24
RWKV_WIDTH = RWKV_HEADS * RWKV_DIM
RWKV_CHUNK = 64
RWKV_GROUP = 1
DECAY_LORA = 128
ICL_LORA = 128
GATE_LORA = 480
GATE_LORA_PAD = 512
D_FF = 11008
FFN_CONV = 3

COL_GDN_Q, COL_GDN_K, COL_GDN_V, COL_GDN_Z, COL_GDN_BA = 0, 12, 24, 36, 48
COL_SWA_Q, COL_SWA_K, COL_SWA_V = 49, 57, 65
COL_WD, COL_AD = 73, 74
COL_GD = 76
COL_R, COL_K, COL_V = 80, 92, 104
PROJ_BLOCKS = 116
PROJ_WIDTH = PROJ_BLOCKS * LANE


def _cparams(sem):
    return pltpu.CompilerParams(dimension_semantics=sem, vmem_limit_bytes=VMEM_LIMIT)


def _silu(x):
    return x * (1.0 / (1.0 + jnp.exp(-x)))


def _sigmoid(x):
    return 1.0 / (1.0 + jnp.exp(-x))


def _softplus(x):
    return jnp.maximum(x, 0.0) + jnp.log(1.0 + jnp.exp(-jnp.abs(x)))


def _split_bf16(x):
    hi = x.astype(BF16)
    return hi, (x - hi.astype(F32)).astype(BF16)


def _dot(a, b, dims=(((1,), (0,)), ((), ())), mode="bf16"):
    if mode == "f32":
        return lax.dot_general(a.astype(F32), b.astype(F32), dims, precision=HIGHEST,
                               preferred_element_type=F32)
    if mode == "x3":
        a_hi, a_lo = _split_bf16(a)
        b_hi, b_lo = _split_bf16(b)
        mm = lambda p, q: lax.dot_general(p, q, dims, preferred_element_type=F32)
        return mm(a_hi, b_hi) + (mm(a_hi, b_lo) + mm(a_lo, b_hi))
    return lax.dot_general(a.astype(BF16), b.astype(BF16), dims, preferred_element_type=F32)


MODE_CUMSUM = "f32"
MODE_INV = "bf16"
MODE_SOLVE = "bf16"
MODE_STATE = "bf16"


_NT = (((1,), (1,)), ((), ()))
_TN = (((0,), (0,)), ((), ()))


def _rmsnorm_body(x_ref, w_ref, o_ref):
    x = x_ref[...]
    ms = jnp.mean(x * x, axis=-1, keepdims=True)
    o_ref[...] = (x * lax.rsqrt(ms + RMS_EPS) * w_ref[...]).astype(o_ref.dtype)


def _rmsnorm(x, w, out_dtype, tm=512):
    s, d = x.shape
    return pl.pallas_call(
        _rmsnorm_body,
        grid=(s // tm,),
        in_specs=[pl.BlockSpec((tm, d), lambda i: (i, 0)),
                  pl.BlockSpec((1, d), lambda i: (0, 0))],
        out_specs=pl.BlockSpec((tm, d), lambda i: (i, 0)),
        out_shape=jax.ShapeDtypeStruct((s, d), out_dtype),
        compiler_params=_cparams(("parallel",)),
        name="rmsnorm",
    )(x, w.reshape(1, d))


def _cast_body(w_ref, o_ref):
    o_ref[...] = w_ref[...].astype(o_ref.dtype)


def _to_bf16(w, layer, rows=256):
    _, r, c = w.shape
    return pl.pallas_call(
        _cast_body,
        grid=(r // rows,),
        in_specs=[pl.BlockSpec((None, rows, c), lambda i: (layer, i, 0))],
        out_specs=pl.BlockSpec((rows, c), lambda i: (i, 0)),
        out_shape=jax.ShapeDtypeStruct((r, c), BF16),
        compiler_params=_cparams(("parallel",)),
        name="to_bf16",
    )(w)


def _w_in_segments():
    gw, sw, rw = GDN_WIDTH, SWA_WIDTH, RWKV_WIDTH
    src = 0
    segs = []
    for width, dst in ((4 * gw, COL_GDN_Q), (2 * GDN_HEADS, COL_GDN_BA), (3 * sw, COL_SWA_Q),
                       (3 * rw, COL_R), (DECAY_LORA, COL_WD), (ICL_LORA, COL_AD), (GATE_LORA, COL_GD)):
        segs.append((src, width, dst * LANE))
        src += width
    return segs, src


def _layout_w_in_body(w_ref, o_ref):
    o_ref[...] = jnp.zeros_like(o_ref)
    for src, width, dst in _w_in_segments()[0]:
        o_ref[:, dst:dst + width] = w_ref[:, src:src + width].astype(o_ref.dtype)


def _layout_w_in_pallas(w, layer, rows=32):
    _, r, c = w.shape
    assert c == _w_in_segments()[1]
    return pl.pallas_call(
        _layout_w_in_body,
        grid=(r // rows,),
        in_specs=[pl.BlockSpec((None, rows, c), lambda i: (layer, i, 0))],
        out_specs=pl.BlockSpec((rows, PROJ_WIDTH), lambda i: (i, 0)),
        out_shape=jax.ShapeDtypeStruct((r, PROJ_WIDTH), BF16),
        compiler_params=_cparams(("parallel",)),
        name="layout_w_in",
    )(w)


def _matmul_body(a_ref, b_ref, o_ref):
    o_ref[...] = jnp.dot(a_ref[...], b_ref[...], preferred_element_type=F32).astype(o_ref.dtype)


def _in_proj(h, w, tm=1024, tn=512):
    s, k = h.shape
    n = w.shape[1]
    return pl.pallas_call(
        _matmul_body,
        grid=(s // tm, n // tn),
        in_specs=[pl.BlockSpec((tm, k), lambda i, j: (i, 0)),
                  pl.BlockSpec((k, tn), lambda i, j: (0, j))],
        out_specs=pl.BlockSpec((tm, tn), lambda i, j: (i, j)),
        out_shape=jax.ShapeDtypeStruct((s, n), F32),
        compiler_params=_cparams(("parallel", "arbitrary")),
        name="in_proj",
    )(h, w)


def _out_proj_body(x_ref, a_ref, b_ref, c_ref, wa_ref, wb_ref, wc_ref, o_ref):
    acc = jnp.dot(a_ref[...], wa_ref[...], preferred_element_type=F32)
    acc += jnp.dot(b_ref[...], wb_ref[...], preferred_element_type=F32)
    acc += jnp.dot(c_ref[...], wc_ref[...], preferred_element_type=F32)
    o_ref[...] = x_ref[...] + acc


def _out_proj(x, oa, ob, oc, wa, wb, wc, tm=1024, tn=512):
    s, d = x.shape
    ka, kb, kc = oa.shape[1], ob.shape[1], oc.shape[1]
    return pl.pallas_call(
        _out_proj_body,
        grid=(s // tm, d // tn),
        in_specs=[pl.BlockSpec((tm, tn), lambda i, j: (i, j)),
                  pl.BlockSpec((tm, ka), lambda i, j: (i, 0)),
                  pl.BlockSpec((tm, kb), lambda i, j: (i, 0)),
                  pl.BlockSpec((tm, kc), lambda i, j: (i, 0)),
                  pl.BlockSpec((ka, tn), lambda i, j: (0, j)),
                  pl.BlockSpec((kb, tn), lambda i, j: (0, j)),
                  pl.BlockSpec((kc, tn), lambda i, j: (0, j))],
        out_specs=pl.BlockSpec((tm, tn), lambda i, j: (i, j)),
        out_shape=jax.ShapeDtypeStruct((s, d), F32),
        compiler_params=_cparams(("parallel", "arbitrary")),
        name="out_proj",
    )(x, oa, ob, oc, wa, wb, wc)


FFN_HALO = 16


def _ffn_up_body(h_ref, halo_ref, wg_ref, wu_ref, cw_ref, cb_ref, o_ref, hbuf, *, tm):
    i = pl.program_id(0)

    @pl.when(pl.program_id(1) == 0)
    def _stage():
        halo = halo_ref[...]
        hbuf[0:FFN_HALO, :] = jnp.where(i == 0, jnp.zeros_like(halo), halo)
        hbuf[FFN_HALO:FFN_HALO + tm, :] = h_ref[...]

    g = jnp.dot(hbuf[...], wg_ref[...], preferred_element_type=F32)
    u = jnp.dot(hbuf[FFN_HALO:FFN_HALO + tm, :], wu_ref[...], preferred_element_type=F32)
    cw = cw_ref[...]
    conv = (cw[0:1, :] * pltpu.roll(g, 2, 0)[FFN_HALO:, :]
            + cw[1:2, :] * pltpu.roll(g, 1, 0)[FFN_HALO:, :]
            + cw[2:3, :] * g[FFN_HALO:, :]) + cb_ref[...]
    o_ref[...] = (_silu(conv) * u).astype(o_ref.dtype)


def _ffn_up(h, wg, wu, cw, cb, tm=1024, tn=256):
    s, k = h.shape
    f = wg.shape[1]
    halo_blocks = tm // FFN_HALO
    return pl.pallas_call(
        functools.partial(_ffn_up_body, tm=tm),
        grid=(s // tm, f // tn),
        in_specs=[pl.BlockSpec((tm, k), lambda i, j: (i, 0)),
                  pl.BlockSpec((FFN_HALO, k), lambda i, j: (jnp.maximum(i * halo_blocks - 1, 0), 0)),
                  pl.BlockSpec((k, tn), lambda i, j: (0, j)),
                  pl.BlockSpec((k, tn), lambda i, j: (0, j)),
                  pl.BlockSpec((FFN_CONV, tn), lambda i, j: (0, j)),
                  pl.BlockSpec((1, tn), lambda i, j: (0, j))],
        out_specs=pl.BlockSpec((tm, tn), lambda i, j: (i, j)),
        out_shape=jax.ShapeDtypeStruct((s, f), BF16),
        scratch_shapes=[pltpu.VMEM((tm + FFN_HALO, k), BF16)],
        compiler_params=_cparams(("parallel", "arbitrary")),
        name="ffn_up",
    )(h, h, wg, wu, cw, cb.reshape(1, f))


def _ffn_down_body(x_ref, a_ref, w_ref, o_ref, acc_ref):
    kk = pl.program_id(2)

    @pl.when(kk == 0)
    def _init():
        acc_ref[...] = x_ref[...]

    acc_ref[...] += jnp.dot(a_ref[...], w_ref[...], preferred_element_type=F32)

    @pl.when(kk == pl.num_programs(2) - 1)
    def _done():
        o_ref[...] = acc_ref[...]


def _ffn_down(x, act, w, tm=1024, tn=512, ksplit=2):
    s, d = x.shape
    f = act.shape[1]
    tk = f // ksplit
    return pl.pallas_call(
        _ffn_down_body,
        grid=(s // tm, d // tn, ksplit),
        in_specs=[pl.BlockSpec((tm, tn), lambda i, j, kk: (i, j)),
                  pl.BlockSpec((tm, tk), lambda i, j, kk: (i, kk)),
                  pl.BlockSpec((tk, tn), lambda i, j, kk: (kk, j))],
        out_specs=pl.BlockSpec((tm, tn), lambda i, j, kk: (i, j)),
        out_shape=jax.ShapeDtypeStruct((s, d), F32),
        scratch_shapes=[pltpu.VMEM((tm, tn), F32)],
        compiler_params=_cparams(("parallel", "arbitrary", "arbitrary")),
        name="ffn_down",
    )(x, act, w)


def _unit_lower_inverses(n_mats, size, out):
    row = lax.broadcasted_iota(jnp.int32, (size, size), 0)
    col = lax.broadcasted_iota(jnp.int32, (size, size), 1)
    eye = jnp.where(row == col, 1.0, 0.0).astype(F32)
    invs = [eye + n for n in n_mats]
    powers = [_dot(n, n, mode=MODE_INV) for n in n_mats]
    yield
    steps = int(math.log2(size)) - 1
    for s in range(steps):
        last = s == steps - 1
        for i, (inv, p) in enumerate(zip(invs, powers)):
            if last:
                invs[i] = inv + _dot(inv, p, mode=MODE_INV)
            else:
                both = _dot(jnp.concatenate([inv, p], axis=0), p, mode=MODE_INV)
                invs[i] = inv + both[0:size]
                powers[i] = both[size:2 * size]
        yield
    out.extend(invs)


def _alternate(*stage_generators):
    live = list(stage_generators)
    while live:
        for gen in list(live):
            try:
                next(gen)
            except StopIteration:
                live.remove(gen)


def _gdn_body(alog_ref, dtb_ref, q_ref, k_ref, v_ref, z_ref, ba_ref, cq_ref, ck_ref, cv_ref,
              nw_ref, o_ref, xbuf, state_ref, *, tile):
    c = GDN_CHUNK
    hd = GDN_DIM
    width = GDN_GROUP * hd
    n_chunks = tile // c
    t = pl.program_id(1)

    @pl.when(t == 0)
    def _reset():
        state_ref[...] = jnp.zeros_like(state_ref)
        xbuf[:, 0:8, :] = jnp.zeros((3, 8, width), F32)

    convs = []
    for idx, (src, cw_ref) in enumerate(((q_ref, cq_ref), (k_ref, ck_ref), (v_ref, cv_ref))):
        xbuf[idx, 8:8 + tile, :] = src[...]
        cw = cw_ref[...]
        acc = cw[0:1, :] * xbuf[idx, 5:5 + tile, :]
        acc += cw[1:2, :] * xbuf[idx, 6:6 + tile, :]
        acc += cw[2:3, :] * xbuf[idx, 7:7 + tile, :]
        acc += cw[3:4, :] * xbuf[idx, 8:8 + tile, :]
        convs.append(_silu(acc))
        xbuf[idx, 0:8, :] = xbuf[idx, tile:tile + 8, :]
    q_all, k_all, v_all = convs

    ba = ba_ref[...]
    lane = lax.broadcasted_iota(jnp.int32, (1, LANE), 1)
    ri = lax.broadcasted_iota(jnp.int32, (c, c), 0)
    ci = lax.broadcasted_iota(jnp.int32, (c, c), 1)
    lower = jnp.where(ci <= ri, 1.0, 0.0).astype(F32)
    upper = jnp.where(ri <= ci, 1.0, 0.0).astype(F32)
    r2 = lax.broadcasted_iota(jnp.int32, (2 * c, c), 0)
    c2 = lax.broadcasted_iota(jnp.int32, (2 * c, c), 1)
    band = c2 <= jnp.where(r2 < c, r2 - 1, r2 - c)
    nw = nw_ref[...]

    units = []
    for j in range(GDN_GROUP):
        h = pl.program_id(0) * GDN_GROUP + j
        cols = slice(j * hd, (j + 1) * hd)
        q, k, v = q_all[:, cols], k_all[:, cols], v_all[:, cols]
        q = q * lax.rsqrt(jnp.sum(q * q, axis=-1, keepdims=True) + 1e-6) * (hd ** -0.5)
        k = k * lax.rsqrt(jnp.sum(k * k, axis=-1, keepdims=True) + 1e-6)
        b_col = jnp.sum(jnp.where(lane == h, ba, 0.0), axis=-1, keepdims=True)
        a_col = jnp.sum(jnp.where(lane == h + GDN_HEADS, ba, 0.0), axis=-1, keepdims=True)
        beta = _sigmoid(b_col)
        g = -jnp.exp(alog_ref[h]) * _softplus(a_col + dtb_ref[h])
        g_b = g * jnp.ones((1, hd), F32)
        g_col = jnp.concatenate([_dot(lower, g_b[n * c:(n + 1) * c], mode=MODE_CUMSUM)
                                 for n in range(n_chunks)], axis=0)
        g_row = jnp.concatenate([_dot(jnp.ones((8, c), F32), g_b[n * c:(n + 1) * c, 0:c] * upper,
                                      mode=MODE_CUMSUM) for n in range(n_chunks)], axis=1)
        exp_g = jnp.exp(g_col)
        kb = k * beta
        vb = v * beta
        kbe = kb * exp_g
        qg = q * exp_g
        for n in range(n_chunks):
            rows = slice(n * c, (n + 1) * c)
            g_last = g_col[n * c + c - 1:n * c + c, :]
            gamma = jnp.exp(jnp.minimum(g_col[rows, 0:c] - g_row[0:1, n * c:(n + 1) * c], 0.0))
            units.append(dict(j=j, n=n, k=k[rows], kq=jnp.concatenate([kb[rows], q[rows]], axis=0),
                              rhs=jnp.concatenate([vb[rows], kbe[rows]], axis=1), qg=qg[rows],
                              kd=k[rows] * jnp.exp(g_last - g_col[rows]), decay=jnp.exp(g_last),
                              gamma2=jnp.concatenate([gamma, gamma], axis=0)))
    states = [state_ref[j] for j in range(GDN_GROUP)]
    outs = [[None] * n_chunks for _ in range(GDN_GROUP)]

    def prepare(group):
        for un in group:
            un["scores"] = jnp.where(band, _dot(un["kq"], un["k"], _NT) * un["gamma2"], 0.0)
        yield
        t_invs = []
        yield from _unit_lower_inverses([-un["scores"][0:c] for un in group], c, t_invs)
        for un, t_inv in zip(group, t_invs):
            un["uw"] = _dot(t_inv, un["rhs"], mode=MODE_SOLVE)
        yield
        for un in group:
            un["trans"] = _dot(un["kd"], un["uw"][:, hd:2 * hd], _TN, mode=MODE_STATE)
            un["const"] = _dot(un["kd"], un["uw"][:, 0:hd], _TN, mode=MODE_STATE)
        yield

    def chain(group):
        for n in sorted({un["n"] for un in group}):
            for un in group:
                if un["n"] == n:
                    j = un["j"]
                    un["s0"] = states[j]
                    states[j] = (states[j] * un["decay"] - _dot(un["trans"], states[j], mode=MODE_STATE)
                                 + un["const"])
            yield

    def outputs(group, chunks_per_stage):
        chunk_ids = sorted({un["n"] for un in group})
        for i in range(0, len(chunk_ids), chunks_per_stage):
            part = [un for un in group if un["n"] in chunk_ids[i:i + chunks_per_stage]]
            for un in part:
                un["ws"] = _dot(jnp.concatenate([un["uw"][:, hd:2 * hd], un["qg"]], axis=0), un["s0"],
                                mode=MODE_STATE)
            yield
            for un in part:
                v_new = un["uw"][:, 0:hd] - un["ws"][0:c]
                outs[un["j"]][un["n"]] = (un["ws"][c:2 * c]
                                          + _dot(un["scores"][c:2 * c], v_new, mode=MODE_STATE))
            yield

    _alternate(prepare(units))
    _alternate(chain(units))
    _alternate(outputs(units, n_chunks))
    for j in range(GDN_GROUP):
        state_ref[j] = states[j]
    for j in range(GDN_GROUP):
        cols = slice(j * hd, (j + 1) * hd)
        o = jnp.concatenate(outs[j], axis=0)
        o = o * lax.rsqrt(jnp.mean(o * o, axis=-1, keepdims=True) + RMS_EPS) * nw
        o_ref[:, cols] = (o * _silu(z_ref[:, cols])).astype(o_ref.dtype)


def _gdn(proj, conv_w, a_log, dt_bias, norm_w, tile=512):
    s = proj.shape[0]
    width = GDN_GROUP * GDN_DIM
    per = LANE // GDN_DIM * GDN_GROUP
    blk = lambda col: pl.BlockSpec((tile, width), lambda h, t: (t, col // per + h))
    cblk = lambda col: pl.BlockSpec((GDN_CONV, width), lambda h, t: (0, col // per + h))
    smem = pl.BlockSpec(memory_space=pltpu.SMEM)
    return pl.pallas_call(
        functools.partial(_gdn_body, tile=tile),
        grid=(GDN_HEADS // GDN_GROUP, s // tile),
        in_specs=[smem, smem,
                  blk(COL_GDN_Q), blk(COL_GDN_K), blk(COL_GDN_V), blk(COL_GDN_Z),
                  pl.BlockSpec((tile, LANE), lambda h, t: (t, COL_GDN_BA)),
                  cblk(0), cblk(GDN_HEADS), cblk(2 * GDN_HEADS),
                  pl.BlockSpec((1, GDN_DIM), lambda h, t: (0, 0))],
        out_specs=pl.BlockSpec((tile, width), lambda h, t: (t, h)),
        out_shape=jax.ShapeDtypeStruct((s, GDN_WIDTH), BF16),
        scratch_shapes=[pltpu.VMEM((3, tile + 8, width), F32),
                        pltpu.VMEM((GDN_GROUP, GDN_DIM, GDN_DIM), F32)],
        compiler_params=_cparams(("parallel", "arbitrary")),
        name="gdn",
    )(a_log, dt_bias, proj, proj, proj, proj, proj, conv_w, conv_w, conv_w, norm_w.reshape(1, GDN_DIM))


def _t5_bucket_table():
    exact = NUM_BUCKETS // 2
    i = np.arange(SWA_BLK)[:, None]
    j = np.arange(2 * SWA_BLK)[None, :]
    steps = np.maximum(i + SWA_BLK - j, 0)
    tables = []
    for _, dilation in SWA_PATTERNS:
        dist = steps * dilation
        d = np.maximum(dist, 1).astype(np.float32)
        ratio = (np.log(d / np.float32(exact)) / np.float32(math.log(MAX_DISTANCE / exact))
                 * np.float32(NUM_BUCKETS - exact)).astype(np.float32)
        log_b = exact + ratio.astype(np.int32)
        tables.append(np.where(dist < exact, dist, np.minimum(log_b, NUM_BUCKETS - 1)))
    return np.stack(tables).astype(np.int32)


def _swa_body(rb_ref, bkt_ref, q_ref, kc_ref, kp_ref, vc_ref, vp_ref, o_ref,
              bias_ref, kbuf, vbuf, o_scr, m_scr, l_scr, *, tile):
    h = pl.program_id(0)
    t = pl.program_id(1)
    blk = SWA_BLK

    @pl.when(t == 0)
    def _bias():
        for p in range(len(SWA_PATTERNS)):
            bkt = bkt_ref[p]
            bias = jnp.zeros((blk, 2 * blk), F32)
            for b in range(NUM_BUCKETS):
                bias = jnp.where(bkt == b, rb_ref[b, h], bias)
            bias_ref[p] = bias

    kbuf[0:tile, :] = kp_ref[...]
    kbuf[tile:2 * tile, :] = kc_ref[...]
    vbuf[0:tile, :] = vp_ref[...]
    vbuf[tile:2 * tile, :] = vc_ref[...]

    qi = lax.broadcasted_iota(jnp.int32, (blk, 2 * blk), 0)
    kj = lax.broadcasted_iota(jnp.int32, (blk, 2 * blk), 1)
    in_band = (kj >= qi) & (kj <= qi + blk)
    scale = SWA_DIM ** -0.5

    band_first = in_band & (kj >= jnp.where(t == 0, blk, 0))
    ones_row = jnp.ones((1, SWA_DIM), F32)

    for p, (window, dil) in enumerate(SWA_PATTERNS):
        span = blk * dil
        bias = bias_ref[p]
        blocks = [(res + n * span, n) for n in range(tile // span) for res in range(dil)]
        for g0 in range(0, len(blocks), SWA_GROUP):
            group = blocks[g0:g0 + SWA_GROUP]
            scores = []
            for start, n in group:
                qb = q_ref[pl.ds(start, blk, stride=dil), :] * scale
                kw = kbuf[pl.ds(tile + start - span, 2 * blk, stride=dil), :]
                sc = _dot(qb, kw, _NT) + bias
                scores.append(jnp.where(band_first if n == 0 else in_band, sc, NEG_INF))
            probs = []
            for sc in scores:
                m = jnp.max(sc, axis=-1, keepdims=True)
                pe = jnp.exp(sc - m)
                probs.append((pe, m, jnp.sum(pe, axis=-1, keepdims=True)))
            for (start, n), (pe, m, l) in zip(group, probs):
                vw = vbuf[pl.ds(tile + start - span, 2 * blk, stride=dil), :]
                o_scr[p, pl.ds(start, blk, stride=dil), :] = _dot(pe, vw)
                m_scr[p, pl.ds(start, blk, stride=dil), :] = m * ones_row
                l_scr[p, pl.ds(start, blk, stride=dil), :] = l * ones_row

    m_max = jnp.maximum(jnp.maximum(m_scr[0], m_scr[1]), m_scr[2])
    num = jnp.zeros((tile, SWA_DIM), F32)
    den = jnp.zeros((tile, SWA_DIM), F32)
    for p in range(len(SWA_PATTERNS)):
        sc = jnp.exp(m_scr[p] - m_max)
        num += o_scr[p] * sc
        den += l_scr[p] * sc
    o_ref[...] = (num / den).astype(o_ref.dtype)


def _swa(proj, rel_bias, tile=SWA_TILE):
    s = proj.shape[0]
    n_pat = len(SWA_PATTERNS)
    bkt = jnp.asarray(_t5_bucket_table())
    cur = lambda col: pl.BlockSpec((tile, LANE), lambda h, t: (t, col + h))
    prev = lambda col: pl.BlockSpec((tile, LANE), lambda h, t: (jnp.maximum(t - 1, 0), col + h))
    return pl.pallas_call(
        functools.partial(_swa_body, tile=tile),
        grid=(SWA_HEADS, s // tile),
        in_specs=[pl.BlockSpec(memory_space=pltpu.SMEM),
                  pl.BlockSpec((n_pat, SWA_BLK, 2 * SWA_BLK), lambda h, t: (0, 0, 0)),
                  cur(COL_SWA_Q), cur(COL_SWA_K), prev(COL_SWA_K), cur(COL_SWA_V), prev(COL_SWA_V)],
        out_specs=pl.BlockSpec((tile, LANE), lambda h, t: (t, h)),
        out_shape=jax.ShapeDtypeStruct((s, SWA_WIDTH), BF16),
        scratch_shapes=[pltpu.VMEM((n_pat, SWA_BLK, 2 * SWA_BLK), F32),
                        pltpu.VMEM((2 * tile, SWA_DIM), F32),
                        pltpu.VMEM((2 * tile, SWA_DIM), F32),
                        pltpu.VMEM((n_pat, tile, SWA_DIM), F32),
                        pltpu.VMEM((n_pat, tile, SWA_DIM), F32),
                        pltpu.VMEM((n_pat, tile, SWA_DIM), F32)],
        compiler_params=_cparams(("parallel", "arbitrary")),
        name="swa",
    )(rel_bias, bkt, proj, proj, proj, proj, proj)


PRM_W0, PRM_A0, PRM_KK, PRM_KA, PRM_RK, PRM_LNW, PRM_LNB, PRM_MUR, PRM_MUK, PRM_MUV = range(10)
PRM_ROWS = 16


LORA_IN_WIDTH = DECAY_LORA + ICL_LORA + GATE_LORA_PAD


def _rwkv_lora_in_body(wd_ref, ad_ref, gd_ref, mul_ref, mug_ref, o_ref, xb, xg, *, tile):
    @pl.when(pl.program_id(0) == 0)
    def _reset():
        xb[:, 0:8, :] = jnp.zeros((2, 8, LANE), F32)
        xg[0:8, :] = jnp.zeros((8, GATE_LORA_PAD), F32)

    def shifted(buf, src, mu):
        buf[8:8 + tile, :] = src[...]
        cur = buf[8:8 + tile, :]
        prev = buf[7:7 + tile, :]
        buf[0:8, :] = buf[tile:tile + 8, :]
        return cur + mu * (prev - cur)

    wd = shifted(xb.at[0], wd_ref, mul_ref[0:1, :])
    ad = shifted(xb.at[1], ad_ref, mul_ref[1:2, :])
    gd = shifted(xg, gd_ref, mug_ref[...])
    o_ref[:, 0:DECAY_LORA] = jnp.tanh(wd).astype(o_ref.dtype)
    o_ref[:, DECAY_LORA:DECAY_LORA + ICL_LORA] = ad.astype(o_ref.dtype)
    o_ref[:, DECAY_LORA + ICL_LORA:] = _sigmoid(gd).astype(o_ref.dtype)


def _rwkv_lora_in(proj, mu_lora, mu_gate, tile=1024):
    s = proj.shape[0]
    fixed = lambda col: pl.BlockSpec((tile, LANE), lambda t: (t, col))
    return pl.pallas_call(
        functools.partial(_rwkv_lora_in_body, tile=tile),
        grid=(s // tile,),
        in_specs=[fixed(COL_WD), fixed(COL_AD),
                  pl.BlockSpec((tile, GATE_LORA_PAD), lambda t: (t, COL_GD * LANE // GATE_LORA_PAD)),
                  pl.BlockSpec((8, LANE), lambda t: (0, 0)),
                  pl.BlockSpec((1, GATE_LORA_PAD), lambda t: (0, 0))],
        out_specs=pl.BlockSpec((tile, LORA_IN_WIDTH), lambda t: (t, 0)),
        out_shape=jax.ShapeDtypeStruct((s, LORA_IN_WIDTH), BF16),
        scratch_shapes=[pltpu.VMEM((2, tile + 8, LANE), F32),
                        pltpu.VMEM((tile + 8, GATE_LORA_PAD), F32)],
        compiler_params=_cparams(("arbitrary",)),
        name="rwkv_lora_in",
    )(proj, proj, proj, mu_lora, mu_gate)


def _rwkv_body(r_ref, k_ref, v_ref, lora_ref, prm_ref, wup_ref, aup_ref, gup_ref, o_ref, xb, state_ref,
               *, tile):
    c = RWKV_CHUNK
    nd = RWKV_DIM
    t = pl.program_id(1)

    @pl.when(t == 0)
    def _reset():
        state_ref[...] = jnp.zeros_like(state_ref)
        xb[:, 0:8, :] = jnp.zeros((3, 8, RWKV_GROUP * LANE), F32)

    prm = prm_ref[...]
    row = lambda i: prm[i:i + 1, :]

    def shifted(idx, src, mu):
        xb[idx, 8:8 + tile, :] = src[...]
        cur = xb[idx, 8:8 + tile, :]
        prev = xb[idx, 7:7 + tile, :]
        xb[idx, 0:8, :] = xb[idx, tile:tile + 8, :]
        return cur + mu * (prev - cur)

    r = shifted(0, r_ref, row(PRM_MUR))
    k = shifted(1, k_ref, row(PRM_MUK))
    v = shifted(2, v_ref, row(PRM_MUV))

    lora = lora_ref[...]
    w_log = -_softplus(-(row(PRM_W0) + _dot(lora[:, 0:DECAY_LORA], wup_ref[...]))) - 0.5
    log_w = -jnp.exp(w_log)
    a_gate = _sigmoid(row(PRM_A0) + _dot(lora[:, DECAY_LORA:DECAY_LORA + ICL_LORA], aup_ref[...]))
    g_gate = _dot(lora[:, DECAY_LORA + ICL_LORA:], gup_ref[...])
    kx = k * row(PRM_KK)
    k_mod = k * (1.0 + (a_gate - 1.0) * row(PRM_KA))

    ri = lax.broadcasted_iota(jnp.int32, (c, c), 0)
    ci = lax.broadcasted_iota(jnp.int32, (c, c), 1)
    lower = jnp.where(ci <= ri, 1.0, 0.0).astype(F32)
    cum = jnp.concatenate([_dot(lower, log_w[n * c:(n + 1) * c], mode=MODE_CUMSUM)
                           for n in range(tile // c)], axis=0)

    r2 = lax.broadcasted_iota(jnp.int32, (2 * c, 2 * c), 0)
    c2 = lax.broadcasted_iota(jnp.int32, (2 * c, 2 * c), 1)
    col_in = jnp.where(c2 < c, c2, c2 - c)
    band = col_in <= jnp.where(r2 < c, r2 - 1, r2 - c)

    width = RWKV_GROUP * LANE
    n_heads = width // nd
    n_chunks = tile // c
    lane = lax.broadcasted_iota(jnp.int32, (1, LANE), 1)
    head_masks = [(lane // nd) == hh for hh in range(LANE // nd)]

    def head_sum(x):
        blocks = []
        for b in range(RWKV_GROUP):
            xs = x[:, b * LANE:(b + 1) * LANE]
            out = jnp.zeros_like(xs)
            for msk in head_masks:
                out = jnp.where(msk, jnp.sum(jnp.where(msk, xs, 0.0), axis=-1, keepdims=True), out)
            blocks.append(out)
        return blocks[0] if RWKV_GROUP == 1 else jnp.concatenate(blocks, axis=1)

    kk = kx * lax.rsqrt(head_sum(kx * kx) + 1e-12)
    b_vec = kk * a_gate
    c_last = jnp.concatenate(
        [jnp.broadcast_to(cum[n * c + c - 1:n * c + c, :], (c, width)) for n in range(n_chunks)], axis=0)
    p_out = jnp.exp(-cum)
    a_t = -kk * jnp.exp(cum - log_w)
    r_t = r * jnp.exp(cum)
    k_t = k_mod * p_out
    b_t = b_vec * p_out
    decay_end = jnp.exp(c_last - cum)
    k_e = k_mod * decay_end
    b_e = b_vec * decay_end
    decay_all = jnp.exp(c_last)

    units = []
    for hh in range(n_heads):
        lanes = slice(hh * nd, (hh + 1) * nd)
        for n in range(n_chunks):
            rows = slice(n * c, (n + 1) * c)
            units.append(dict(hh=hh, n=n, a=a_t[rows, lanes], r=r_t[rows, lanes], v=v[rows, lanes],
                              ar=jnp.concatenate([a_t[rows, lanes], r_t[rows, lanes]], axis=0),
                              bk=jnp.concatenate([b_t[rows, lanes], k_t[rows, lanes]], axis=0),
                              bk_e=jnp.concatenate([b_e[rows, lanes], k_e[rows, lanes]], axis=0),
                              decay=decay_all[n * c:n * c + 1, lanes]))
    states = [state_ref[hh] for hh in range(n_heads)]
    ys = [[None] * n_chunks for _ in range(n_heads)]

    def prepare(group):
        for un in group:
            un["mm"] = jnp.where(band, _dot(un["ar"], un["bk"], _NT), 0.0)
        yield
        t_invs = []
        yield from _unit_lower_inverses([un["mm"][0:c, 0:c] for un in group], c, t_invs)
        for un in group:
            un["mv"] = _dot(un["mm"][0:c, c:2 * c], un["v"], mode=MODE_SOLVE)
        yield
        for un, t_inv in zip(group, t_invs):
            un["wu"] = _dot(t_inv, jnp.concatenate([un["a"], un["mv"]], axis=1), mode=MODE_SOLVE)
        yield
        for un in group:
            un["trans"] = _dot(un["wu"][:, 0:nd], un["bk_e"][0:c], _TN, mode=MODE_STATE)
            un["const"] = _dot(jnp.concatenate([un["wu"][:, nd:2 * nd], un["v"]], axis=0), un["bk_e"],
                               _TN, mode=MODE_STATE)
        yield

    def chain(group):
        for n in sorted({un["n"] for un in group}):
            for un in group:
                if un["n"] == n:
                    hh = un["hh"]
                    un["s0"] = states[hh]
                    states[hh] = (states[hh] * un["decay"]
                                  + _dot(states[hh], un["trans"], mode=MODE_STATE) + un["const"])
            yield

    def outputs(group, chunks_per_stage):
        chunk_ids = sorted({un["n"] for un in group})
        for i in range(0, len(chunk_ids), chunks_per_stage):
            part = [un for un in group if un["n"] in chunk_ids[i:i + chunks_per_stage]]
            for un in part:
                un["sr"] = _dot(jnp.concatenate([un["wu"][:, 0:nd], un["r"]], axis=0), un["s0"], _NT,
                                mode=MODE_STATE)
            yield
            for un in part:
                u = un["sr"][0:c] + un["wu"][:, nd:2 * nd]
                uv = jnp.concatenate([u, un["v"]], axis=0)
                ys[un["hh"]][un["n"]] = un["sr"][c:2 * c] + _dot(un["mm"][c:2 * c], uv, mode=MODE_STATE)
            yield

    _alternate(prepare(units))
    _alternate(chain(units))
    _alternate(outputs(units, n_chunks))
    for hh in range(n_heads):
        state_ref[hh] = states[hh]
    y = jnp.concatenate([jnp.concatenate(ys[hh], axis=0) for hh in range(n_heads)], axis=1)
    mean = head_sum(y) * (1.0 / nd)
    var = head_sum(jnp.square(y - mean)) * (1.0 / nd)
    y_n = (y - mean) * lax.rsqrt(var + GN_EPS) * row(PRM_LNW) + row(PRM_LNB)
    bonus = head_sum(r * k_mod * row(PRM_RK)) * v
    o_ref[...] = ((y_n + bonus) * g_gate).astype(o_ref.dtype)


def _rwkv(proj, prm, mu_lora, mu_gate, w_up, a_up, g_up, tile=512):
    s = proj.shape[0]
    width = RWKV_GROUP * LANE
    groups = RWKV_WIDTH // width
    lora_in = _rwkv_lora_in(proj, mu_lora, mu_gate)
    blk = lambda col: pl.BlockSpec((tile, width), lambda h, t: (t, col // RWKV_GROUP + h))
    return pl.pallas_call(
        functools.partial(_rwkv_body, tile=tile),
        grid=(groups, s // tile),
        in_specs=[blk(COL_R), blk(COL_K), blk(COL_V),
                  pl.BlockSpec((tile, LORA_IN_WIDTH), lambda h, t: (t, 0)),
                  pl.BlockSpec((PRM_ROWS, width), lambda h, t: (0, h)),
                  pl.BlockSpec((DECAY_LORA, width), lambda h, t: (0, h)),
                  pl.BlockSpec((ICL_LORA, width), lambda h, t: (0, h)),
                  pl.BlockSpec((GATE_LORA_PAD, width), lambda h, t: (0, h))],
        out_specs=pl.BlockSpec((tile, width), lambda h, t: (t, h)),
        out_shape=jax.ShapeDtypeStruct((s, RWKV_WIDTH), BF16),
        scratch_shapes=[pltpu.VMEM((3, tile + 8, width), F32),
                        pltpu.VMEM((width // RWKV_DIM, RWKV_DIM, RWKV_DIM), F32)],
        compiler_params=_cparams(("parallel", "arbitrary")),
        name="rwkv7",
    )(proj, proj, proj, lora_in, prm, w_up, a_up, g_up)


def _layout_rwkv_params(mu, w0, a0, k_k, k_a, r_k, ln_w, ln_b):
    rw = RWKV_WIDTH
    mu_r, mu_k, mu_v = mu[0:rw], mu[rw:2 * rw], mu[2 * rw:3 * rw]
    o = 3 * rw
    mu_wd = mu[o:o + DECAY_LORA]; o += DECAY_LORA
    mu_ad = mu[o:o + ICL_LORA]; o += ICL_LORA
    mu_gd = mu[o:o + GATE_LORA]
    rows = [w0, a0, k_k, k_a, r_k.reshape(rw), ln_w, ln_b, mu_r, mu_k, mu_v]
    prm = jnp.stack(rows + [jnp.zeros((rw,), F32)] * (PRM_ROWS - len(rows))).astype(F32)
    mu_lora = jnp.stack([mu_wd, mu_ad] + [jnp.zeros((LANE,), F32)] * 6).astype(F32)
    mu_gate = jnp.concatenate([mu_gd, jnp.zeros((GATE_LORA_PAD - GATE_LORA,), F32)]).reshape(1, GATE_LORA_PAD)
    return prm, mu_lora, mu_gate


def kernel(x, attn_norm, w_in, gdn_conv, gdn_a_log, gdn_dt_bias, gdn_norm, rwkv_mu, rwkv_w0, rwkv_w_up, rwkv_a0, rwkv_a_up, rwkv_g_up, rwkv_k_k, rwkv_k_a, rwkv_r_k, rwkv_ln_w, rwkv_ln_b, w_out, ffn_norm, w_ffn_gate, w_ffn_up, ffn_conv, ffn_conv_b, w_ffn_down, rel_bias, final_norm):
    batch, seq, d = x.shape
    depth = w_in.shape[0]
    outs = []
    w_in_bf16 = w_in.astype(BF16)
    rows_in = [x.reshape(seq, d)] if batch == 1 else [x[b] for b in range(batch)]
    for xb in rows_in:
        for l in range(depth):
            h = _rmsnorm(xb, attn_norm[l], BF16)
            proj = _in_proj(h, _layout_w_in_pallas(w_in_bf16, l))
            o_a = _gdn(proj, gdn_conv[l], gdn_a_log[l], gdn_dt_bias[l], gdn_norm[l])
            o_b = _swa(proj, rel_bias)
            prm, mu_lora, mu_gate = _layout_rwkv_params(
                rwkv_mu[l], rwkv_w0[l], rwkv_a0[l], rwkv_k_k[l], rwkv_k_a[l], rwkv_r_k[l],
                rwkv_ln_w[l], rwkv_ln_b[l])
            g_up = jnp.concatenate(
                [rwkv_g_up[l], jnp.zeros((GATE_LORA_PAD - GATE_LORA, RWKV_WIDTH), F32)], axis=0)
            o_c = _rwkv(proj, prm, mu_lora, mu_gate, rwkv_w_up[l].astype(BF16),
                        rwkv_a_up[l].astype(BF16), g_up.astype(BF16))
            wo = _to_bf16(w_out, l)
            xb = _out_proj(xb, o_a, o_b, o_c, wo[0:GDN_WIDTH], wo[GDN_WIDTH:GDN_WIDTH + SWA_WIDTH],
                           wo[GDN_WIDTH + SWA_WIDTH:])
            h = _rmsnorm(xb, ffn_norm[l], BF16)
            act = _ffn_up(h, _to_bf16(w_ffn_gate, l), _to_bf16(w_ffn_up, l), ffn_conv[l], ffn_conv_b[l])
            xb = _ffn_down(xb, act, _to_bf16(w_ffn_down, l))
        outs.append(_rmsnorm(xb, final_norm, x.dtype))
    return outs[0].reshape(1, seq, d) if batch == 1 else jnp.stack(outs)
```

```python
import functools
import math

import numpy as np
import jax
import jax.numpy as jnp
from jax import lax
from jax.experimental import pallas as pl
from jax.experimental.pallas import tpu as pltpu

F32 = jnp.float32
BF16 = jnp.bfloat16
HIGHEST = lax.Precision.HIGHEST

LANE = 128
D_MODEL = 4096
RMS_EPS = 1e-6
GN_EPS = 64e-5
NEG_INF = -1e30
VMEM_LIMIT = 56 * 1024 * 1024

GDN_DIM = 128
GDN_HEADS = 12
GDN_WIDTH = GDN_HEADS * GDN_DIM
GDN_CONV = 4
GDN_CHUNK = 64
GDN_GROUP = 2
SWA_DIM = 128
SWA_HEADS = 8
SWA_WIDTH = SWA_HEADS * SWA_DIM
SWA_PATTERNS = ((128, 1), (512, 4), (2048, 16))
SWA_BLK = 128
SWA_TILE = 2048
SWA_GROUP = 4
NUM_BUCKETS = 32
MAX_DISTANCE = 2048
RWKV_DIM = 64
RWKV_HEADS = 24
RWKV_WIDTH = RWKV_HEADS * RWKV_DIM
RWKV_CHUNK = 64
RWKV_GROUP = 1
DECAY_LORA = 128
ICL_LORA = 128
GATE_LORA = 480
GATE_LORA_PAD = 512
D_FF = 11008
FFN_CONV = 3

COL_GDN_Q, COL_GDN_K, COL_GDN_V, COL_GDN_Z, COL_GDN_BA = 0, 12, 24, 36, 48
COL_SWA_Q, COL_SWA_K, COL_SWA_V = 49, 57, 65
COL_WD, COL_AD = 73, 74
COL_GD = 76
COL_R, COL_K, COL_V = 80, 92, 104
PROJ_BLOCKS = 116
PROJ_WIDTH = PROJ_BLOCKS * LANE


def _cparams(sem):
    return pltpu.CompilerParams(dimension_semantics=sem, vmem_limit_bytes=VMEM_LIMIT)


def _silu(x):
    return x * (1.0 / (1.0 + jnp.exp(-x)))


def _sigmoid(x):
    return 1.0 / (1.0 + jnp.exp(-x))


def _softplus(x):
    return jnp.maximum(x, 0.0) + jnp.log(1.0 + jnp.exp(-jnp.abs(x)))


def _split_bf16(x):
    hi = x.astype(BF16)
    return hi, (x - hi.astype(F32)).astype(BF16)


def _dot(a, b, dims=(((1,), (0,)), ((), ())), mode="bf16"):
    if mode == "f32":
        return lax.dot_general(a.astype(F32), b.astype(F32), dims, precision=HIGHEST,
                               preferred_element_type=F32)
    if mode == "x3":
        a_hi, a_lo = _split_bf16(a)
        b_hi, b_lo = _split_bf16(b)
        mm = lambda p, q: lax.dot_general(p, q, dims, preferred_element_type=F32)
        return mm(a_hi, b_hi) + (mm(a_hi, b_lo) + mm(a_lo, b_hi))
    return lax.dot_general(a.astype(BF16), b.astype(BF16), dims, preferred_element_type=F32)


MODE_CUMSUM = "f32"
MODE_INV = "bf16"
MODE_SOLVE = "bf16"
MODE_STATE = "bf16"


_NT = (((1,), (1,)), ((), ()))
_TN = (((0,), (0,)), ((), ()))


def _rmsnorm_body(x_ref, w_ref, o_ref):
    x = x_ref[...]
    ms = jnp.mean(x * x, axis=-1, keepdims=True)
    o_ref[...] = (x * lax.rsqrt(ms + RMS_EPS) * w_ref[...]).astype(o_ref.dtype)


def _rmsnorm(x, w, out_dtype, tm=512):
    s, d = x.shape
    return pl.pallas_call(
        _rmsnorm_body,
        grid=(s // tm,),
        in_specs=[pl.BlockSpec((tm, d), lambda i: (i, 0)),
                  pl.BlockSpec((1, d), lambda i: (0, 0))],
        out_specs=pl.BlockSpec((tm, d), lambda i: (i, 0)),
        out_shape=jax.ShapeDtypeStruct((s, d), out_dtype),
        compiler_params=_cparams(("parallel",)),
        name="rmsnorm",
    )(x, w.reshape(1, d))


def _cast_body(w_ref, o_ref):
    o_ref[...] = w_ref[...].astype(o_ref.dtype)


def _to_bf16(w, layer, rows=256):
    _, r, c = w.shape
    return pl.pallas_call(
        _cast_body,
        grid=(r // rows,),
        in_specs=[pl.BlockSpec((None, rows, c), lambda i: (layer, i, 0))],
        out_specs=pl.BlockSpec((rows, c), lambda i: (i, 0)),
        out_shape=jax.ShapeDtypeStruct((r, c), BF16),
        compiler_params=_cparams(("parallel",)),
        name="to_bf16",
    )(w)


def _w_in_segments():
    gw, sw, rw = GDN_WIDTH, SWA_WIDTH, RWKV_WIDTH
    src = 0
    segs = []
    for width, dst in ((4 * gw, COL_GDN_Q), (2 * GDN_HEADS, COL_GDN_BA), (3 * sw, COL_SWA_Q),
                       (3 * rw, COL_R), (DECAY_LORA, COL_WD), (ICL_LORA, COL_AD), (GATE_LORA, COL_GD)):
        segs.append((src, width, dst * LANE))
        src += width
    return segs, src


def _layout_w_in_body(w_ref, o_ref):
    o_ref[...] = jnp.zeros_like(o_ref)
    for src, width, dst in _w_in_segments()[0]:
        o_ref[:, dst:dst + width] = w_ref[:, src:src + width].astype(o_ref.dtype)


def _layout_w_in_pallas(w, layer, rows=32):
    _, r, c = w.shape
    assert c == _w_in_segments()[1]
    return pl.pallas_call(
        _layout_w_in_body,
        grid=(r // rows,),
        in_specs=[pl.BlockSpec((None, rows, c), lambda i: (layer, i, 0))],
        out_specs=pl.BlockSpec((rows, PROJ_WIDTH), lambda i: (i, 0)),
        out_shape=jax.ShapeDtypeStruct((r, PROJ_WIDTH), BF16),
        compiler_params=_cparams(("parallel",)),
        name="layout_w_in",
    )(w)


def _matmul_body(a_ref, b_ref, o_ref):
    o_ref[...] = jnp.dot(a_ref[...], b_ref[...], preferred_element_type=F32).astype(o_ref.dtype)


def _in_proj(h, w, tm=1024, tn=512):
    s, k = h.shape
    n = w.shape[1]
    return pl.pallas_call(
        _matmul_body,
        grid=(s // tm, n // tn),
        in_specs=[pl.BlockSpec((tm, k), lambda i, j: (i, 0)),
                  pl.BlockSpec((k, tn), lambda i, j: (0, j))],
        out_specs=pl.BlockSpec((tm, tn), lambda i, j: (i, j)),
        out_shape=jax.ShapeDtypeStruct((s, n), F32),
        compiler_params=_cparams(("parallel", "arbitrary")),
        name="in_proj",
    )(h, w)


def _out_proj_body(x_ref, a_ref, b_ref, c_ref, wa_ref, wb_ref, wc_ref, o_ref):
    acc = jnp.dot(a_ref[...], wa_ref[...], preferred_element_type=F32)
    acc += jnp.dot(b_ref[...], wb_ref[...], preferred_element_type=F32)
    acc += jnp.dot(c_ref[...], wc_ref[...], preferred_element_type=F32)
    o_ref[...] = x_ref[...] + acc


def _out_proj(x, oa, ob, oc, wa, wb, wc, tm=1024, tn=512):
    s, d = x.shape
    ka, kb, kc = oa.shape[1], ob.shape[1], oc.shape[1]
    return pl.pallas_call(
        _out_proj_body,
        grid=(s // tm, d // tn),
        in_specs=[pl.BlockSpec((tm, tn), lambda i, j: (i, j)),
                  pl.BlockSpec((tm, ka), lambda i, j: (i, 0)),
                  pl.BlockSpec((tm, kb), lambda i, j: (i, 0)),
                  pl.BlockSpec((tm, kc), lambda i, j: (i, 0)),
                  pl.BlockSpec((ka, tn), lambda i, j: (0, j)),
                  pl.BlockSpec((kb, tn), lambda i, j: (0, j)),
                  pl.BlockSpec((kc, tn), lambda i, j: (0, j))],
        out_specs=pl.BlockSpec((tm, tn), lambda i, j: (i, j)),
        out_shape=jax.ShapeDtypeStruct((s, d), F32),
        compiler_params=_cparams(("parallel", "arbitrary")),
        name="out_proj",
    )(x, oa, ob, oc, wa, wb, wc)


FFN_HALO = 16


def _ffn_up_body(h_ref, halo_ref, wg_ref, wu_ref, cw_ref, cb_ref, o_ref, hbuf, *, tm):
    i = pl.program_id(0)

    @pl.when(pl.program_id(1) == 0)
    def _stage():
        halo = halo_ref[...]
        hbuf[0:FFN_HALO, :] = jnp.where(i == 0, jnp.zeros_like(halo), halo)
        hbuf[FFN_HALO:FFN_HALO + tm, :] = h_ref[...]

    g = jnp.dot(hbuf[...], wg_ref[...], preferred_element_type=F32)
    u = jnp.dot(hbuf[FFN_HALO:FFN_HALO + tm, :], wu_ref[...], preferred_element_type=F32)
    cw = cw_ref[...]
    conv = (cw[0:1, :] * pltpu.roll(g, 2, 0)[FFN_HALO:, :]
            + cw[1:2, :] * pltpu.roll(g, 1, 0)[FFN_HALO:, :]
            + cw[2:3, :] * g[FFN_HALO:, :]) + cb_ref[...]
    o_ref[...] = (_silu(conv) * u).astype(o_ref.dtype)


def _ffn_up(h, wg, wu, cw, cb, tm=1024, tn=256):
    s, k = h.shape
    f = wg.shape[1]
    halo_blocks = tm // FFN_HALO
    return pl.pallas_call(
        functools.partial(_ffn_up_body, tm=tm),
        grid=(s // tm, f // tn),
        in_specs=[pl.BlockSpec((tm, k), lambda i, j: (i, 0)),
                  pl.BlockSpec((FFN_HALO, k), lambda i, j: (jnp.maximum(i * halo_blocks - 1, 0), 0)),
                  pl.BlockSpec((k, tn), lambda i, j: (0, j)),
                  pl.BlockSpec((k, tn), lambda i, j: (0, j)),
                  pl.BlockSpec((FFN_CONV, tn), lambda i, j: (0, j)),
                  pl.BlockSpec((1, tn), lambda i, j: (0, j))],
        out_specs=pl.BlockSpec((tm, tn), lambda i, j: (i, j)),
        out_shape=jax.ShapeDtypeStruct((s, f), BF16),
        scratch_shapes=[pltpu.VMEM((tm + FFN_HALO, k), BF16)],
        compiler_params=_cparams(("parallel", "arbitrary")),
        name="ffn_up",
    )(h, h, wg, wu, cw, cb.reshape(1, f))


def _ffn_down_body(x_ref, a_ref, w_ref, o_ref, acc_ref):
    kk = pl.program_id(2)

    @pl.when(kk == 0)
    def _init():
        acc_ref[...] = x_ref[...]

    acc_ref[...] += jnp.dot(a_ref[...], w_ref[...], preferred_element_type=F32)

    @pl.when(kk == pl.num_programs(2) - 1)
    def _done():
        o_ref[...] = acc_ref[...]


def _ffn_down(x, act, w, tm=512, tn=512, ksplit=1):
    s, d = x.shape
    f = act.shape[1]
    tk = f // ksplit
    return pl.pallas_call(
        _ffn_down_body,
        grid=(s // tm, d // tn, ksplit),
        in_specs=[pl.BlockSpec((tm, tn), lambda i, j, kk: (i, j)),
                  pl.BlockSpec((tm, tk), lambda i, j, kk: (i, kk)),
                  pl.BlockSpec((tk, tn), lambda i, j, kk: (kk, j))],
        out_specs=pl.BlockSpec((tm, tn), lambda i, j, kk: (i, j)),
        out_shape=jax.ShapeDtypeStruct((s, d), F32),
        scratch_shapes=[pltpu.VMEM((tm, tn), F32)],
        compiler_params=_cparams(("parallel", "arbitrary", "arbitrary")),
        name="ffn_down",
    )(x, act, w)


def _unit_lower_inverses(n_mats, size, out):
    row = lax.broadcasted_iota(jnp.int32, (size, size), 0)
    col = lax.broadcasted_iota(jnp.int32, (size, size), 1)
    eye = jnp.where(row == col, 1.0, 0.0).astype(F32)
    invs = [eye + n for n in n_mats]
    powers = [_dot(n, n, mode=MODE_INV) for n in n_mats]
    yield
    steps = int(math.log2(size)) - 1
    for s in range(steps):
        last = s == steps - 1
        for i, (inv, p) in enumerate(zip(invs, powers)):
            if last:
                invs[i] = inv + _dot(inv, p, mode=MODE_INV)
            else:
                both = _dot(jnp.concatenate([inv, p], axis=0), p, mode=MODE_INV)
                invs[i] = inv + both[0:size]
                powers[i] = both[size:2 * size]
        yield
    out.extend(invs)


def _alternate(*stage_generators):
    live = list(stage_generators)
    while live:
        for gen in list(live):
            try:
                next(gen)
            except StopIteration:
                live.remove(gen)


def _gdn_body(alog_ref, dtb_ref, q_ref, k_ref, v_ref, z_ref, ba_ref, cq_ref, ck_ref, cv_ref,
              nw_ref, o_ref, xbuf, state_ref, *, tile):
    c = GDN_CHUNK
    hd = GDN_DIM
    width = GDN_GROUP * hd
    n_chunks = tile // c
    t = pl.program_id(1)

    @pl.when(t == 0)
    def _reset():
        state_ref[...] = jnp.zeros_like(state_ref)
        xbuf[:, 0:8, :] = jnp.zeros((3, 8, width), F32)

    convs = []
    for idx, (src, cw_ref) in enumerate(((q_ref, cq_ref), (k_ref, ck_ref), (v_ref, cv_ref))):
        xbuf[idx, 8:8 + tile, :] = src[...]
        cw = cw_ref[...]
        acc = cw[0:1, :] * xbuf[idx, 5:5 + tile, :]
        acc += cw[1:2, :] * xbuf[idx, 6:6 + tile, :]
        acc += cw[2:3, :] * xbuf[idx, 7:7 + tile, :]
        acc += cw[3:4, :] * xbuf[idx, 8:8 + tile, :]
        convs.append(_silu(acc))
        xbuf[idx, 0:8, :] = xbuf[idx, tile:tile + 8, :]
    q_all, k_all, v_all = convs

    ba = ba_ref[...]
    lane = lax.broadcasted_iota(jnp.int32, (1, LANE), 1)
    ri = lax.broadcasted_iota(jnp.int32, (c, c), 0)
    ci = lax.broadcasted_iota(jnp.int32, (c, c), 1)
    lower = jnp.where(ci <= ri, 1.0, 0.0).astype(F32)
    upper = jnp.where(ri <= ci, 1.0, 0.0).astype(F32)
    r2 = lax.broadcasted_iota(jnp.int32, (2 * c, c), 0)
    c2 = lax.broadcasted_iota(jnp.int32, (2 * c, c), 1)
    band = c2 <= jnp.where(r2 < c, r2 - 1, r2 - c)
    nw = nw_ref[...]

    units = []
    for j in range(GDN_GROUP):
        h = pl.program_id(0) * GDN_GROUP + j
        cols = slice(j * hd, (j + 1) * hd)
        q, k, v = q_all[:, cols], k_all[:, cols], v_all[:, cols]
        q = q * lax.rsqrt(jnp.sum(q * q, axis=-1, keepdims=True) + 1e-6) * (hd ** -0.5)
        k = k * lax.rsqrt(jnp.sum(k * k, axis=-1, keepdims=True) + 1e-6)
        b_col = jnp.sum(jnp.where(lane == h, ba, 0.0), axis=-1, keepdims=True)
        a_col = jnp.sum(jnp.where(lane == h + GDN_HEADS, ba, 0.0), axis=-1, keepdims=True)
        beta = _sigmoid(b_col)
        g = -jnp.exp(alog_ref[h]) * _softplus(a_col + dtb_ref[h])
        g_b = g * jnp.ones((1, hd), F32)
        g_col = jnp.concatenate([_dot(lower, g_b[n * c:(n + 1) * c], mode=MODE_CUMSUM)
                                 for n in range(n_chunks)], axis=0)
        g_row = jnp.concatenate([_dot(jnp.ones((8, c), F32), g_b[n * c:(n + 1) * c, 0:c] * upper,
                                      mode=MODE_CUMSUM) for n in range(n_chunks)], axis=1)
        exp_g = jnp.exp(g_col)
        kb = k * beta
        vb = v * beta
        kbe = kb * exp_g
        qg = q * exp_g
        for n in range(n_chunks):
            rows = slice(n * c, (n + 1) * c)
            g_last = g_col[n * c + c - 1:n * c + c, :]
            gamma = jnp.exp(jnp.minimum(g_col[rows, 0:c] - g_row[0:1, n * c:(n + 1) * c], 0.0))
            units.append(dict(j=j, n=n, k=k[rows], kq=jnp.concatenate([kb[rows], q[rows]], axis=0),
                              rhs=jnp.concatenate([vb[rows], kbe[rows]], axis=1), qg=qg[rows],
                              kd=k[rows] * jnp.exp(g_last - g_col[rows]), decay=jnp.exp(g_last),
                              gamma2=jnp.concatenate([gamma, gamma], axis=0)))
    states = [state_ref[j] for j in range(GDN_GROUP)]
    outs = [[None] * n_chunks for _ in range(GDN_GROUP)]

    def prepare(group):
        for un in group:
            un["scores"] = jnp.where(band, _dot(un["kq"], un["k"], _NT) * un["gamma2"], 0.0)
        yield
        t_invs = []
        yield from _unit_lower_inverses([-un["scores"][0:c] for un in group], c, t_invs)
        for un, t_inv in zip(group, t_invs):
            un["uw"] = _dot(t_inv, un["rhs"], mode=MODE_SOLVE)
        yield
        for un in group:
            un["trans"] = _dot(un["kd"], un["uw"][:, hd:2 * hd], _TN, mode=MODE_STATE)
            un["const"] = _dot(un["kd"], un["uw"][:, 0:hd], _TN, mode=MODE_STATE)
        yield

    def chain(group):
        for n in sorted({un["n"] for un in group}):
            for un in group:
                if un["n"] == n:
                    j = un["j"]
                    un["s0"] = states[j]
                    states[j] = (states[j] * un["decay"] - _dot(un["trans"], states[j], mode=MODE_STATE)
                                 + un["const"])
            yield

    def outputs(group, chunks_per_stage):
        chunk_ids = sorted({un["n"] for un in group})
        for i in range(0, len(chunk_ids), chunks_per_stage):
            part = [un for un in group if un["n"] in chunk_ids[i:i + chunks_per_stage]]
            for un in part:
                un["ws"] = _dot(jnp.concatenate([un["uw"][:, hd:2 * hd], un["qg"]], axis=0), un["s0"],
                                mode=MODE_STATE)
            yield
            for un in part:
                v_new = un["uw"][:, 0:hd] - un["ws"][0:c]
                outs[un["j"]][un["n"]] = (un["ws"][c:2 * c]
                                          + _dot(un["scores"][c:2 * c], v_new, mode=MODE_STATE))
            yield

    _alternate(prepare(units))
    _alternate(chain(units))
    _alternate(outputs(units, n_chunks))
    for j in range(GDN_GROUP):
        state_ref[j] = states[j]
    for j in range(GDN_GROUP):
        cols = slice(j * hd, (j + 1) * hd)
        o = jnp.concatenate(outs[j], axis=0)
        o = o * lax.rsqrt(jnp.mean(o * o, axis=-1, keepdims=True) + RMS_EPS) * nw
        o_ref[:, cols] = (o * _silu(z_ref[:, cols])).astype(o_ref.dtype)


def _gdn(proj, conv_w, a_log, dt_bias, norm_w, tile=512):
    s = proj.shape[0]
    width = GDN_GROUP * GDN_DIM
    per = LANE // GDN_DIM * GDN_GROUP
    blk = lambda col: pl.BlockSpec((tile, width), lambda h, t: (t, col // per + h))
    cblk = lambda col: pl.BlockSpec((GDN_CONV, width), lambda h, t: (0, col // per + h))
    smem = pl.BlockSpec(memory_space=pltpu.SMEM)
    return pl.pallas_call(
        functools.partial(_gdn_body, tile=tile),
        grid=(GDN_HEADS // GDN_GROUP, s // tile),
        in_specs=[smem, smem,
                  blk(COL_GDN_Q), blk(COL_GDN_K), blk(COL_GDN_V), blk(COL_GDN_Z),
                  pl.BlockSpec((tile, LANE), lambda h, t: (t, COL_GDN_BA)),
                  cblk(0), cblk(GDN_HEADS), cblk(2 * GDN_HEADS),
                  pl.BlockSpec((1, GDN_DIM), lambda h, t: (0, 0))],
        out_specs=pl.BlockSpec((tile, width), lambda h, t: (t, h)),
        out_shape=jax.ShapeDtypeStruct((s, GDN_WIDTH), BF16),
        scratch_shapes=[pltpu.VMEM((3, tile + 8, width), F32),
                        pltpu.VMEM((GDN_GROUP, GDN_DIM, GDN_DIM), F32)],
        compiler_params=_cparams(("parallel", "arbitrary")),
        name="gdn",
    )(a_log, dt_bias, proj, proj, proj, proj, proj, conv_w, conv_w, conv_w, norm_w.reshape(1, GDN_DIM))


def _t5_bucket_table():
    exact = NUM_BUCKETS // 2
    i = np.arange(SWA_BLK)[:, None]
    j = np.arange(2 * SWA_BLK)[None, :]
    steps = np.maximum(i + SWA_BLK - j, 0)
    tables = []
    for _, dilation in SWA_PATTERNS:
        dist = steps * dilation
        d = np.maximum(dist, 1).astype(np.float32)
        ratio = (np.log(d / np.float32(exact)) / np.float32(math.log(MAX_DISTANCE / exact))
                 * np.float32(NUM_BUCKETS - exact)).astype(np.float32)
        log_b = exact + ratio.astype(np.int32)
        tables.append(np.where(dist < exact, dist, np.minimum(log_b, NUM_BUCKETS - 1)))
    return np.stack(tables).astype(np.int32)


def _swa_body(rb_ref, bkt_ref, q_ref, kc_ref, kp_ref, vc_ref, vp_ref, o_ref,
              bias_ref, kbuf, vbuf, o_scr, m_scr, l_scr, *, tile):
    h = pl.program_id(0)
    t = pl.program_id(1)
    blk = SWA_BLK

    @pl.when(t == 0)
    def _bias():
        for p in range(len(SWA_PATTERNS)):
            bkt = bkt_ref[p]
            bias = jnp.zeros((blk, 2 * blk), F32)
            for b in range(NUM_BUCKETS):
                bias = jnp.where(bkt == b, rb_ref[b, h], bias)
            bias_ref[p] = bias

    kbuf[0:tile, :] = kp_ref[...]
    kbuf[tile:2 * tile, :] = kc_ref[...]
    vbuf[0:tile, :] = vp_ref[...]
    vbuf[tile:2 * tile, :] = vc_ref[...]

    qi = lax.broadcasted_iota(jnp.int32, (blk, 2 * blk), 0)
    kj = lax.broadcasted_iota(jnp.int32, (blk, 2 * blk), 1)
    in_band = (kj >= qi) & (kj <= qi + blk)
    scale = SWA_DIM ** -0.5

    band_first = in_band & (kj >= jnp.where(t == 0, blk, 0))
    ones_row = jnp.ones((1, SWA_DIM), F32)

    for p, (window, dil) in enumerate(SWA_PATTERNS):
        span = blk * dil
        bias = bias_ref[p]
        blocks = [(res + n * span, n) for n in range(tile // span) for res in range(dil)]
        for g0 in range(0, len(blocks), SWA_GROUP):
            group = blocks[g0:g0 + SWA_GROUP]
            scores = []
            for start, n in group:
                qb = q_ref[pl.ds(start, blk, stride=dil), :] * scale
                kw = kbuf[pl.ds(tile + start - span, 2 * blk, stride=dil), :]
                sc = _dot(qb, kw, _NT) + bias
                scores.append(jnp.where(band_first if n == 0 else in_band, sc, NEG_INF))
            probs = []
            for sc in scores:
                m = jnp.max(sc, axis=-1, keepdims=True)
                pe = jnp.exp(sc - m)
                probs.append((pe, m, jnp.sum(pe, axis=-1, keepdims=True)))
            for (start, n), (pe, m, l) in zip(group, probs):
                vw = vbuf[pl.ds(tile + start - span, 2 * blk, stride=dil), :]
                o_scr[p, pl.ds(start, blk, stride=dil), :] = _dot(pe, vw)
                m_scr[p, pl.ds(start, blk, stride=dil), :] = m * ones_row
                l_scr[p, pl.ds(start, blk, stride=dil), :] = l * ones_row

    m_max = jnp.maximum(jnp.maximum(m_scr[0], m_scr[1]), m_scr[2])
    num = jnp.zeros((tile, SWA_DIM), F32)
    den = jnp.zeros((tile, SWA_DIM), F32)
    for p in range(len(SWA_PATTERNS)):
        sc = jnp.exp(m_scr[p] - m_max)
        num += o_scr[p] * sc
        den += l_scr[p] * sc
    o_ref[...] = (num / den).astype(o_ref.dtype)


def _swa(proj, rel_bias, tile=SWA_TILE):
    s = proj.shape[0]
    n_pat = len(SWA_PATTERNS)
    bkt = jnp.asarray(_t5_bucket_table())
    cur = lambda col: pl.BlockSpec((tile, LANE), lambda h, t: (t, col + h))
    prev = lambda col: pl.BlockSpec((tile, LANE), lambda h, t: (jnp.maximum(t - 1, 0), col + h))
    return pl.pallas_call(
        functools.partial(_swa_body, tile=tile),
        grid=(SWA_HEADS, s // tile),
        in_specs=[pl.BlockSpec(memory_space=pltpu.SMEM),
                  pl.BlockSpec((n_pat, SWA_BLK, 2 * SWA_BLK), lambda h, t: (0, 0, 0)),
                  cur(COL_SWA_Q), cur(COL_SWA_K), prev(COL_SWA_K), cur(COL_SWA_V), prev(COL_SWA_V)],
        out_specs=pl.BlockSpec((tile, LANE), lambda h, t: (t, h)),
        out_shape=jax.ShapeDtypeStruct((s, SWA_WIDTH), BF16),
        scratch_shapes=[pltpu.VMEM((n_pat, SWA_BLK, 2 * SWA_BLK), F32),
                        pltpu.VMEM((2 * tile, SWA_DIM), F32),
                        pltpu.VMEM((2 * tile, SWA_DIM), F32),
                        pltpu.VMEM((n_pat, tile, SWA_DIM), F32),
                        pltpu.VMEM((n_pat, tile, SWA_DIM), F32),
                        pltpu.VMEM((n_pat, tile, SWA_DIM), F32)],
        compiler_params=_cparams(("parallel", "arbitrary")),
        name="swa",
    )(rel_bias, bkt, proj, proj, proj, proj, proj)


PRM_W0, PRM_A0, PRM_KK, PRM_KA, PRM_RK, PRM_LNW, PRM_LNB, PRM_MUR, PRM_MUK, PRM_MUV = range(10)
PRM_ROWS = 16


LORA_IN_WIDTH = DECAY_LORA + ICL_LORA + GATE_LORA_PAD


def _rwkv_lora_in_body(wd_ref, ad_ref, gd_ref, mul_ref, mug_ref, o_ref, xb, xg, *, tile):
    @pl.when(pl.program_id(0) == 0)
    def _reset():
        xb[:, 0:8, :] = jnp.zeros((2, 8, LANE), F32)
        xg[0:8, :] = jnp.zeros((8, GATE_LORA_PAD), F32)

    def shifted(buf, src, mu):
        buf[8:8 + tile, :] = src[...]
        cur = buf[8:8 + tile, :]
        prev = buf[7:7 + tile, :]
        buf[0:8, :] = buf[tile:tile + 8, :]
        return cur + mu * (prev - cur)

    wd = shifted(xb.at[0], wd_ref, mul_ref[0:1, :])
    ad = shifted(xb.at[1], ad_ref, mul_ref[1:2, :])
    gd = shifted(xg, gd_ref, mug_ref[...])
    o_ref[:, 0:DECAY_LORA] = jnp.tanh(wd).astype(o_ref.dtype)
    o_ref[:, DECAY_LORA:DECAY_LORA + ICL_LORA] = ad.astype(o_ref.dtype)
    o_ref[:, DECAY_LORA + ICL_LORA:] = _sigmoid(gd).astype(o_ref.dtype)


def _rwkv_lora_in(proj, mu_lora, mu_gate, tile=1024):
    s = proj.shape[0]
    fixed = lambda col: pl.BlockSpec((tile, LANE), lambda t: (t, col))
    return pl.pallas_call(
        functools.partial(_rwkv_lora_in_body, tile=tile),
        grid=(s // tile,),
        in_specs=[fixed(COL_WD), fixed(COL_AD),
                  pl.BlockSpec((tile, GATE_LORA_PAD), lambda t: (t, COL_GD * LANE // GATE_LORA_PAD)),
                  pl.BlockSpec((8, LANE), lambda t: (0, 0)),
                  pl.BlockSpec((1, GATE_LORA_PAD), lambda t: (0, 0))],
        out_specs=pl.BlockSpec((tile, LORA_IN_WIDTH), lambda t: (t, 0)),
        out_shape=jax.ShapeDtypeStruct((s, LORA_IN_WIDTH), BF16),
        scratch_shapes=[pltpu.VMEM((2, tile + 8, LANE), F32),
                        pltpu.VMEM((tile + 8, GATE_LORA_PAD), F32)],
        compiler_params=_cparams(("arbitrary",)),
        name="rwkv_lora_in",
    )(proj, proj, proj, mu_lora, mu_gate)


def _rwkv_body(r_ref, k_ref, v_ref, lora_ref, prm_ref, wup_ref, aup_ref, gup_ref, o_ref, xb, state_ref,
               *, tile):
    c = RWKV_CHUNK
    nd = RWKV_DIM
    t = pl.program_id(1)

    @pl.when(t == 0)
    def _reset():
        state_ref[...] = jnp.zeros_like(state_ref)
        xb[:, 0:8, :] = jnp.zeros((3, 8, RWKV_GROUP * LANE), F32)

    prm = prm_ref[...]
    row = lambda i: prm[i:i + 1, :]

    def shifted(idx, src, mu):
        xb[idx, 8:8 + tile, :] = src[...]
        cur = xb[idx, 8:8 + tile, :]
        prev = xb[idx, 7:7 + tile, :]
        xb[idx, 0:8, :] = xb[idx, tile:tile + 8, :]
        return cur + mu * (prev - cur)

    r = shifted(0, r_ref, row(PRM_MUR))
    k = shifted(1, k_ref, row(PRM_MUK))
    v = shifted(2, v_ref, row(PRM_MUV))

    lora = lora_ref[...]
    w_log = -_softplus(-(row(PRM_W0) + _dot(lora[:, 0:DECAY_LORA], wup_ref[...]))) - 0.5
    log_w = -jnp.exp(w_log)
    a_gate = _sigmoid(row(PRM_A0) + _dot(lora[:, DECAY_LORA:DECAY_LORA + ICL_LORA], aup_ref[...]))
    g_gate = _dot(lora[:, DECAY_LORA + ICL_LORA:], gup_ref[...])
    kx = k * row(PRM_KK)
    k_mod = k * (1.0 + (a_gate - 1.0) * row(PRM_KA))

    ri = lax.broadcasted_iota(jnp.int32, (c, c), 0)
    ci = lax.broadcasted_iota(jnp.int32, (c, c), 1)
    lower = jnp.where(ci <= ri, 1.0, 0.0).astype(F32)
    cum = jnp.concatenate([_dot(lower, log_w[n * c:(n + 1) * c], mode=MODE_CUMSUM)
                           for n in range(tile // c)], axis=0)

    r2 = lax.broadcasted_iota(jnp.int32, (2 * c, 2 * c), 0)
    c2 = lax.broadcasted_iota(jnp.int32, (2 * c, 2 * c), 1)
    col_in = jnp.where(c2 < c, c2, c2 - c)
    band = col_in <= jnp.where(r2 < c, r2 - 1, r2 - c)

    width = RWKV_GROUP * LANE
    n_heads = width // nd
    n_chunks = tile // c
    lane = lax.broadcasted_iota(jnp.int32, (1, LANE), 1)
    head_masks = [(lane // nd) == hh for hh in range(LANE // nd)]

    def head_sum(x):
        blocks = []
        for b in range(RWKV_GROUP):
            xs = x[:, b * LANE:(b + 1) * LANE]
            out = jnp.zeros_like(xs)
            for msk in head_masks:
                out = jnp.where(msk, jnp.sum(jnp.where(msk, xs, 0.0), axis=-1, keepdims=True), out)
            blocks.append(out)
        return blocks[0] if RWKV_GROUP == 1 else jnp.concatenate(blocks, axis=1)

    kk = kx * lax.rsqrt(head_sum(kx * kx) + 1e-12)
    b_vec = kk * a_gate
    c_last = jnp.concatenate(
        [jnp.broadcast_to(cum[n * c + c - 1:n * c + c, :], (c, width)) for n in range(n_chunks)], axis=0)
    p_out = jnp.exp(-cum)
    a_t = -kk * jnp.exp(cum - log_w)
    r_t = r * jnp.exp(cum)
    k_t = k_mod * p_out
    b_t = b_vec * p_out
    decay_end = jnp.exp(c_last - cum)
    k_e = k_mod * decay_end
    b_e = b_vec * decay_end
    decay_all = jnp.exp(c_last)

    units = []
    for hh in range(n_heads):
        lanes = slice(hh * nd, (hh + 1) * nd)
        for n in range(n_chunks):
            rows = slice(n * c, (n + 1) * c)
            units.append(dict(hh=hh, n=n, a=a_t[rows, lanes], r=r_t[rows, lanes], v=v[rows, lanes],
                              ar=jnp.concatenate([a_t[rows, lanes], r_t[rows, lanes]], axis=0),
                              bk=jnp.concatenate([b_t[rows, lanes], k_t[rows, lanes]], axis=0),
                              bk_e=jnp.concatenate([b_e[rows, lanes], k_e[rows, lanes]], axis=0),
                              decay=decay_all[n * c:n * c + 1, lanes]))
    states = [state_ref[hh] for hh in range(n_heads)]
    ys = [[None] * n_chunks for _ in range(n_heads)]

    def prepare(group):
        for un in group:
            un["mm"] = jnp.where(band, _dot(un["ar"], un["bk"], _NT), 0.0)
        yield
        t_invs = []
        yield from _unit_lower_inverses([un["mm"][0:c, 0:c] for un in group], c, t_invs)
        for un in group:
            un["mv"] = _dot(un["mm"][0:c, c:2 * c], un["v"], mode=MODE_SOLVE)
        yield
        for un, t_inv in zip(group, t_invs):
            un["wu"] = _dot(t_inv, jnp.concatenate([un["a"], un["mv"]], axis=1), mode=MODE_SOLVE)
        yield
        for un in group:
            un["trans"] = _dot(un["wu"][:, 0:nd], un["bk_e"][0:c], _TN, mode=MODE_STATE)
            un["const"] = _dot(jnp.concatenate([un["wu"][:, nd:2 * nd], un["v"]], axis=0), un["bk_e"],
                               _TN, mode=MODE_STATE)
        yield

    def chain(group):
        for n in sorted({un["n"] for un in group}):
            for un in group:
                if un["n"] == n:
                    hh = un["hh"]
                    un["s0"] = states[hh]
                    states[hh] = (states[hh] * un["decay"]
                                  + _dot(states[hh], un["trans"], mode=MODE_STATE) + un["const"])
            yield

    def outputs(group, chunks_per_stage):
        chunk_ids = sorted({un["n"] for un in group})
        for i in range(0, len(chunk_ids), chunks_per_stage):
            part = [un for un in group if un["n"] in chunk_ids[i:i + chunks_per_stage]]
            for un in part:
                un["sr"] = _dot(jnp.concatenate([un["wu"][:, 0:nd], un["r"]], axis=0), un["s0"], _NT,
                                mode=MODE_STATE)
            yield
            for un in part:
                u = un["sr"][0:c] + un["wu"][:, nd:2 * nd]
                uv = jnp.concatenate([u, un["v"]], axis=0)
                ys[un["hh"]][un["n"]] = un["sr"][c:2 * c] + _dot(un["mm"][c:2 * c], uv, mode=MODE_STATE)
            yield

    _alternate(prepare(units))
    _alternate(chain(units))
    _alternate(outputs(units, n_chunks))
    for hh in range(n_heads):
        state_ref[hh] = states[hh]
    y = jnp.concatenate([jnp.concatenate(ys[hh], axis=0) for hh in range(n_heads)], axis=1)
    mean = head_sum(y) * (1.0 / nd)
    var = head_sum(jnp.square(y - mean)) * (1.0 / nd)
    y_n = (y - mean) * lax.rsqrt(var + GN_EPS) * row(PRM_LNW) + row(PRM_LNB)
    bonus = head_sum(r * k_mod * row(PRM_RK)) * v
    o_ref[...] = ((y_n + bonus) * g_gate).astype(o_ref.dtype)


def _rwkv(proj, prm, mu_lora, mu_gate, w_up, a_up, g_up, tile=512):
    s = proj.shape[0]
    width = RWKV_GROUP * LANE
    groups = RWKV_WIDTH // width
    lora_in = _rwkv_lora_in(proj, mu_lora, mu_gate)
    blk = lambda col: pl.BlockSpec((tile, width), lambda h, t: (t, col // RWKV_GROUP + h))
    return pl.pallas_call(
        functools.partial(_rwkv_body, tile=tile),
        grid=(groups, s // tile),
        in_specs=[blk(COL_R), blk(COL_K), blk(COL_V),
                  pl.BlockSpec((tile, LORA_IN_WIDTH), lambda h, t: (t, 0)),
                  pl.BlockSpec((PRM_ROWS, width), lambda h, t: (0, h)),
                  pl.BlockSpec((DECAY_LORA, width), lambda h, t: (0, h)),
                  pl.BlockSpec((ICL_LORA, width), lambda h, t: (0, h)),
                  pl.BlockSpec((GATE_LORA_PAD, width), lambda h, t: (0, h))],
        out_specs=pl.BlockSpec((tile, width), lambda h, t: (t, h)),
        out_shape=jax.ShapeDtypeStruct((s, RWKV_WIDTH), BF16),
        scratch_shapes=[pltpu.VMEM((3, tile + 8, width), F32),
                        pltpu.VMEM((width // RWKV_DIM, RWKV_DIM, RWKV_DIM), F32)],
        compiler_params=_cparams(("parallel", "arbitrary")),
        name="rwkv7",
    )(proj, proj, proj, lora_in, prm, w_up, a_up, g_up)


def _layout_rwkv_params(mu, w0, a0, k_k, k_a, r_k, ln_w, ln_b):
    rw = RWKV_WIDTH
    mu_r, mu_k, mu_v = mu[0:rw], mu[rw:2 * rw], mu[2 * rw:3 * rw]
    o = 3 * rw
    mu_wd = mu[o:o + DECAY_LORA]; o += DECAY_LORA
    mu_ad = mu[o:o + ICL_LORA]; o += ICL_LORA
    mu_gd = mu[o:o + GATE_LORA]
    rows = [w0, a0, k_k, k_a, r_k.reshape(rw), ln_w, ln_b, mu_r, mu_k, mu_v]
    prm = jnp.stack(rows + [jnp.zeros((rw,), F32)] * (PRM_ROWS - len(rows))).astype(F32)
    mu_lora = jnp.stack([mu_wd, mu_ad] + [jnp.zeros((LANE,), F32)] * 6).astype(F32)
    mu_gate = jnp.concatenate([mu_gd, jnp.zeros((GATE_LORA_PAD - GATE_LORA,), F32)]).reshape(1, GATE_LORA_PAD)
    return prm, mu_lora, mu_gate


def kernel(x, attn_norm, w_in, gdn_conv, gdn_a_log, gdn_dt_bias, gdn_norm, rwkv_mu, rwkv_w0, rwkv_w_up, rwkv_a0, rwkv_a_up, rwkv_g_up, rwkv_k_k, rwkv_k_a, rwkv_r_k, rwkv_ln_w, rwkv_ln_b, w_out, ffn_norm, w_ffn_gate, w_ffn_up, ffn_conv, ffn_conv_b, w_ffn_down, rel_bias, final_norm):
    batch, seq, d = x.shape
    depth = w_in.shape[0]
    outs = []
    w_in_bf16 = w_in.astype(BF16)
    rows_in = [x.reshape(seq, d)] if batch == 1 else [x[b] for b in range(batch)]
    for xb in rows_in:
        for l in range(depth):
            h = _rmsnorm(xb, attn_norm[l], BF16)
            proj = _in_proj(h, _layout_w_in_pallas(w_in_bf16, l))
            o_a = _gdn(proj, gdn_conv[l], gdn_a_log[l], gdn_dt_bias[l], gdn_norm[l])
            o_b = _swa(proj, rel_bias)
            prm, mu_lora, mu_gate = _layout_rwkv_params(
                rwkv_mu[l], rwkv_w0[l], rwkv_a0[l], rwkv_k_k[l], rwkv_k_a[l], rwkv_r_k[l],
                rwkv_ln_w[l], rwkv_ln_b[l])
            g_up = jnp.concatenate(
                [rwkv_g_up[l], jnp.zeros((GATE_LORA_PAD - GATE_LORA, RWKV_WIDTH), F32)], axis=0)
            o_c = _rwkv(proj, prm, mu_lora, mu_gate, rwkv_w_up[l].astype(BF16),
                        rwkv_a_up[l].astype(BF16), g_up.astype(BF16))
            wo = _to_bf16(w_out, l)
            xb = _out_proj(xb, o_a, o_b, o_c, wo[0:GDN_WIDTH], wo[GDN_WIDTH:GDN_WIDTH + SWA_WIDTH],
                           wo[GDN_WIDTH + SWA_WIDTH:])
            h = _rmsnorm(xb, ffn_norm[l], BF16)
            act = _ffn_up(h, _to_bf16(w_ffn_gate, l), _to_bf16(w_ffn_up, l), ffn_conv[l], ffn_conv_b[l])
            xb = _ffn_down(xb, act, _to_bf16(w_ffn_down, l))
        outs.append(_rmsnorm(xb, final_norm, x.dtype))
    return outs[0].reshape(1, seq, d) if batch == 1 else jnp.stack(outs)
```

```python
import functools
import math

import numpy as np
import jax
import jax.numpy as jnp
from jax import lax
from jax.experimental import pallas as pl
from jax.experimental.pallas import tpu as pltpu

F32 = jnp.float32
BF16 = jnp.bfloat16
HIGHEST = lax.Precision.HIGHEST

LANE = 128
D_MODEL = 4096
RMS_EPS = 1e-6
GN_EPS = 64e-5
NEG_INF = -1e30
VMEM_LIMIT = 56 * 1024 * 1024

GDN_DIM = 128
GDN_HEADS = 12
GDN_WIDTH = GDN_HEADS * GDN_DIM
GDN_CONV = 4
GDN_CHUNK = 64
GDN_GROUP = 2
SWA_DIM = 128
SWA_HEADS = 8
SWA_WIDTH = SWA_HEADS * SWA_DIM
SWA_PATTERNS = ((128, 1), (512, 4), (2048, 16))
SWA_BLK = 128
SWA_TILE = 2048
SWA_GROUP = 4
NUM_BUCKETS = 32
MAX_DISTANCE = 2048
RWKV_DIM = 64
RWKV_HEADS = 24
RWKV_WIDTH = RWKV_HEADS * RWKV_DIM
RWKV_CHUNK = 64
RWKV_GROUP = 1
DECAY_LORA = 128
ICL_LORA = 128
GATE_LORA = 480
GATE_LORA_PAD = 512
D_FF = 11008
FFN_CONV = 3

COL_GDN_Q, COL_GDN_K, COL_GDN_V, COL_GDN_Z, COL_GDN_BA = 0, 12, 24, 36, 48
COL_SWA_Q, COL_SWA_K, COL_SWA_V = 49, 57, 65
COL_WD, COL_AD = 73, 74
COL_GD = 76
COL_R, COL_K, COL_V = 80, 92, 104
PROJ_BLOCKS = 116
PROJ_WIDTH = PROJ_BLOCKS * LANE


def _cparams(sem):
    return pltpu.CompilerParams(dimension_semantics=sem, vmem_limit_bytes=VMEM_LIMIT)


def _silu(x):
    return x * (1.0 / (1.0 + jnp.exp(-x)))


def _sigmoid(x):
    return 1.0 / (1.0 + jnp.exp(-x))


def _softplus(x):
    return jnp.maximum(x, 0.0) + jnp.log(1.0 + jnp.exp(-jnp.abs(x)))


def _split_bf16(x):
    hi = x.astype(BF16)
    return hi, (x - hi.astype(F32)).astype(BF16)


def _dot(a, b, dims=(((1,), (0,)), ((), ())), mode="bf16"):
    if mode == "f32":
        return lax.dot_general(a.astype(F32), b.astype(F32), dims, precision=HIGHEST,
                               preferred_element_type=F32)
    if mode == "x3":
        a_hi, a_lo = _split_bf16(a)
        b_hi, b_lo = _split_bf16(b)
        mm = lambda p, q: lax.dot_general(p, q, dims, preferred_element_type=F32)
        return mm(a_hi, b_hi) + (mm(a_hi, b_lo) + mm(a_lo, b_hi))
    return lax.dot_general(a.astype(BF16), b.astype(BF16), dims, preferred_element_type=F32)


MODE_CUMSUM = "f32"
MODE_INV = "bf16"
MODE_SOLVE = "bf16"
MODE_STATE = "bf16"


_NT = (((1,), (1,)), ((), ()))
_TN = (((0,), (0,)), ((), ()))


def _rmsnorm_body(x_ref, w_ref, o_ref):
    x = x_ref[...]
    ms = jnp.mean(x * x, axis=-1, keepdims=True)
    o_ref[...] = (x * lax.rsqrt(ms + RMS_EPS) * w_ref[...]).astype(o_ref.dtype)


def _rmsnorm(x, w, out_dtype, tm=512):
    s, d = x.shape
    return pl.pallas_call(
        _rmsnorm_body,
        grid=(s // tm,),
        in_specs=[pl.BlockSpec((tm, d), lambda i: (i, 0)),
                  pl.BlockSpec((1, d), lambda i: (0, 0))],
        out_specs=pl.BlockSpec((tm, d), lambda i: (i, 0)),
        out_shape=jax.ShapeDtypeStruct((s, d), out_dtype),
        compiler_params=_cparams(("parallel",)),
        name="rmsnorm",
    )(x, w.reshape(1, d))


def _cast_body(w_ref, o_ref):
    o_ref[...] = w_ref[...].astype(o_ref.dtype)


def _to_bf16(w, layer, rows=256):
    _, r, c = w.shape
    return pl.pallas_call(
        _cast_body,
        grid=(r // rows,),
        in_specs=[pl.BlockSpec((None, rows, c), lambda i: (layer, i, 0))],
        out_specs=pl.BlockSpec((rows, c), lambda i: (i, 0)),
        out_shape=jax.ShapeDtypeStruct((r, c), BF16),
        compiler_params=_cparams(("parallel",)),
        name="to_bf16",
    )(w)


def _w_in_segments():
    gw, sw, rw = GDN_WIDTH, SWA_WIDTH, RWKV_WIDTH
    src = 0
    segs = []
    for width, dst in ((4 * gw, COL_GDN_Q), (2 * GDN_HEADS, COL_GDN_BA), (3 * sw, COL_SWA_Q),
                       (3 * rw, COL_R), (DECAY_LORA, COL_WD), (ICL_LORA, COL_AD), (GATE_LORA, COL_GD)):
        segs.append((src, width, dst * LANE))
        src += width
    return segs, src


def _layout_w_in_body(w_ref, o_ref):
    o_ref[...] = jnp.zeros_like(o_ref)
    for src, width, dst in _w_in_segments()[0]:
        o_ref[:, dst:dst + width] = w_ref[:, src:src + width].astype(o_ref.dtype)


def _layout_w_in_pallas(w, layer, rows=32):
    _, r, c = w.shape
    assert c == _w_in_segments()[1]
    return pl.pallas_call(
        _layout_w_in_body,
        grid=(r // rows,),
        in_specs=[pl.BlockSpec((None, rows, c), lambda i: (layer, i, 0))],
        out_specs=pl.BlockSpec((rows, PROJ_WIDTH), lambda i: (i, 0)),
        out_shape=jax.ShapeDtypeStruct((r, PROJ_WIDTH), BF16),
        compiler_params=_cparams(("parallel",)),
        name="layout_w_in",
    )(w)


NORM_ROWS = 128


def _stage_rmsnorm(x_ref, nw_ref, dst_ref, dst_row0, rows):
    nw = nw_ref[...]
    for r0 in range(0, rows, NORM_ROWS):
        n = min(NORM_ROWS, rows - r0)
        xs = x_ref[r0:r0 + n, :]
        ms = jnp.mean(xs * xs, axis=-1, keepdims=True)
        dst_ref[dst_row0 + r0:dst_row0 + r0 + n, :] = (xs * lax.rsqrt(ms + RMS_EPS) * nw).astype(dst_ref.dtype)


def _in_proj_body(x_ref, nw_ref, b_ref, o_ref, hbuf, *, tm):
    @pl.when(pl.program_id(1) == 0)
    def _stage():
        _stage_rmsnorm(x_ref, nw_ref, hbuf, 0, tm)

    o_ref[...] = jnp.dot(hbuf[...], b_ref[...], preferred_element_type=F32)


def _in_proj(x, norm_w, w, tm=1024, tn=512):
    s, k = x.shape
    n = w.shape[1]
    return pl.pallas_call(
        functools.partial(_in_proj_body, tm=tm),
        grid=(s // tm, n // tn),
        in_specs=[pl.BlockSpec((tm, k), lambda i, j: (i, 0)),
                  pl.BlockSpec((1, k), lambda i, j: (0, 0)),
                  pl.BlockSpec((k, tn), lambda i, j: (0, j))],
        out_specs=pl.BlockSpec((tm, tn), lambda i, j: (i, j)),
        out_shape=jax.ShapeDtypeStruct((s, n), F32),
        scratch_shapes=[pltpu.VMEM((tm, k), BF16)],
        compiler_params=_cparams(("parallel", "arbitrary")),
        name="in_proj",
    )(x, norm_w.reshape(1, k), w)


def _out_proj_body(x_ref, a_ref, b_ref, c_ref, wa_ref, wb_ref, wc_ref, o_ref):
    acc = jnp.dot(a_ref[...], wa_ref[...], preferred_element_type=F32)
    acc += jnp.dot(b_ref[...], wb_ref[...], preferred_element_type=F32)
    acc += jnp.dot(c_ref[...], wc_ref[...], preferred_element_type=F32)
    o_ref[...] = x_ref[...] + acc


def _out_proj(x, oa, ob, oc, wa, wb, wc, tm=1024, tn=512):
    s, d = x.shape
    ka, kb, kc = oa.shape[1], ob.shape[1], oc.shape[1]
    return pl.pallas_call(
        _out_proj_body,
        grid=(s // tm, d // tn),
        in_specs=[pl.BlockSpec((tm, tn), lambda i, j: (i, j)),
                  pl.BlockSpec((tm, ka), lambda i, j: (i, 0)),
                  pl.BlockSpec((tm, kb), lambda i, j: (i, 0)),
                  pl.BlockSpec((tm, kc), lambda i, j: (i, 0)),
                  pl.BlockSpec((ka, tn), lambda i, j: (0, j)),
                  pl.BlockSpec((kb, tn), lambda i, j: (0, j)),
                  pl.BlockSpec((kc, tn), lambda i, j: (0, j))],
        out_specs=pl.BlockSpec((tm, tn), lambda i, j: (i, j)),
        out_shape=jax.ShapeDtypeStruct((s, d), F32),
        compiler_params=_cparams(("parallel", "arbitrary")),
        name="out_proj",
    )(x, oa, ob, oc, wa, wb, wc)


FFN_HALO = 16


def _ffn_up_body(x_ref, halo_ref, nw_ref, wg_ref, wu_ref, cw_ref, cb_ref, o_ref, hbuf, *, tm):
    i = pl.program_id(0)

    @pl.when(pl.program_id(1) == 0)
    def _stage():
        _stage_rmsnorm(halo_ref, nw_ref, hbuf, 0, FFN_HALO)
        _stage_rmsnorm(x_ref, nw_ref, hbuf, FFN_HALO, tm)

    @pl.when(jnp.logical_and(pl.program_id(1) == 0, i == 0))
    def _no_history():
        hbuf[0:FFN_HALO, :] = jnp.zeros((FFN_HALO, hbuf.shape[1]), hbuf.dtype)

    g = jnp.dot(hbuf[...], wg_ref[...], preferred_element_type=F32)
    u = jnp.dot(hbuf[FFN_HALO:FFN_HALO + tm, :], wu_ref[...], preferred_element_type=F32)
    cw = cw_ref[...]
    conv = (cw[0:1, :] * pltpu.roll(g, 2, 0)[FFN_HALO:, :]
            + cw[1:2, :] * pltpu.roll(g, 1, 0)[FFN_HALO:, :]
            + cw[2:3, :] * g[FFN_HALO:, :]) + cb_ref[...]
    o_ref[...] = (_silu(conv) * u).astype(o_ref.dtype)


def _ffn_up(x, norm_w, wg, wu, cw, cb, tm=1024, tn=256):
    s, k = x.shape
    f = wg.shape[1]
    halo_blocks = tm // FFN_HALO
    return pl.pallas_call(
        functools.partial(_ffn_up_body, tm=tm),
        grid=(s // tm, f // tn),
        in_specs=[pl.BlockSpec((tm, k), lambda i, j: (i, 0)),
                  pl.BlockSpec((FFN_HALO, k), lambda i, j: (jnp.maximum(i * halo_blocks - 1, 0), 0)),
                  pl.BlockSpec((1, k), lambda i, j: (0, 0)),
                  pl.BlockSpec((k, tn), lambda i, j: (0, j)),
                  pl.BlockSpec((k, tn), lambda i, j: (0, j)),
                  pl.BlockSpec((FFN_CONV, tn), lambda i, j: (0, j)),
                  pl.BlockSpec((1, tn), lambda i, j: (0, j))],
        out_specs=pl.BlockSpec((tm, tn), lambda i, j: (i, j)),
        out_shape=jax.ShapeDtypeStruct((s, f), BF16),
        scratch_shapes=[pltpu.VMEM((tm + FFN_HALO, k), BF16)],
        compiler_params=_cparams(("parallel", "arbitrary")),
        name="ffn_up",
    )(x, x, norm_w.reshape(1, k), wg, wu, cw, cb.reshape(1, f))


def _ffn_down_body(x_ref, a_ref, w_ref, o_ref, acc_ref):
    kk = pl.program_id(2)

    @pl.when(kk == 0)
    def _init():
        acc_ref[...] = x_ref[...]

    acc_ref[...] += jnp.dot(a_ref[...], w_ref[...], preferred_element_type=F32)

    @pl.when(kk == pl.num_programs(2) - 1)
    def _done():
        o_ref[...] = acc_ref[...]


def _ffn_down(x, act, w, tm=512, tn=512, ksplit=1):
    s, d = x.shape
    f = act.shape[1]
    tk = f // ksplit
    return pl.pallas_call(
        _ffn_down_body,
        grid=(s // tm, d // tn, ksplit),
        in_specs=[pl.BlockSpec((tm, tn), lambda i, j, kk: (i, j)),
                  pl.BlockSpec((tm, tk), lambda i, j, kk: (i, kk)),
                  pl.BlockSpec((tk, tn), lambda i, j, kk: (kk, j))],
        out_specs=pl.BlockSpec((tm, tn), lambda i, j, kk: (i, j)),
        out_shape=jax.ShapeDtypeStruct((s, d), F32),
        scratch_shapes=[pltpu.VMEM((tm, tn), F32)],
        compiler_params=_cparams(("parallel", "arbitrary", "arbitrary")),
        name="ffn_down",
    )(x, act, w)


def _unit_lower_inverses(n_mats, size, out):
    row = lax.broadcasted_iota(jnp.int32, (size, size), 0)
    col = lax.broadcasted_iota(jnp.int32, (size, size), 1)
    eye = jnp.where(row == col, 1.0, 0.0).astype(F32)
    invs = [eye + n for n in n_mats]
    powers = [_dot(n, n, mode=MODE_INV) for n in n_mats]
    yield
    steps = int(math.log2(size)) - 1
    for s in range(steps):
        last = s == steps - 1
        for i, (inv, p) in enumerate(zip(invs, powers)):
            if last:
                invs[i] = inv + _dot(inv, p, mode=MODE_INV)
            else:
                both = _dot(jnp.concatenate([inv, p], axis=0), p, mode=MODE_INV)
                invs[i] = inv + both[0:size]
                powers[i] = both[size:2 * size]
        yield
    out.extend(invs)


def _alternate(*stage_generators):
    live = list(stage_generators)
    while live:
        for gen in list(live):
            try:
                next(gen)
            except StopIteration:
                live.remove(gen)


def _gdn_body(alog_ref, dtb_ref, q_ref, k_ref, v_ref, z_ref, ba_ref, cq_ref, ck_ref, cv_ref,
              nw_ref, o_ref, xbuf, state_ref, *, tile):
    c = GDN_CHUNK
    hd = GDN_DIM
    width = GDN_GROUP * hd
    n_chunks = tile // c
    t = pl.program_id(1)

    @pl.when(t == 0)
    def _reset():
        state_ref[...] = jnp.zeros_like(state_ref)
        xbuf[:, 0:8, :] = jnp.zeros((3, 8, width), F32)

    convs = []
    for idx, (src, cw_ref) in enumerate(((q_ref, cq_ref), (k_ref, ck_ref), (v_ref, cv_ref))):
        xbuf[idx, 8:8 + tile, :] = src[...]
        cw = cw_ref[...]
        acc = cw[0:1, :] * xbuf[idx, 5:5 + tile, :]
        acc += cw[1:2, :] * xbuf[idx, 6:6 + tile, :]
        acc += cw[2:3, :] * xbuf[idx, 7:7 + tile, :]
        acc += cw[3:4, :] * xbuf[idx, 8:8 + tile, :]
        convs.append(_silu(acc))
        xbuf[idx, 0:8, :] = xbuf[idx, tile:tile + 8, :]
    q_all, k_all, v_all = convs

    ba = ba_ref[...]
    lane = lax.broadcasted_iota(jnp.int32, (1, LANE), 1)
    ri = lax.broadcasted_iota(jnp.int32, (c, c), 0)
    ci = lax.broadcasted_iota(jnp.int32, (c, c), 1)
    lower = jnp.where(ci <= ri, 1.0, 0.0).astype(F32)
    upper = jnp.where(ri <= ci, 1.0, 0.0).astype(F32)
    r2 = lax.broadcasted_iota(jnp.int32, (2 * c, c), 0)
    c2 = lax.broadcasted_iota(jnp.int32, (2 * c, c), 1)
    band = c2 <= jnp.where(r2 < c, r2 - 1, r2 - c)
    nw = nw_ref[...]

    units = []
    for j in range(GDN_GROUP):
        h = pl.program_id(0) * GDN_GROUP + j
        cols = slice(j * hd, (j + 1) * hd)
        q, k, v = q_all[:, cols], k_all[:, cols], v_all[:, cols]
        q = q * lax.rsqrt(jnp.sum(q * q, axis=-1, keepdims=True) + 1e-6) * (hd ** -0.5)
        k = k * lax.rsqrt(jnp.sum(k * k, axis=-1, keepdims=True) + 1e-6)
        b_col = jnp.sum(jnp.where(lane == h, ba, 0.0), axis=-1, keepdims=True)
        a_col = jnp.sum(jnp.where(lane == h + GDN_HEADS, ba, 0.0), axis=-1, keepdims=True)
        beta = _sigmoid(b_col)
        g = -jnp.exp(alog_ref[h]) * _softplus(a_col + dtb_ref[h])
        g_b = g * jnp.ones((1, hd), F32)
        g_col = jnp.concatenate([_dot(lower, g_b[n * c:(n + 1) * c], mode=MODE_CUMSUM)
                                 for n in range(n_chunks)], axis=0)
        g_row = jnp.concatenate([_dot(jnp.ones((8, c), F32), g_b[n * c:(n + 1) * c, 0:c] * upper,
                                      mode=MODE_CUMSUM) for n in range(n_chunks)], axis=1)
        exp_g = jnp.exp(g_col)
        kb = k * beta
        vb = v * beta
        kbe = kb * exp_g
        qg = q * exp_g
        for n in range(n_chunks):
            rows = slice(n * c, (n + 1) * c)
            g_last = g_col[n * c + c - 1:n * c + c, :]
            gamma = jnp.exp(jnp.minimum(g_col[rows, 0:c] - g_row[0:1, n * c:(n + 1) * c], 0.0))
            units.append(dict(j=j, n=n, k=k[rows], kq=jnp.concatenate([kb[rows], q[rows]], axis=0),
                              rhs=jnp.concatenate([vb[rows], kbe[rows]], axis=1), qg=qg[rows],
                              kd=k[rows] * jnp.exp(g_last - g_col[rows]), decay=jnp.exp(g_last),
                              gamma2=jnp.concatenate([gamma, gamma], axis=0)))
    states = [state_ref[j] for j in range(GDN_GROUP)]
    outs = [[None] * n_chunks for _ in range(GDN_GROUP)]

    def prepare(group):
        for un in group:
            un["scores"] = jnp.where(band, _dot(un["kq"], un["k"], _NT) * un["gamma2"], 0.0)
        yield
        t_invs = []
        yield from _unit_lower_inverses([-un["scores"][0:c] for un in group], c, t_invs)
        for un, t_inv in zip(group, t_invs):
            un["uw"] = _dot(t_inv, un["rhs"], mode=MODE_SOLVE)
        yield
        for un in group:
            un["trans"] = _dot(un["kd"], un["uw"][:, hd:2 * hd], _TN, mode=MODE_STATE)
            un["const"] = _dot(un["kd"], un["uw"][:, 0:hd], _TN, mode=MODE_STATE)
        yield

    def chain(group):
        for n in sorted({un["n"] for un in group}):
            for un in group:
                if un["n"] == n:
                    j = un["j"]
                    un["s0"] = states[j]
                    states[j] = (states[j] * un["decay"] - _dot(un["trans"], states[j], mode=MODE_STATE)
                                 + un["const"])
            yield

    def outputs(group, chunks_per_stage):
        chunk_ids = sorted({un["n"] for un in group})
        for i in range(0, len(chunk_ids), chunks_per_stage):
            part = [un for un in group if un["n"] in chunk_ids[i:i + chunks_per_stage]]
            for un in part:
                un["ws"] = _dot(jnp.concatenate([un["uw"][:, hd:2 * hd], un["qg"]], axis=0), un["s0"],
                                mode=MODE_STATE)
            yield
            for un in part:
                v_new = un["uw"][:, 0:hd] - un["ws"][0:c]
                outs[un["j"]][un["n"]] = (un["ws"][c:2 * c]
                                          + _dot(un["scores"][c:2 * c], v_new, mode=MODE_STATE))
            yield

    _alternate(prepare(units))
    _alternate(chain(units))
    _alternate(outputs(units, n_chunks))
    for j in range(GDN_GROUP):
        state_ref[j] = states[j]
    for j in range(GDN_GROUP):
        cols = slice(j * hd, (j + 1) * hd)
        o = jnp.concatenate(outs[j], axis=0)
        o = o * lax.rsqrt(jnp.mean(o * o, axis=-1, keepdims=True) + RMS_EPS) * nw
        o_ref[:, cols] = (o * _silu(z_ref[:, cols])).astype(o_ref.dtype)


def _gdn(proj, conv_w, a_log, dt_bias, norm_w, tile=512):
    s = proj.shape[0]
    width = GDN_GROUP * GDN_DIM
    per = LANE // GDN_DIM * GDN_GROUP
    blk = lambda col: pl.BlockSpec((tile, width), lambda h, t: (t, col // per + h))
    cblk = lambda col: pl.BlockSpec((GDN_CONV, width), lambda h, t: (0, col // per + h))
    smem = pl.BlockSpec(memory_space=pltpu.SMEM)
    return pl.pallas_call(
        functools.partial(_gdn_body, tile=tile),
        grid=(GDN_HEADS // GDN_GROUP, s // tile),
        in_specs=[smem, smem,
                  blk(COL_GDN_Q), blk(COL_GDN_K), blk(COL_GDN_V), blk(COL_GDN_Z),
                  pl.BlockSpec((tile, LANE), lambda h, t: (t, COL_GDN_BA)),
                  cblk(0), cblk(GDN_HEADS), cblk(2 * GDN_HEADS),
                  pl.BlockSpec((1, GDN_DIM), lambda h, t: (0, 0))],
        out_specs=pl.BlockSpec((tile, width), lambda h, t: (t, h)),
        out_shape=jax.ShapeDtypeStruct((s, GDN_WIDTH), BF16),
        scratch_shapes=[pltpu.VMEM((3, tile + 8, width), F32),
                        pltpu.VMEM((GDN_GROUP, GDN_DIM, GDN_DIM), F32)],
        compiler_params=_cparams(("parallel", "arbitrary")),
        name="gdn",
    )(a_log, dt_bias, proj, proj, proj, proj, proj, conv_w, conv_w, conv_w, norm_w.reshape(1, GDN_DIM))


def _t5_bucket_table():
    exact = NUM_BUCKETS // 2
    i = np.arange(SWA_BLK)[:, None]
    j = np.arange(2 * SWA_BLK)[None, :]
    steps = np.maximum(i + SWA_BLK - j, 0)
    tables = []
    for _, dilation in SWA_PATTERNS:
        dist = steps * dilation
        d = np.maximum(dist, 1).astype(np.float32)
        ratio = (np.log(d / np.float32(exact)) / np.float32(math.log(MAX_DISTANCE / exact))
                 * np.float32(NUM_BUCKETS - exact)).astype(np.float32)
        log_b = exact + ratio.astype(np.int32)
        tables.append(np.where(dist < exact, dist, np.minimum(log_b, NUM_BUCKETS - 1)))
    return np.stack(tables).astype(np.int32)


def _swa_body(rb_ref, bkt_ref, q_ref, kc_ref, kp_ref, vc_ref, vp_ref, o_ref,
              bias_ref, kbuf, vbuf, o_scr, m_scr, l_scr, *, tile):
    h = pl.program_id(0)
    t = pl.program_id(1)
    blk = SWA_BLK

    @pl.when(t == 0)
    def _bias():
        for p in range(len(SWA_PATTERNS)):
            bkt = bkt_ref[p]
            bias = jnp.zeros((blk, 2 * blk), F32)
            for b in range(NUM_BUCKETS):
                bias = jnp.where(bkt == b, rb_ref[b, h], bias)
            bias_ref[p] = bias

    kbuf[0:tile, :] = kp_ref[...]
    kbuf[tile:2 * tile, :] = kc_ref[...]
    vbuf[0:tile, :] = vp_ref[...]
    vbuf[tile:2 * tile, :] = vc_ref[...]

    qi = lax.broadcasted_iota(jnp.int32, (blk, 2 * blk), 0)
    kj = lax.broadcasted_iota(jnp.int32, (blk, 2 * blk), 1)
    in_band = (kj >= qi) & (kj <= qi + blk)
    scale = SWA_DIM ** -0.5

    band_first = in_band & (kj >= jnp.where(t == 0, blk, 0))
    ones_row = jnp.ones((1, SWA_DIM), F32)

    for p, (window, dil) in enumerate(SWA_PATTERNS):
        span = blk * dil
        bias = bias_ref[p]
        blocks = [(res + n * span, n) for n in range(tile // span) for res in range(dil)]
        for g0 in range(0, len(blocks), SWA_GROUP):
            group = blocks[g0:g0 + SWA_GROUP]
            scores = []
            for start, n in group:
                qb = q_ref[pl.ds(start, blk, stride=dil), :] * scale
                kw = kbuf[pl.ds(tile + start - span, 2 * blk, stride=dil), :]
                sc = _dot(qb, kw, _NT) + bias
                scores.append(jnp.where(band_first if n == 0 else in_band, sc, NEG_INF))
            probs = []
            for sc in scores:
                m = jnp.max(sc, axis=-1, keepdims=True)
                pe = jnp.exp(sc - m)
                probs.append((pe, m, jnp.sum(pe, axis=-1, keepdims=True)))
            for (start, n), (pe, m, l) in zip(group, probs):
                vw = vbuf[pl.ds(tile + start - span, 2 * blk, stride=dil), :]
                o_scr[p, pl.ds(start, blk, stride=dil), :] = _dot(pe, vw)
                m_scr[p, pl.ds(start, blk, stride=dil), :] = m * ones_row
                l_scr[p, pl.ds(start, blk, stride=dil), :] = l * ones_row

    m_max = jnp.maximum(jnp.maximum(m_scr[0], m_scr[1]), m_scr[2])
    num = jnp.zeros((tile, SWA_DIM), F32)
    den = jnp.zeros((tile, SWA_DIM), F32)
    for p in range(len(SWA_PATTERNS)):
        sc = jnp.exp(m_scr[p] - m_max)
        num += o_scr[p] * sc
        den += l_scr[p] * sc
    o_ref[...] = (num / den).astype(o_ref.dtype)


def _swa(proj, rel_bias, tile=SWA_TILE):
    s = proj.shape[0]
    n_pat = len(SWA_PATTERNS)
    bkt = jnp.asarray(_t5_bucket_table())
    cur = lambda col: pl.BlockSpec((tile, LANE), lambda h, t: (t, col + h))
    prev = lambda col: pl.BlockSpec((tile, LANE), lambda h, t: (jnp.maximum(t - 1, 0), col + h))
    return pl.pallas_call(
        functools.partial(_swa_body, tile=tile),
        grid=(SWA_HEADS, s // tile),
        in_specs=[pl.BlockSpec(memory_space=pltpu.SMEM),
                  pl.BlockSpec((n_pat, SWA_BLK, 2 * SWA_BLK), lambda h, t: (0, 0, 0)),
                  cur(COL_SWA_Q), cur(COL_SWA_K), prev(COL_SWA_K), cur(COL_SWA_V), prev(COL_SWA_V)],
        out_specs=pl.BlockSpec((tile, LANE), lambda h, t: (t, h)),
        out_shape=jax.ShapeDtypeStruct((s, SWA_WIDTH), BF16),
        scratch_shapes=[pltpu.VMEM((n_pat, SWA_BLK, 2 * SWA_BLK), F32),
                        pltpu.VMEM((2 * tile, SWA_DIM), F32),
                        pltpu.VMEM((2 * tile, SWA_DIM), F32),
                        pltpu.VMEM((n_pat, tile, SWA_DIM), F32),
                        pltpu.VMEM((n_pat, tile, SWA_DIM), F32),
                        pltpu.VMEM((n_pat, tile, SWA_DIM), F32)],
        compiler_params=_cparams(("parallel", "arbitrary")),
        name="swa",
    )(rel_bias, bkt, proj, proj, proj, proj, proj)


PRM_W0, PRM_A0, PRM_KK, PRM_KA, PRM_RK, PRM_LNW, PRM_LNB, PRM_MUR, PRM_MUK, PRM_MUV = range(10)
PRM_ROWS = 16


LORA_IN_WIDTH = DECAY_LORA + ICL_LORA + GATE_LORA_PAD


def _rwkv_lora_in_body(wd_ref, ad_ref, gd_ref, mul_ref, mug_ref, o_ref, xb, xg, *, tile):
    @pl.when(pl.program_id(0) == 0)
    def _reset():
        xb[:, 0:8, :] = jnp.zeros((2, 8, LANE), F32)
        xg[0:8, :] = jnp.zeros((8, GATE_LORA_PAD), F32)

    def shifted(buf, src, mu):
        buf[8:8 + tile, :] = src[...]
        cur = buf[8:8 + tile, :]
        prev = buf[7:7 + tile, :]
        buf[0:8, :] = buf[tile:tile + 8, :]
        return cur + mu * (prev - cur)

    wd = shifted(xb.at[0], wd_ref, mul_ref[0:1, :])
    ad = shifted(xb.at[1], ad_ref, mul_ref[1:2, :])
    gd = shifted(xg, gd_ref, mug_ref[...])
    o_ref[:, 0:DECAY_LORA] = jnp.tanh(wd).astype(o_ref.dtype)
    o_ref[:, DECAY_LORA:DECAY_LORA + ICL_LORA] = ad.astype(o_ref.dtype)
    o_ref[:, DECAY_LORA + ICL_LORA:] = _sigmoid(gd).astype(o_ref.dtype)


def _rwkv_lora_in(proj, mu_lora, mu_gate, tile=1024):
    s = proj.shape[0]
    fixed = lambda col: pl.BlockSpec((tile, LANE), lambda t: (t, col))
    return pl.pallas_call(
        functools.partial(_rwkv_lora_in_body, tile=tile),
        grid=(s // tile,),
        in_specs=[fixed(COL_WD), fixed(COL_AD),
                  pl.BlockSpec((tile, GATE_LORA_PAD), lambda t: (t, COL_GD * LANE // GATE_LORA_PAD)),
                  pl.BlockSpec((8, LANE), lambda t: (0, 0)),
                  pl.BlockSpec((1, GATE_LORA_PAD), lambda t: (0, 0))],
        out_specs=pl.BlockSpec((tile, LORA_IN_WIDTH), lambda t: (t, 0)),
        out_shape=jax.ShapeDtypeStruct((s, LORA_IN_WIDTH), BF16),
        scratch_shapes=[pltpu.VMEM((2, tile + 8, LANE), F32),
                        pltpu.VMEM((tile + 8, GATE_LORA_PAD), F32)],
        compiler_params=_cparams(("arbitrary",)),
        name="rwkv_lora_in",
    )(proj, proj, proj, mu_lora, mu_gate)


def _rwkv_body(r_ref, k_ref, v_ref, lora_ref, prm_ref, wup_ref, aup_ref, gup_ref, o_ref, xb, state_ref,
               *, tile):
    c = RWKV_CHUNK
    nd = RWKV_DIM
    t = pl.program_id(1)

    @pl.when(t == 0)
    def _reset():
        state_ref[...] = jnp.zeros_like(state_ref)
        xb[:, 0:8, :] = jnp.zeros((3, 8, RWKV_GROUP * LANE), F32)

    prm = prm_ref[...]
    row = lambda i: prm[i:i + 1, :]

    def shifted(idx, src, mu):
        xb[idx, 8:8 + tile, :] = src[...]
        cur = xb[idx, 8:8 + tile, :]
        prev = xb[idx, 7:7 + tile, :]
        xb[idx, 0:8, :] = xb[idx, tile:tile + 8, :]
        return cur + mu * (prev - cur)

    r = shifted(0, r_ref, row(PRM_MUR))
    k = shifted(1, k_ref, row(PRM_MUK))
    v = shifted(2, v_ref, row(PRM_MUV))

    lora = lora_ref[...]
    w_log = -_softplus(-(row(PRM_W0) + _dot(lora[:, 0:DECAY_LORA], wup_ref[...]))) - 0.5
    log_w = -jnp.exp(w_log)
    a_gate = _sigmoid(row(PRM_A0) + _dot(lora[:, DECAY_LORA:DECAY_LORA + ICL_LORA], aup_ref[...]))
    g_gate = _dot(lora[:, DECAY_LORA + ICL_LORA:], gup_ref[...])
    kx = k * row(PRM_KK)
    k_mod = k * (1.0 + (a_gate - 1.0) * row(PRM_KA))

    ri = lax.broadcasted_iota(jnp.int32, (c, c), 0)
    ci = lax.broadcasted_iota(jnp.int32, (c, c), 1)
    lower = jnp.where(ci <= ri, 1.0, 0.0).astype(F32)
    cum = jnp.concatenate([_dot(lower, log_w[n * c:(n + 1) * c], mode=MODE_CUMSUM)
                           for n in range(tile // c)], axis=0)

    r2 = lax.broadcasted_iota(jnp.int32, (2 * c, 2 * c), 0)
    c2 = lax.broadcasted_iota(jnp.int32, (2 * c, 2 * c), 1)
    col_in = jnp.where(c2 < c, c2, c2 - c)
    band = col_in <= jnp.where(r2 < c, r2 - 1, r2 - c)

    width = RWKV_GROUP * LANE
    n_heads = width // nd
    n_chunks = tile // c
    lane = lax.broadcasted_iota(jnp.int32, (1, LANE), 1)
    head_masks = [(lane // nd) == hh for hh in range(LANE // nd)]

    def head_sum(x):
        blocks = []
        for b in range(RWKV_GROUP):
            xs = x[:, b * LANE:(b + 1) * LANE]
            out = jnp.zeros_like(xs)
            for msk in head_masks:
                out = jnp.where(msk, jnp.sum(jnp.where(msk, xs, 0.0), axis=-1, keepdims=True), out)
            blocks.append(out)
        return blocks[0] if RWKV_GROUP == 1 else jnp.concatenate(blocks, axis=1)

    kk = kx * lax.rsqrt(head_sum(kx * kx) + 1e-12)
    b_vec = kk * a_gate
    c_last = jnp.concatenate(
        [jnp.broadcast_to(cum[n * c + c - 1:n * c + c, :], (c, width)) for n in range(n_chunks)], axis=0)
    p_out = jnp.exp(-cum)
    a_t = -kk * jnp.exp(cum - log_w)
    r_t = r * jnp.exp(cum)
    k_t = k_mod * p_out
    b_t = b_vec * p_out
    decay_end = jnp.exp(c_last - cum)
    k_e = k_mod * decay_end
    b_e = b_vec * decay_end
    decay_all = jnp.exp(c_last)

    units = []
    for hh in range(n_heads):
        lanes = slice(hh * nd, (hh + 1) * nd)
        for n in range(n_chunks):
            rows = slice(n * c, (n + 1) * c)
            units.append(dict(hh=hh, n=n, a=a_t[rows, lanes], r=r_t[rows, lanes], v=v[rows, lanes],
                              ar=jnp.concatenate([a_t[rows, lanes], r_t[rows, lanes]], axis=0),
                              bk=jnp.concatenate([b_t[rows, lanes], k_t[rows, lanes]], axis=0),
                              bk_e=jnp.concatenate([b_e[rows, lanes], k_e[rows, lanes]], axis=0),
                              decay=decay_all[n * c:n * c + 1, lanes]))
    states = [state_ref[hh] for hh in range(n_heads)]
    ys = [[None] * n_chunks for _ in range(n_heads)]

    def prepare(group):
        for un in group:
            un["mm"] = jnp.where(band, _dot(un["ar"], un["bk"], _NT), 0.0)
        yield
        t_invs = []
        yield from _unit_lower_inverses([un["mm"][0:c, 0:c] for un in group], c, t_invs)
        for un in group:
            un["mv"] = _dot(un["mm"][0:c, c:2 * c], un["v"], mode=MODE_SOLVE)
        yield
        for un, t_inv in zip(group, t_invs):
            un["wu"] = _dot(t_inv, jnp.concatenate([un["a"], un["mv"]], axis=1), mode=MODE_SOLVE)
        yield
        for un in group:
            un["trans"] = _dot(un["wu"][:, 0:nd], un["bk_e"][0:c], _TN, mode=MODE_STATE)
            un["const"] = _dot(jnp.concatenate([un["wu"][:, nd:2 * nd], un["v"]], axis=0), un["bk_e"],
                               _TN, mode=MODE_STATE)
        yield

    def chain(group):
        for n in sorted({un["n"] for un in group}):
            for un in group:
                if un["n"] == n:
                    hh = un["hh"]
                    un["s0"] = states[hh]
                    states[hh] = (states[hh] * un["decay"]
                                  + _dot(states[hh], un["trans"], mode=MODE_STATE) + un["const"])
            yield

    def outputs(group, chunks_per_stage):
        chunk_ids = sorted({un["n"] for un in group})
        for i in range(0, len(chunk_ids), chunks_per_stage):
            part = [un for un in group if un["n"] in chunk_ids[i:i + chunks_per_stage]]
            for un in part:
                un["sr"] = _dot(jnp.concatenate([un["wu"][:, 0:nd], un["r"]], axis=0), un["s0"], _NT,
                                mode=MODE_STATE)
            yield
            for un in part:
                u = un["sr"][0:c] + un["wu"][:, nd:2 * nd]
                uv = jnp.concatenate([u, un["v"]], axis=0)
                ys[un["hh"]][un["n"]] = un["sr"][c:2 * c] + _dot(un["mm"][c:2 * c], uv, mode=MODE_STATE)
            yield

    _alternate(prepare(units))
    _alternate(chain(units))
    _alternate(outputs(units, n_chunks))
    for hh in range(n_heads):
        state_ref[hh] = states[hh]
    y = jnp.concatenate([jnp.concatenate(ys[hh], axis=0) for hh in range(n_heads)], axis=1)
    mean = head_sum(y) * (1.0 / nd)
    var = head_sum(jnp.square(y - mean)) * (1.0 / nd)
    y_n = (y - mean) * lax.rsqrt(var + GN_EPS) * row(PRM_LNW) + row(PRM_LNB)
    bonus = head_sum(r * k_mod * row(PRM_RK)) * v
    o_ref[...] = ((y_n + bonus) * g_gate).astype(o_ref.dtype)


def _rwkv(proj, prm, mu_lora, mu_gate, w_up, a_up, g_up, tile=512):
    s = proj.shape[0]
    width = RWKV_GROUP * LANE
    groups = RWKV_WIDTH // width
    lora_in = _rwkv_lora_in(proj, mu_lora, mu_gate)
    blk = lambda col: pl.BlockSpec((tile, width), lambda h, t: (t, col // RWKV_GROUP + h))
    return pl.pallas_call(
        functools.partial(_rwkv_body, tile=tile),
        grid=(groups, s // tile),
        in_specs=[blk(COL_R), blk(COL_K), blk(COL_V),
                  pl.BlockSpec((tile, LORA_IN_WIDTH), lambda h, t: (t, 0)),
                  pl.BlockSpec((PRM_ROWS, width), lambda h, t: (0, h)),
                  pl.BlockSpec((DECAY_LORA, width), lambda h, t: (0, h)),
                  pl.BlockSpec((ICL_LORA, width), lambda h, t: (0, h)),
                  pl.BlockSpec((GATE_LORA_PAD, width), lambda h, t: (0, h))],
        out_specs=pl.BlockSpec((tile, width), lambda h, t: (t, h)),
        out_shape=jax.ShapeDtypeStruct((s, RWKV_WIDTH), BF16),
        scratch_shapes=[pltpu.VMEM((3, tile + 8, width), F32),
                        pltpu.VMEM((width // RWKV_DIM, RWKV_DIM, RWKV_DIM), F32)],
        compiler_params=_cparams(("parallel", "arbitrary")),
        name="rwkv7",
    )(proj, proj, proj, lora_in, prm, w_up, a_up, g_up)


def _layout_rwkv_params(mu, w0, a0, k_k, k_a, r_k, ln_w, ln_b):
    rw = RWKV_WIDTH
    mu_r, mu_k, mu_v = mu[0:rw], mu[rw:2 * rw], mu[2 * rw:3 * rw]
    o = 3 * rw
    mu_wd = mu[o:o + DECAY_LORA]; o += DECAY_LORA
    mu_ad = mu[o:o + ICL_LORA]; o += ICL_LORA
    mu_gd = mu[o:o + GATE_LORA]
    rows = [w0, a0, k_k, k_a, r_k.reshape(rw), ln_w, ln_b, mu_r, mu_k, mu_v]
    prm = jnp.stack(rows + [jnp.zeros((rw,), F32)] * (PRM_ROWS - len(rows))).astype(F32)
    mu_lora = jnp.stack([mu_wd, mu_ad] + [jnp.zeros((LANE,), F32)] * 6).astype(F32)
    mu_gate = jnp.concatenate([mu_gd, jnp.zeros((GATE_LORA_PAD - GATE_LORA,), F32)]).reshape(1, GATE_LORA_PAD)
    return prm, mu_lora, mu_gate


def kernel(x, attn_norm, w_in, gdn_conv, gdn_a_log, gdn_dt_bias, gdn_norm, rwkv_mu, rwkv_w0, rwkv_w_up, rwkv_a0, rwkv_a_up, rwkv_g_up, rwkv_k_k, rwkv_k_a, rwkv_r_k, rwkv_ln_w, rwkv_ln_b, w_out, ffn_norm, w_ffn_gate, w_ffn_up, ffn_conv, ffn_conv_b, w_ffn_down, rel_bias, final_norm):
    batch, seq, d = x.shape
    depth = w_in.shape[0]
    outs = []
    w_in_bf16 = w_in.astype(BF16)
    rows_in = [x.reshape(seq, d)] if batch == 1 else [x[b] for b in range(batch)]
    for xb in rows_in:
        for l in range(depth):
            proj = _in_proj(xb, attn_norm[l], _layout_w_in_pallas(w_in_bf16, l))
            o_a = _gdn(proj, gdn_conv[l], gdn_a_log[l], gdn_dt_bias[l], gdn_norm[l])
            o_b = _swa(proj, rel_bias)
            prm, mu_lora, mu_gate = _layout_rwkv_params(
                rwkv_mu[l], rwkv_w0[l], rwkv_a0[l], rwkv_k_k[l], rwkv_k_a[l], rwkv_r_k[l],
                rwkv_ln_w[l], rwkv_ln_b[l])
            g_up = jnp.concatenate(
                [rwkv_g_up[l], jnp.zeros((GATE_LORA_PAD - GATE_LORA, RWKV_WIDTH), F32)], axis=0)
            o_c = _rwkv(proj, prm, mu_lora, mu_gate, rwkv_w_up[l].astype(BF16),
                        rwkv_a_up[l].astype(BF16), g_up.astype(BF16))
            wo = _to_bf16(w_out, l)
            xb = _out_proj(xb, o_a, o_b, o_c, wo[0:GDN_WIDTH], wo[GDN_WIDTH:GDN_WIDTH + SWA_WIDTH],
                           wo[GDN_WIDTH + SWA_WIDTH:])
            act = _ffn_up(xb, ffn_norm[l], _to_bf16(w_ffn_gate, l), _to_bf16(w_ffn_up, l), ffn_conv[l],
                          ffn_conv_b[l])
            xb = _ffn_down(xb, act, _to_bf16(w_ffn_down, l))
        outs.append(_rmsnorm(xb, final_norm, x.dtype))
    return outs[0].reshape(1, seq, d) if batch == 1 else jnp.stack(outs)
```

```python
import functools
import math

import numpy as np
import jax
import jax.numpy as jnp
from jax import lax
from jax.experimental import pallas as pl
from jax.experimental.pallas import tpu as pltpu

F32 = jnp.float32
BF16 = jnp.bfloat16
HIGHEST = lax.Precision.HIGHEST

LANE = 128
D_MODEL = 4096
RMS_EPS = 1e-6
GN_EPS = 64e-5
NEG_INF = -1e30
VMEM_LIMIT = 56 * 1024 * 1024

GDN_DIM = 128
GDN_HEADS = 12
GDN_WIDTH = GDN_HEADS * GDN_DIM
GDN_CONV = 4
GDN_CHUNK = 64
GDN_GROUP = 4
SWA_DIM = 128
SWA_HEADS = 8
SWA_WIDTH = SWA_HEADS * SWA_DIM
SWA_PATTERNS = ((128, 1), (512, 4), (2048, 16))
SWA_BLK = 128
SWA_TILE = 2048
SWA_GROUP = 4
NUM_BUCKETS = 32
MAX_DISTANCE = 2048
RWKV_DIM = 64
RWKV_HEADS = 24
RWKV_WIDTH = RWKV_HEADS * RWKV_DIM
RWKV_CHUNK = 64
RWKV_GROUP = 1
DECAY_LORA = 128
ICL_LORA = 128
GATE_LORA = 480
GATE_LORA_PAD = 512
D_FF = 11008
FFN_CONV = 3

COL_GDN_Q, COL_GDN_K, COL_GDN_V, COL_GDN_Z, COL_GDN_BA = 0, 12, 24, 36, 48
COL_SWA_Q, COL_SWA_K, COL_SWA_V = 49, 57, 65
COL_WD, COL_AD = 73, 74
COL_GD = 76
COL_R, COL_K, COL_V = 80, 92, 104
PROJ_BLOCKS = 116
PROJ_WIDTH = PROJ_BLOCKS * LANE


def _cparams(sem):
    return pltpu.CompilerParams(dimension_semantics=sem, vmem_limit_bytes=VMEM_LIMIT)


def _silu(x):
    return x * (1.0 / (1.0 + jnp.exp(-x)))


def _sigmoid(x):
    return 1.0 / (1.0 + jnp.exp(-x))


def _softplus(x):
    return jnp.maximum(x, 0.0) + jnp.log(1.0 + jnp.exp(-jnp.abs(x)))


def _split_bf16(x):
    hi = x.astype(BF16)
    return hi, (x - hi.astype(F32)).astype(BF16)


def _dot(a, b, dims=(((1,), (0,)), ((), ())), mode="bf16"):
    if mode == "f32":
        return lax.dot_general(a.astype(F32), b.astype(F32), dims, precision=HIGHEST,
                               preferred_element_type=F32)
    if mode == "x3":
        a_hi, a_lo = _split_bf16(a)
        b_hi, b_lo = _split_bf16(b)
        mm = lambda p, q: lax.dot_general(p, q, dims, preferred_element_type=F32)
        return mm(a_hi, b_hi) + (mm(a_hi, b_lo) + mm(a_lo, b_hi))
    return lax.dot_general(a.astype(BF16), b.astype(BF16), dims, preferred_element_type=F32)


MODE_CUMSUM = "f32"
MODE_INV = "bf16"
MODE_SOLVE = "bf16"
MODE_STATE = "bf16"


_NT = (((1,), (1,)), ((), ()))
_TN = (((0,), (0,)), ((), ()))


def _rmsnorm_body(x_ref, w_ref, o_ref):
    x = x_ref[...]
    ms = jnp.mean(x * x, axis=-1, keepdims=True)
    o_ref[...] = (x * lax.rsqrt(ms + RMS_EPS) * w_ref[...]).astype(o_ref.dtype)


def _rmsnorm(x, w, out_dtype, tm=512):
    s, d = x.shape
    return pl.pallas_call(
        _rmsnorm_body,
        grid=(s // tm,),
        in_specs=[pl.BlockSpec((tm, d), lambda i: (i, 0)),
                  pl.BlockSpec((1, d), lambda i: (0, 0))],
        out_specs=pl.BlockSpec((tm, d), lambda i: (i, 0)),
        out_shape=jax.ShapeDtypeStruct((s, d), out_dtype),
        compiler_params=_cparams(("parallel",)),
        name="rmsnorm",
    )(x, w.reshape(1, d))


def _cast_body(w_ref, o_ref):
    o_ref[...] = w_ref[...].astype(o_ref.dtype)


def _to_bf16(w, layer, rows=256):
    _, r, c = w.shape
    return pl.pallas_call(
        _cast_body,
        grid=(r // rows,),
        in_specs=[pl.BlockSpec((None, rows, c), lambda i: (layer, i, 0))],
        out_specs=pl.BlockSpec((rows, c), lambda i: (i, 0)),
        out_shape=jax.ShapeDtypeStruct((r, c), BF16),
        compiler_params=_cparams(("parallel",)),
        name="to_bf16",
    )(w)


def _w_in_segments():
    gw, sw, rw = GDN_WIDTH, SWA_WIDTH, RWKV_WIDTH
    src = 0
    segs = []
    for width, dst in ((4 * gw, COL_GDN_Q), (2 * GDN_HEADS, COL_GDN_BA), (3 * sw, COL_SWA_Q),
                       (3 * rw, COL_R), (DECAY_LORA, COL_WD), (ICL_LORA, COL_AD), (GATE_LORA, COL_GD)):
        segs.append((src, width, dst * LANE))
        src += width
    return segs, src


def _layout_w_in_body(w_ref, o_ref):
    o_ref[...] = jnp.zeros_like(o_ref)
    for src, width, dst in _w_in_segments()[0]:
        o_ref[:, dst:dst + width] = w_ref[:, src:src + width].astype(o_ref.dtype)


def _layout_w_in_pallas(w, layer, rows=32):
    _, r, c = w.shape
    assert c == _w_in_segments()[1]
    return pl.pallas_call(
        _layout_w_in_body,
        grid=(r // rows,),
        in_specs=[pl.BlockSpec((None, rows, c), lambda i: (layer, i, 0))],
        out_specs=pl.BlockSpec((rows, PROJ_WIDTH), lambda i: (i, 0)),
        out_shape=jax.ShapeDtypeStruct((r, PROJ_WIDTH), BF16),
        compiler_params=_cparams(("parallel",)),
        name="layout_w_in",
    )(w)


NORM_ROWS = 16


def _stage_rmsnorm(x_ref, nw_ref, dst_ref, dst_row0, rows):
    nw = nw_ref[...]
    for r0 in range(0, rows, NORM_ROWS):
        n = min(NORM_ROWS, rows - r0)
        xs = x_ref[r0:r0 + n, :]
        ms = jnp.mean(xs * xs, axis=-1, keepdims=True)
        dst_ref[dst_row0 + r0:dst_row0 + r0 + n, :] = (xs * lax.rsqrt(ms + RMS_EPS) * nw).astype(dst_ref.dtype)


def _in_proj_body(x_ref, nw_ref, b_ref, o_ref, hbuf, *, tm):
    @pl.when(pl.program_id(1) == 0)
    def _stage():
        _stage_rmsnorm(x_ref, nw_ref, hbuf, 0, tm)

    o_ref[...] = jnp.dot(hbuf[...], b_ref[...], preferred_element_type=F32)


def _in_proj(x, norm_w, w, tm=1024, tn=512):
    s, k = x.shape
    n = w.shape[1]
    return pl.pallas_call(
        functools.partial(_in_proj_body, tm=tm),
        grid=(s // tm, n // tn),
        in_specs=[pl.BlockSpec((tm, k), lambda i, j: (i, 0)),
                  pl.BlockSpec((1, k), lambda i, j: (0, 0)),
                  pl.BlockSpec((k, tn), lambda i, j: (0, j))],
        out_specs=pl.BlockSpec((tm, tn), lambda i, j: (i, j)),
        out_shape=jax.ShapeDtypeStruct((s, n), F32),
        scratch_shapes=[pltpu.VMEM((tm, k), BF16)],
        compiler_params=_cparams(("parallel", "arbitrary")),
        name="in_proj",
    )(x, norm_w.reshape(1, k), w)


def _out_proj_body(x_ref, a_ref, b_ref, c_ref, wa_ref, wb_ref, wc_ref, o_ref):
    acc = jnp.dot(a_ref[...], wa_ref[...], preferred_element_type=F32)
    acc += jnp.dot(b_ref[...], wb_ref[...], preferred_element_type=F32)
    acc += jnp.dot(c_ref[...], wc_ref[...], preferred_element_type=F32)
    o_ref[...] = x_ref[...] + acc


def _out_proj(x, oa, ob, oc, wa, wb, wc, tm=1024, tn=512):
    s, d = x.shape
    ka, kb, kc = oa.shape[1], ob.shape[1], oc.shape[1]
    return pl.pallas_call(
        _out_proj_body,
        grid=(s // tm, d // tn),
        in_specs=[pl.BlockSpec((tm, tn), lambda i, j: (i, j)),
                  pl.BlockSpec((tm, ka), lambda i, j: (i, 0)),
                  pl.BlockSpec((tm, kb), lambda i, j: (i, 0)),
                  pl.BlockSpec((tm, kc), lambda i, j: (i, 0)),
                  pl.BlockSpec((ka, tn), lambda i, j: (0, j)),
                  pl.BlockSpec((kb, tn), lambda i, j: (0, j)),
                  pl.BlockSpec((kc, tn), lambda i, j: (0, j))],
        out_specs=pl.BlockSpec((tm, tn), lambda i, j: (i, j)),
        out_shape=jax.ShapeDtypeStruct((s, d), F32),
        compiler_params=_cparams(("parallel", "arbitrary")),
        name="out_proj",
    )(x, oa, ob, oc, wa, wb, wc)


FFN_HALO = 16


def _ffn_up_body(x_ref, halo_ref, nw_ref, wg_ref, wu_ref, cw_ref, cb_ref, o_ref, hbuf, *, tm):
    i = pl.program_id(0)

    @pl.when(pl.program_id(1) == 0)
    def _stage():
        _stage_rmsnorm(halo_ref, nw_ref, hbuf, 0, FFN_HALO)
        _stage_rmsnorm(x_ref, nw_ref, hbuf, FFN_HALO, tm)

    @pl.when(jnp.logical_and(pl.program_id(1) == 0, i == 0))
    def _no_history():
        hbuf[0:FFN_HALO, :] = jnp.zeros((FFN_HALO, hbuf.shape[1]), hbuf.dtype)

    g = jnp.dot(hbuf[...], wg_ref[...], preferred_element_type=F32)
    u = jnp.dot(hbuf[FFN_HALO:FFN_HALO + tm, :], wu_ref[...], preferred_element_type=F32)
    cw = cw_ref[...]
    conv = (cw[0:1, :] * pltpu.roll(g, 2, 0)[FFN_HALO:, :]
            + cw[1:2, :] * pltpu.roll(g, 1, 0)[FFN_HALO:, :]
            + cw[2:3, :] * g[FFN_HALO:, :]) + cb_ref[...]
    o_ref[...] = (_silu(conv) * u).astype(o_ref.dtype)


def _ffn_up(x, norm_w, wg, wu, cw, cb, tm=1024, tn=256):
    s, k = x.shape
    f = wg.shape[1]
    halo_blocks = tm // FFN_HALO
    return pl.pallas_call(
        functools.partial(_ffn_up_body, tm=tm),
        grid=(s // tm, f // tn),
        in_specs=[pl.BlockSpec((tm, k), lambda i, j: (i, 0)),
                  pl.BlockSpec((FFN_HALO, k), lambda i, j: (jnp.maximum(i * halo_blocks - 1, 0), 0)),
                  pl.BlockSpec((1, k), lambda i, j: (0, 0)),
                  pl.BlockSpec((k, tn), lambda i, j: (0, j)),
                  pl.BlockSpec((k, tn), lambda i, j: (0, j)),
                  pl.BlockSpec((FFN_CONV, tn), lambda i, j: (0, j)),
                  pl.BlockSpec((1, tn), lambda i, j: (0, j))],
        out_specs=pl.BlockSpec((tm, tn), lambda i, j: (i, j)),
        out_shape=jax.ShapeDtypeStruct((s, f), BF16),
        scratch_shapes=[pltpu.VMEM((tm + FFN_HALO, k), BF16)],
        compiler_params=_cparams(("parallel", "arbitrary")),
        name="ffn_up",
    )(x, x, norm_w.reshape(1, k), wg, wu, cw, cb.reshape(1, f))


def _ffn_down_body(x_ref, a_ref, w_ref, o_ref, acc_ref):
    kk = pl.program_id(2)

    @pl.when(kk == 0)
    def _init():
        acc_ref[...] = x_ref[...]

    acc_ref[...] += jnp.dot(a_ref[...], w_ref[...], preferred_element_type=F32)

    @pl.when(kk == pl.num_programs(2) - 1)
    def _done():
        o_ref[...] = acc_ref[...]


def _ffn_down(x, act, w, tm=512, tn=512, ksplit=1):
    s, d = x.shape
    f = act.shape[1]
    tk = f // ksplit
    return pl.pallas_call(
        _ffn_down_body,
        grid=(s // tm, d // tn, ksplit),
        in_specs=[pl.BlockSpec((tm, tn), lambda i, j, kk: (i, j)),
                  pl.BlockSpec((tm, tk), lambda i, j, kk: (i, kk)),
                  pl.BlockSpec((tk, tn), lambda i, j, kk: (kk, j))],
        out_specs=pl.BlockSpec((tm, tn), lambda i, j, kk: (i, j)),
        out_shape=jax.ShapeDtypeStruct((s, d), F32),
        scratch_shapes=[pltpu.VMEM((tm, tn), F32)],
        compiler_params=_cparams(("parallel", "arbitrary", "arbitrary")),
        name="ffn_down",
    )(x, act, w)


def _unit_lower_inverses(n_mats, size, out):
    row = lax.broadcasted_iota(jnp.int32, (size, size), 0)
    col = lax.broadcasted_iota(jnp.int32, (size, size), 1)
    eye = jnp.where(row == col, 1.0, 0.0).astype(F32)
    invs = [eye + n for n in n_mats]
    powers = [_dot(n, n, mode=MODE_INV) for n in n_mats]
    yield
    steps = int(math.log2(size)) - 1
    for s in range(steps):
        last = s == steps - 1
        for i, (inv, p) in enumerate(zip(invs, powers)):
            if last:
                invs[i] = inv + _dot(inv, p, mode=MODE_INV)
            else:
                both = _dot(jnp.concatenate([inv, p], axis=0), p, mode=MODE_INV)
                invs[i] = inv + both[0:size]
                powers[i] = both[size:2 * size]
        yield
    out.extend(invs)


def _alternate(*stage_generators):
    live = list(stage_generators)
    while live:
        for gen in list(live):
            try:
                next(gen)
            except StopIteration:
                live.remove(gen)


def _gdn_body(alog_ref, dtb_ref, q_ref, k_ref, v_ref, z_ref, ba_ref, cq_ref, ck_ref, cv_ref,
              nw_ref, o_ref, xbuf, state_ref, *, tile):
    c = GDN_CHUNK
    hd = GDN_DIM
    width = GDN_GROUP * hd
    n_chunks = tile // c
    t = pl.program_id(1)

    @pl.when(t == 0)
    def _reset():
        state_ref[...] = jnp.zeros_like(state_ref)
        xbuf[:, 0:8, :] = jnp.zeros((3, 8, width), F32)

    convs = []
    for idx, (src, cw_ref) in enumerate(((q_ref, cq_ref), (k_ref, ck_ref), (v_ref, cv_ref))):
        xbuf[idx, 8:8 + tile, :] = src[...]
        cw = cw_ref[...]
        acc = cw[0:1, :] * xbuf[idx, 5:5 + tile, :]
        acc += cw[1:2, :] * xbuf[idx, 6:6 + tile, :]
        acc += cw[2:3, :] * xbuf[idx, 7:7 + tile, :]
        acc += cw[3:4, :] * xbuf[idx, 8:8 + tile, :]
        convs.append(_silu(acc))
        xbuf[idx, 0:8, :] = xbuf[idx, tile:tile + 8, :]
    q_all, k_all, v_all = convs

    ba = ba_ref[...]
    lane = lax.broadcasted_iota(jnp.int32, (1, LANE), 1)
    ri = lax.broadcasted_iota(jnp.int32, (c, c), 0)
    ci = lax.broadcasted_iota(jnp.int32, (c, c), 1)
    lower = jnp.where(ci <= ri, 1.0, 0.0).astype(F32)
    upper = jnp.where(ri <= ci, 1.0, 0.0).astype(F32)
    r2 = lax.broadcasted_iota(jnp.int32, (2 * c, c), 0)
    c2 = lax.broadcasted_iota(jnp.int32, (2 * c, c), 1)
    band = c2 <= jnp.where(r2 < c, r2 - 1, r2 - c)
    nw = nw_ref[...]

    units = []
    for j in range(GDN_GROUP):
        h = pl.program_id(0) * GDN_GROUP + j
        cols = slice(j * hd, (j + 1) * hd)
        q, k, v = q_all[:, cols], k_all[:, cols], v_all[:, cols]
        q = q * lax.rsqrt(jnp.sum(q * q, axis=-1, keepdims=True) + 1e-6) * (hd ** -0.5)
        k = k * lax.rsqrt(jnp.sum(k * k, axis=-1, keepdims=True) + 1e-6)
        b_col = jnp.sum(jnp.where(lane == h, ba, 0.0), axis=-1, keepdims=True)
        a_col = jnp.sum(jnp.where(lane == h + GDN_HEADS, ba, 0.0), axis=-1, keepdims=True)
        beta = _sigmoid(b_col)
        g = -jnp.exp(alog_ref[h]) * _softplus(a_col + dtb_ref[h])
        g_b = g * jnp.ones((1, hd), F32)
        g_col = jnp.concatenate([_dot(lower, g_b[n * c:(n + 1) * c], mode=MODE_CUMSUM)
                                 for n in range(n_chunks)], axis=0)
        g_row = jnp.concatenate([_dot(jnp.ones((8, c), F32), g_b[n * c:(n + 1) * c, 0:c] * upper,
                                      mode=MODE_CUMSUM) for n in range(n_chunks)], axis=1)
        exp_g = jnp.exp(g_col)
        kb = k * beta
        vb = v * beta
        kbe = kb * exp_g
        qg = q * exp_g
        for n in range(n_chunks):
            rows = slice(n * c, (n + 1) * c)
            g_last = g_col[n * c + c - 1:n * c + c, :]
            gamma = jnp.exp(jnp.minimum(g_col[rows, 0:c] - g_row[0:1, n * c:(n + 1) * c], 0.0))
            units.append(dict(j=j, n=n, k=k[rows], kq=jnp.concatenate([kb[rows], q[rows]], axis=0),
                              rhs=jnp.concatenate([vb[rows], kbe[rows]], axis=1), qg=qg[rows],
                              kd=k[rows] * jnp.exp(g_last - g_col[rows]), decay=jnp.exp(g_last),
                              gamma2=jnp.concatenate([gamma, gamma], axis=0)))
    states = [state_ref[j] for j in range(GDN_GROUP)]
    outs = [[None] * n_chunks for _ in range(GDN_GROUP)]

    def prepare(group):
        for un in group:
            un["scores"] = jnp.where(band, _dot(un["kq"], un["k"], _NT) * un["gamma2"], 0.0)
        yield
        t_invs = []
        yield from _unit_lower_inverses([-un["scores"][0:c] for un in group], c, t_invs)
        for un, t_inv in zip(group, t_invs):
            un["uw"] = _dot(t_inv, un["rhs"], mode=MODE_SOLVE)
        yield
        for un in group:
            un["trans"] = _dot(un["kd"], un["uw"][:, hd:2 * hd], _TN, mode=MODE_STATE)
            un["const"] = _dot(un["kd"], un["uw"][:, 0:hd], _TN, mode=MODE_STATE)
        yield

    def chain(group):
        for n in sorted({un["n"] for un in group}):
            for un in group:
                if un["n"] == n:
                    j = un["j"]
                    un["s0"] = states[j]
                    states[j] = (states[j] * un["decay"] - _dot(un["trans"], states[j], mode=MODE_STATE)
                                 + un["const"])
            yield

    def outputs(group, chunks_per_stage):
        chunk_ids = sorted({un["n"] for un in group})
        for i in range(0, len(chunk_ids), chunks_per_stage):
            part = [un for un in group if un["n"] in chunk_ids[i:i + chunks_per_stage]]
            for un in part:
                un["ws"] = _dot(jnp.concatenate([un["uw"][:, hd:2 * hd], un["qg"]], axis=0), un["s0"],
                                mode=MODE_STATE)
            yield
            for un in part:
                v_new = un["uw"][:, 0:hd] - un["ws"][0:c]
                outs[un["j"]][un["n"]] = (un["ws"][c:2 * c]
                                          + _dot(un["scores"][c:2 * c], v_new, mode=MODE_STATE))
            yield

    _alternate(prepare(units))
    _alternate(chain(units))
    _alternate(outputs(units, n_chunks))
    for j in range(GDN_GROUP):
        state_ref[j] = states[j]
    for j in range(GDN_GROUP):
        cols = slice(j * hd, (j + 1) * hd)
        o = jnp.concatenate(outs[j], axis=0)
        o = o * lax.rsqrt(jnp.mean(o * o, axis=-1, keepdims=True) + RMS_EPS) * nw
        o_ref[:, cols] = (o * _silu(z_ref[:, cols])).astype(o_ref.dtype)


def _gdn(proj, conv_w, a_log, dt_bias, norm_w, tile=256):
    s = proj.shape[0]
    width = GDN_GROUP * GDN_DIM
    per = LANE // GDN_DIM * GDN_GROUP
    blk = lambda col: pl.BlockSpec((tile, width), lambda h, t: (t, col // per + h))
    cblk = lambda col: pl.BlockSpec((GDN_CONV, width), lambda h, t: (0, col // per + h))
    smem = pl.BlockSpec(memory_space=pltpu.SMEM)
    return pl.pallas_call(
        functools.partial(_gdn_body, tile=tile),
        grid=(GDN_HEADS // GDN_GROUP, s // tile),
        in_specs=[smem, smem,
                  blk(COL_GDN_Q), blk(COL_GDN_K), blk(COL_GDN_V), blk(COL_GDN_Z),
                  pl.BlockSpec((tile, LANE), lambda h, t: (t, COL_GDN_BA)),
                  cblk(0), cblk(GDN_HEADS), cblk(2 * GDN_HEADS),
                  pl.BlockSpec((1, GDN_DIM), lambda h, t: (0, 0))],
        out_specs=pl.BlockSpec((tile, width), lambda h, t: (t, h)),
        out_shape=jax.ShapeDtypeStruct((s, GDN_WIDTH), BF16),
        scratch_shapes=[pltpu.VMEM((3, tile + 8, width), F32),
                        pltpu.VMEM((GDN_GROUP, GDN_DIM, GDN_DIM), F32)],
        compiler_params=_cparams(("parallel", "arbitrary")),
        name="gdn",
    )(a_log, dt_bias, proj, proj, proj, proj, proj, conv_w, conv_w, conv_w, norm_w.reshape(1, GDN_DIM))


def _t5_bucket_table():
    exact = NUM_BUCKETS // 2
    i = np.arange(SWA_BLK)[:, None]
    j = np.arange(2 * SWA_BLK)[None, :]
    steps = np.maximum(i + SWA_BLK - j, 0)
    tables = []
    for _, dilation in SWA_PATTERNS:
        dist = steps * dilation
        d = np.maximum(dist, 1).astype(np.float32)
        ratio = (np.log(d / np.float32(exact)) / np.float32(math.log(MAX_DISTANCE / exact))
                 * np.float32(NUM_BUCKETS - exact)).astype(np.float32)
        log_b = exact + ratio.astype(np.int32)
        tables.append(np.where(dist < exact, dist, np.minimum(log_b, NUM_BUCKETS - 1)))
    return np.stack(tables).astype(np.int32)


def _swa_body(rb_ref, bkt_ref, q_ref, kc_ref, kp_ref, vc_ref, vp_ref, o_ref,
              bias_ref, kbuf, vbuf, o_scr, m_scr, l_scr, *, tile):
    h = pl.program_id(0)
    t = pl.program_id(1)
    blk = SWA_BLK

    @pl.when(t == 0)
    def _bias():
        for p in range(len(SWA_PATTERNS)):
            bkt = bkt_ref[p]
            bias = jnp.zeros((blk, 2 * blk), F32)
            for b in range(NUM_BUCKETS):
                bias = jnp.where(bkt == b, rb_ref[b, h], bias)
            bias_ref[p] = bias

    kbuf[0:tile, :] = kp_ref[...]
    kbuf[tile:2 * tile, :] = kc_ref[...]
    vbuf[0:tile, :] = vp_ref[...]
    vbuf[tile:2 * tile, :] = vc_ref[...]

    qi = lax.broadcasted_iota(jnp.int32, (blk, 2 * blk), 0)
    kj = lax.broadcasted_iota(jnp.int32, (blk, 2 * blk), 1)
    in_band = (kj >= qi) & (kj <= qi + blk)
    scale = SWA_DIM ** -0.5

    band_first = in_band & (kj >= jnp.where(t == 0, blk, 0))
    ones_row = jnp.ones((1, SWA_DIM), F32)

    for p, (window, dil) in enumerate(SWA_PATTERNS):
        span = blk * dil
        bias = bias_ref[p]
        blocks = [(res + n * span, n) for n in range(tile // span) for res in range(dil)]
        for g0 in range(0, len(blocks), SWA_GROUP):
            group = blocks[g0:g0 + SWA_GROUP]
            scores = []
            for start, n in group:
                qb = q_ref[pl.ds(start, blk, stride=dil), :] * scale
                kw = kbuf[pl.ds(tile + start - span, 2 * blk, stride=dil), :]
                sc = _dot(qb, kw, _NT) + bias
                scores.append(jnp.where(band_first if n == 0 else in_band, sc, NEG_INF))
            probs = []
            for sc in scores:
                m = jnp.max(sc, axis=-1, keepdims=True)
                pe = jnp.exp(sc - m)
                probs.append((pe, m, jnp.sum(pe, axis=-1, keepdims=True)))
            for (start, n), (pe, m, l) in zip(group, probs):
                vw = vbuf[pl.ds(tile + start - span, 2 * blk, stride=dil), :]
                o_scr[p, pl.ds(start, blk, stride=dil), :] = _dot(pe, vw)
                m_scr[p, pl.ds(start, blk, stride=dil), :] = m * ones_row
                l_scr[p, pl.ds(start, blk, stride=dil), :] = l * ones_row

    m_max = jnp.maximum(jnp.maximum(m_scr[0], m_scr[1]), m_scr[2])
    num = jnp.zeros((tile, SWA_DIM), F32)
    den = jnp.zeros((tile, SWA_DIM), F32)
    for p in range(len(SWA_PATTERNS)):
        sc = jnp.exp(m_scr[p] - m_max)
        num += o_scr[p] * sc
        den += l_scr[p] * sc
    o_ref[...] = (num / den).astype(o_ref.dtype)


def _swa(proj, rel_bias, tile=SWA_TILE):
    s = proj.shape[0]
    n_pat = len(SWA_PATTERNS)
    bkt = jnp.asarray(_t5_bucket_table())
    cur = lambda col: pl.BlockSpec((tile, LANE), lambda h, t: (t, col + h))
    prev = lambda col: pl.BlockSpec((tile, LANE), lambda h, t: (jnp.maximum(t - 1, 0), col + h))
    return pl.pallas_call(
        functools.partial(_swa_body, tile=tile),
        grid=(SWA_HEADS, s // tile),
        in_specs=[pl.BlockSpec(memory_space=pltpu.SMEM),
                  pl.BlockSpec((n_pat, SWA_BLK, 2 * SWA_BLK), lambda h, t: (0, 0, 0)),
                  cur(COL_SWA_Q), cur(COL_SWA_K), prev(COL_SWA_K), cur(COL_SWA_V), prev(COL_SWA_V)],
        out_specs=pl.BlockSpec((tile, LANE), lambda h, t: (t, h)),
        out_shape=jax.ShapeDtypeStruct((s, SWA_WIDTH), BF16),
        scratch_shapes=[pltpu.VMEM((n_pat, SWA_BLK, 2 * SWA_BLK), F32),
                        pltpu.VMEM((2 * tile, SWA_DIM), F32),
                        pltpu.VMEM((2 * tile, SWA_DIM), F32),
                        pltpu.VMEM((n_pat, tile, SWA_DIM), F32),
                        pltpu.VMEM((n_pat, tile, SWA_DIM), F32),
                        pltpu.VMEM((n_pat, tile, SWA_DIM), F32)],
        compiler_params=_cparams(("parallel", "arbitrary")),
        name="swa",
    )(rel_bias, bkt, proj, proj, proj, proj, proj)


PRM_W0, PRM_A0, PRM_KK, PRM_KA, PRM_RK, PRM_LNW, PRM_LNB, PRM_MUR, PRM_MUK, PRM_MUV = range(10)
PRM_ROWS = 16


LORA_IN_WIDTH = DECAY_LORA + ICL_LORA + GATE_LORA_PAD


def _rwkv_lora_in_body(wd_ref, ad_ref, gd_ref, mul_ref, mug_ref, o_ref, xb, xg, *, tile):
    @pl.when(pl.program_id(0) == 0)
    def _reset():
        xb[:, 0:8, :] = jnp.zeros((2, 8, LANE), F32)
        xg[0:8, :] = jnp.zeros((8, GATE_LORA_PAD), F32)

    def shifted(buf, src, mu):
        buf[8:8 + tile, :] = src[...]
        cur = buf[8:8 + tile, :]
        prev = buf[7:7 + tile, :]
        buf[0:8, :] = buf[tile:tile + 8, :]
        return cur + mu * (prev - cur)

    wd = shifted(xb.at[0], wd_ref, mul_ref[0:1, :])
    ad = shifted(xb.at[1], ad_ref, mul_ref[1:2, :])
    gd = shifted(xg, gd_ref, mug_ref[...])
    o_ref[:, 0:DECAY_LORA] = jnp.tanh(wd).astype(o_ref.dtype)
    o_ref[:, DECAY_LORA:DECAY_LORA + ICL_LORA] = ad.astype(o_ref.dtype)
    o_ref[:, DECAY_LORA + ICL_LORA:] = _sigmoid(gd).astype(o_ref.dtype)


def _rwkv_lora_in(proj, mu_lora, mu_gate, tile=1024):
    s = proj.shape[0]
    fixed = lambda col: pl.BlockSpec((tile, LANE), lambda t: (t, col))
    return pl.pallas_call(
        functools.partial(_rwkv_lora_in_body, tile=tile),
        grid=(s // tile,),
        in_specs=[fixed(COL_WD), fixed(COL_AD),
                  pl.BlockSpec((tile, GATE_LORA_PAD), lambda t: (t, COL_GD * LANE // GATE_LORA_PAD)),
                  pl.BlockSpec((8, LANE), lambda t: (0, 0)),
                  pl.BlockSpec((1, GATE_LORA_PAD), lambda t: (0, 0))],
        out_specs=pl.BlockSpec((tile, LORA_IN_WIDTH), lambda t: (t, 0)),
        out_shape=jax.ShapeDtypeStruct((s, LORA_IN_WIDTH), BF16),
        scratch_shapes=[pltpu.VMEM((2, tile + 8, LANE), F32),
                        pltpu.VMEM((tile + 8, GATE_LORA_PAD), F32)],
        compiler_params=_cparams(("arbitrary",)),
        name="rwkv_lora_in",
    )(proj, proj, proj, mu_lora, mu_gate)


def _rwkv_body(r_ref, k_ref, v_ref, lora_ref, prm_ref, wup_ref, aup_ref, gup_ref, o_ref, xb, state_ref,
               *, tile):
    c = RWKV_CHUNK
    nd = RWKV_DIM
    t = pl.program_id(1)

    @pl.when(t == 0)
    def _reset():
        state_ref[...] = jnp.zeros_like(state_ref)
        xb[:, 0:8, :] = jnp.zeros((3, 8, RWKV_GROUP * LANE), F32)

    prm = prm_ref[...]
    row = lambda i: prm[i:i + 1, :]

    def shifted(idx, src, mu):
        xb[idx, 8:8 + tile, :] = src[...]
        cur = xb[idx, 8:8 + tile, :]
        prev = xb[idx, 7:7 + tile, :]
        xb[idx, 0:8, :] = xb[idx, tile:tile + 8, :]
        return cur + mu * (prev - cur)

    r = shifted(0, r_ref, row(PRM_MUR))
    k = shifted(1, k_ref, row(PRM_MUK))
    v = shifted(2, v_ref, row(PRM_MUV))

    lora = lora_ref[...]
    w_log = -_softplus(-(row(PRM_W0) + _dot(lora[:, 0:DECAY_LORA], wup_ref[...]))) - 0.5
    log_w = -jnp.exp(w_log)
    a_gate = _sigmoid(row(PRM_A0) + _dot(lora[:, DECAY_LORA:DECAY_LORA + ICL_LORA], aup_ref[...]))
    g_gate = _dot(lora[:, DECAY_LORA + ICL_LORA:], gup_ref[...])
    kx = k * row(PRM_KK)
    k_mod = k * (1.0 + (a_gate - 1.0) * row(PRM_KA))

    ri = lax.broadcasted_iota(jnp.int32, (c, c), 0)
    ci = lax.broadcasted_iota(jnp.int32, (c, c), 1)
    lower = jnp.where(ci <= ri, 1.0, 0.0).astype(F32)
    cum = jnp.concatenate([_dot(lower, log_w[n * c:(n + 1) * c], mode=MODE_CUMSUM)
                           for n in range(tile // c)], axis=0)

    r2 = lax.broadcasted_iota(jnp.int32, (2 * c, 2 * c), 0)
    c2 = lax.broadcasted_iota(jnp.int32, (2 * c, 2 * c), 1)
    col_in = jnp.where(c2 < c, c2, c2 - c)
    band = col_in <= jnp.where(r2 < c, r2 - 1, r2 - c)

    width = RWKV_GROUP * LANE
    n_heads = width // nd
    n_chunks = tile // c
    lane = lax.broadcasted_iota(jnp.int32, (1, LANE), 1)
    head_masks = [(lane // nd) == hh for hh in range(LANE // nd)]

    def head_sum(x):
        blocks = []
        for b in range(RWKV_GROUP):
            xs = x[:, b * LANE:(b + 1) * LANE]
            out = jnp.zeros_like(xs)
            for msk in head_masks:
                out = jnp.where(msk, jnp.sum(jnp.where(msk, xs, 0.0), axis=-1, keepdims=True), out)
            blocks.append(out)
        return blocks[0] if RWKV_GROUP == 1 else jnp.concatenate(blocks, axis=1)

    kk = kx * lax.rsqrt(head_sum(kx * kx) + 1e-12)
    b_vec = kk * a_gate
    c_last = jnp.concatenate(
        [jnp.broadcast_to(cum[n * c + c - 1:n * c + c, :], (c, width)) for n in range(n_chunks)], axis=0)
    p_out = jnp.exp(-cum)
    a_t = -kk * jnp.exp(cum - log_w)
    r_t = r * jnp.exp(cum)
    k_t = k_mod * p_out
    b_t = b_vec * p_out
    decay_end = jnp.exp(c_last - cum)
    k_e = k_mod * decay_end
    b_e = b_vec * decay_end
    decay_all = jnp.exp(c_last)

    units = []
    for hh in range(n_heads):
        lanes = slice(hh * nd, (hh + 1) * nd)
        for n in range(n_chunks):
            rows = slice(n * c, (n + 1) * c)
            units.append(dict(hh=hh, n=n, a=a_t[rows, lanes], r=r_t[rows, lanes], v=v[rows, lanes],
                              ar=jnp.concatenate([a_t[rows, lanes], r_t[rows, lanes]], axis=0),
                              bk=jnp.concatenate([b_t[rows, lanes], k_t[rows, lanes]], axis=0),
                              bk_e=jnp.concatenate([b_e[rows, lanes], k_e[rows, lanes]], axis=0),
                              decay=decay_all[n * c:n * c + 1, lanes]))
    states = [state_ref[hh] for hh in range(n_heads)]
    ys = [[None] * n_chunks for _ in range(n_heads)]

    def prepare(group):
        for un in group:
            un["mm"] = jnp.where(band, _dot(un["ar"], un["bk"], _NT), 0.0)
        yield
        t_invs = []
        yield from _unit_lower_inverses([un["mm"][0:c, 0:c] for un in group], c, t_invs)
        for un in group:
            un["mv"] = _dot(un["mm"][0:c, c:2 * c], un["v"], mode=MODE_SOLVE)
        yield
        for un, t_inv in zip(group, t_invs):
            un["wu"] = _dot(t_inv, jnp.concatenate([un["a"], un["mv"]], axis=1), mode=MODE_SOLVE)
        yield
        for un in group:
            un["trans"] = _dot(un["wu"][:, 0:nd], un["bk_e"][0:c], _TN, mode=MODE_STATE)
            un["const"] = _dot(jnp.concatenate([un["wu"][:, nd:2 * nd], un["v"]], axis=0), un["bk_e"],
                               _TN, mode=MODE_STATE)
        yield

    def chain(group):
        for n in sorted({un["n"] for un in group}):
            for un in group:
                if un["n"] == n:
                    hh = un["hh"]
                    un["s0"] = states[hh]
                    states[hh] = (states[hh] * un["decay"]
                                  + _dot(states[hh], un["trans"], mode=MODE_STATE) + un["const"])
            yield

    def outputs(group, chunks_per_stage):
        chunk_ids = sorted({un["n"] for un in group})
        for i in range(0, len(chunk_ids), chunks_per_stage):
            part = [un for un in group if un["n"] in chunk_ids[i:i + chunks_per_stage]]
            for un in part:
                un["sr"] = _dot(jnp.concatenate([un["wu"][:, 0:nd], un["r"]], axis=0), un["s0"], _NT,
                                mode=MODE_STATE)
            yield
            for un in part:
                u = un["sr"][0:c] + un["wu"][:, nd:2 * nd]
                uv = jnp.concatenate([u, un["v"]], axis=0)
                ys[un["hh"]][un["n"]] = un["sr"][c:2 * c] + _dot(un["mm"][c:2 * c], uv, mode=MODE_STATE)
            yield

    _alternate(prepare(units))
    _alternate(chain(units))
    _alternate(outputs(units, n_chunks))
    for hh in range(n_heads):
        state_ref[hh] = states[hh]
    y = jnp.concatenate([jnp.concatenate(ys[hh], axis=0) for hh in range(n_heads)], axis=1)
    mean = head_sum(y) * (1.0 / nd)
    var = head_sum(jnp.square(y - mean)) * (1.0 / nd)
    y_n = (y - mean) * lax.rsqrt(var + GN_EPS) * row(PRM_LNW) + row(PRM_LNB)
    bonus = head_sum(r * k_mod * row(PRM_RK)) * v
    o_ref[...] = ((y_n + bonus) * g_gate).astype(o_ref.dtype)


def _rwkv(proj, prm, mu_lora, mu_gate, w_up, a_up, g_up, tile=512):
    s = proj.shape[0]
    width = RWKV_GROUP * LANE
    groups = RWKV_WIDTH // width
    lora_in = _rwkv_lora_in(proj, mu_lora, mu_gate)
    blk = lambda col: pl.BlockSpec((tile, width), lambda h, t: (t, col // RWKV_GROUP + h))
    return pl.pallas_call(
        functools.partial(_rwkv_body, tile=tile),
        grid=(groups, s // tile),
        in_specs=[blk(COL_R), blk(COL_K), blk(COL_V),
                  pl.BlockSpec((tile, LORA_IN_WIDTH), lambda h, t: (t, 0)),
                  pl.BlockSpec((PRM_ROWS, width), lambda h, t: (0, h)),
                  pl.BlockSpec((DECAY_LORA, width), lambda h, t: (0, h)),
                  pl.BlockSpec((ICL_LORA, width), lambda h, t: (0, h)),
                  pl.BlockSpec((GATE_LORA_PAD, width), lambda h, t: (0, h))],
        out_specs=pl.BlockSpec((tile, width), lambda h, t: (t, h)),
        out_shape=jax.ShapeDtypeStruct((s, RWKV_WIDTH), BF16),
        scratch_shapes=[pltpu.VMEM((3, tile + 8, width), F32),
                        pltpu.VMEM((width // RWKV_DIM, RWKV_DIM, RWKV_DIM), F32)],
        compiler_params=_cparams(("parallel", "arbitrary")),
        name="rwkv7",
    )(proj, proj, proj, lora_in, prm, w_up, a_up, g_up)


def _layout_rwkv_params(mu, w0, a0, k_k, k_a, r_k, ln_w, ln_b):
    rw = RWKV_WIDTH
    mu_r, mu_k, mu_v = mu[0:rw], mu[rw:2 * rw], mu[2 * rw:3 * rw]
    o = 3 * rw
    mu_wd = mu[o:o + DECAY_LORA]; o += DECAY_LORA
    mu_ad = mu[o:o + ICL_LORA]; o += ICL_LORA
    mu_gd = mu[o:o + GATE_LORA]
    rows = [w0, a0, k_k, k_a, r_k.reshape(rw), ln_w, ln_b, mu_r, mu_k, mu_v]
    prm = jnp.stack(rows + [jnp.zeros((rw,), F32)] * (PRM_ROWS - len(rows))).astype(F32)
    mu_lora = jnp.stack([mu_wd, mu_ad] + [jnp.zeros((LANE,), F32)] * 6).astype(F32)
    mu_gate = jnp.concatenate([mu_gd, jnp.zeros((GATE_LORA_PAD - GATE_LORA,), F32)]).reshape(1, GATE_LORA_PAD)
    return prm, mu_lora, mu_gate


def kernel(x, attn_norm, w_in, gdn_conv, gdn_a_log, gdn_dt_bias, gdn_norm, rwkv_mu, rwkv_w0, rwkv_w_up, rwkv_a0, rwkv_a_up, rwkv_g_up, rwkv_k_k, rwkv_k_a, rwkv_r_k, rwkv_ln_w, rwkv_ln_b, w_out, ffn_norm, w_ffn_gate, w_ffn_up, ffn_conv, ffn_conv_b, w_ffn_down, rel_bias, final_norm):
    batch, seq, d = x.shape
    depth = w_in.shape[0]
    outs = []
    w_in_bf16 = w_in.astype(BF16)
    rows_in = [x.reshape(seq, d)] if batch == 1 else [x[b] for b in range(batch)]
    for xb in rows_in:
        for l in range(depth):
            proj = _in_proj(xb, attn_norm[l], _layout_w_in_pallas(w_in_bf16, l))
            o_a = _gdn(proj, gdn_conv[l], gdn_a_log[l], gdn_dt_bias[l], gdn_norm[l])
            o_b = _swa(proj, rel_bias)
            prm, mu_lora, mu_gate = _layout_rwkv_params(
                rwkv_mu[l], rwkv_w0[l], rwkv_a0[l], rwkv_k_k[l], rwkv_k_a[l], rwkv_r_k[l],
                rwkv_ln_w[l], rwkv_ln_b[l])
            g_up = jnp.concatenate(
                [rwkv_g_up[l], jnp.zeros((GATE_LORA_PAD - GATE_LORA, RWKV_WIDTH), F32)], axis=0)
            o_c = _rwkv(proj, prm, mu_lora, mu_gate, rwkv_w_up[l].astype(BF16),
                        rwkv_a_up[l].astype(BF16), g_up.astype(BF16))
            wo = _to_bf16(w_out, l)
            xb = _out_proj(xb, o_a, o_b, o_c, wo[0:GDN_WIDTH], wo[GDN_WIDTH:GDN_WIDTH + SWA_WIDTH],
                           wo[GDN_WIDTH + SWA_WIDTH:])
            act = _ffn_up(xb, ffn_norm[l], _to_bf16(w_ffn_gate, l), _to_bf16(w_ffn_up, l), ffn_conv[l],
                          ffn_conv_b[l])
            xb = _ffn_down(xb, act, _to_bf16(w_ffn_down, l))
        outs.append(_rmsnorm(xb, final_norm, x.dtype))
    return outs[0].reshape(1, seq, d) if batch == 1 else jnp.stack(outs)
```

```python
import functools
import math

import numpy as np
import jax
import jax.numpy as jnp
from jax import lax
from jax.experimental import pallas as pl
from jax.experimental.pallas import tpu as pltpu

F32 = jnp.float32
BF16 = jnp.bfloat16
HIGHEST = lax.Precision.HIGHEST

LANE = 128
D_MODEL = 4096
RMS_EPS = 1e-6
GN_EPS = 64e-5
NEG_INF = -1e30
VMEM_LIMIT = 56 * 1024 * 1024

GDN_DIM = 128
GDN_HEADS = 12
GDN_WIDTH = GDN_HEADS * GDN_DIM
GDN_CONV = 4
GDN_CHUNK = 64
GDN_GROUP = 6
SWA_DIM = 128
SWA_HEADS = 8
SWA_WIDTH = SWA_HEADS * SWA_DIM
SWA_PATTERNS = ((128, 1), (512, 4), (2048, 16))
SWA_BLK = 128
SWA_TILE = 2048
SWA_GROUP = 4
NUM_BUCKETS = 32
MAX_DISTANCE = 2048
RWKV_DIM = 64
RWKV_HEADS = 24
RWKV_WIDTH = RWKV_HEADS * RWKV_DIM
RWKV_CHUNK = 64
RWKV_GROUP = 1
DECAY_LORA = 128
ICL_LORA = 128
GATE_LORA = 480
GATE_LORA_PAD = 512
D_FF = 11008
FFN_CONV = 3

COL_GDN_Q, COL_GDN_K, COL_GDN_V, COL_GDN_Z, COL_GDN_BA = 0, 12, 24, 36, 48
COL_SWA_Q, COL_SWA_K, COL_SWA_V = 49, 57, 65
COL_WD, COL_AD = 73, 74
COL_GD = 76
COL_R, COL_K, COL_V = 80, 92, 104
PROJ_BLOCKS = 116
PROJ_WIDTH = PROJ_BLOCKS * LANE


def _cparams(sem):
    return pltpu.CompilerParams(dimension_semantics=sem, vmem_limit_bytes=VMEM_LIMIT)


def _sigmoid(x):
    return 0.5 + 0.5 * jnp.tanh(0.5 * x)


def _silu(x):
    return x * _sigmoid(x)


def _softplus(x):
    return jnp.maximum(x, 0.0) + jnp.log(1.0 + jnp.exp(-jnp.abs(x)))


def _split_bf16(x):
    hi = x.astype(BF16)
    return hi, (x - hi.astype(F32)).astype(BF16)


def _dot(a, b, dims=(((1,), (0,)), ((), ())), mode="bf16"):
    if mode == "f32":
        return lax.dot_general(a.astype(F32), b.astype(F32), dims, precision=HIGHEST,
                               preferred_element_type=F32)
    if mode == "x3":
        a_hi, a_lo = _split_bf16(a)
        b_hi, b_lo = _split_bf16(b)
        mm = lambda p, q: lax.dot_general(p, q, dims, preferred_element_type=F32)
        return mm(a_hi, b_hi) + (mm(a_hi, b_lo) + mm(a_lo, b_hi))
    return lax.dot_general(a.astype(BF16), b.astype(BF16), dims, preferred_element_type=F32)


MODE_CUMSUM = "f32"
MODE_INV = "bf16"
MODE_SOLVE = "bf16"
MODE_STATE = "bf16"


_NT = (((1,), (1,)), ((), ()))
_TN = (((0,), (0,)), ((), ()))


def _rmsnorm_body(x_ref, w_ref, o_ref):
    x = x_ref[...]
    ms = jnp.mean(x * x, axis=-1, keepdims=True)
    o_ref[...] = (x * lax.rsqrt(ms + RMS_EPS) * w_ref[...]).astype(o_ref.dtype)


def _rmsnorm(x, w, out_dtype, tm=512):
    s, d = x.shape
    return pl.pallas_call(
        _rmsnorm_body,
        grid=(s // tm,),
        in_specs=[pl.BlockSpec((tm, d), lambda i: (i, 0)),
                  pl.BlockSpec((1, d), lambda i: (0, 0))],
        out_specs=pl.BlockSpec((tm, d), lambda i: (i, 0)),
        out_shape=jax.ShapeDtypeStruct((s, d), out_dtype),
        compiler_params=_cparams(("parallel",)),
        name="rmsnorm",
    )(x, w.reshape(1, d))


def _cast_body(w_ref, o_ref):
    o_ref[...] = w_ref[...].astype(o_ref.dtype)


def _to_bf16(w, layer, rows=256):
    _, r, c = w.shape
    return pl.pallas_call(
        _cast_body,
        grid=(r // rows,),
        in_specs=[pl.BlockSpec((None, rows, c), lambda i: (layer, i, 0))],
        out_specs=pl.BlockSpec((rows, c), lambda i: (i, 0)),
        out_shape=jax.ShapeDtypeStruct((r, c), BF16),
        compiler_params=_cparams(("parallel",)),
        name="to_bf16",
    )(w)


def _w_in_segments():
    gw, sw, rw = GDN_WIDTH, SWA_WIDTH, RWKV_WIDTH
    src = 0
    segs = []
    for width, dst in ((4 * gw, COL_GDN_Q), (2 * GDN_HEADS, COL_GDN_BA), (3 * sw, COL_SWA_Q),
                       (3 * rw, COL_R), (DECAY_LORA, COL_WD), (ICL_LORA, COL_AD), (GATE_LORA, COL_GD)):
        segs.append((src, width, dst * LANE))
        src += width
    return segs, src


def _layout_w_in_body(w_ref, o_ref):
    o_ref[...] = jnp.zeros_like(o_ref)
    for src, width, dst in _w_in_segments()[0]:
        o_ref[:, dst:dst + width] = w_ref[:, src:src + width].astype(o_ref.dtype)


def _layout_w_in_pallas(w, layer, rows=32):
    _, r, c = w.shape
    assert c == _w_in_segments()[1]
    return pl.pallas_call(
        _layout_w_in_body,
        grid=(r // rows,),
        in_specs=[pl.BlockSpec((None, rows, c), lambda i: (layer, i, 0))],
        out_specs=pl.BlockSpec((rows, PROJ_WIDTH), lambda i: (i, 0)),
        out_shape=jax.ShapeDtypeStruct((r, PROJ_WIDTH), BF16),
        compiler_params=_cparams(("parallel",)),
        name="layout_w_in",
    )(w)


NORM_ROWS = 16


def _stage_rmsnorm(x_ref, nw_ref, dst_ref, dst_row0, rows):
    nw = nw_ref[...]
    for r0 in range(0, rows, NORM_ROWS):
        n = min(NORM_ROWS, rows - r0)
        xs = x_ref[r0:r0 + n, :]
        ms = jnp.mean(xs * xs, axis=-1, keepdims=True)
        dst_ref[dst_row0 + r0:dst_row0 + r0 + n, :] = (xs * lax.rsqrt(ms + RMS_EPS) * nw).astype(dst_ref.dtype)


def _in_proj_body(x_ref, nw_ref, b_ref, o_ref, hbuf, *, tm):
    @pl.when(pl.program_id(1) == 0)
    def _stage():
        _stage_rmsnorm(x_ref, nw_ref, hbuf, 0, tm)

    o_ref[...] = jnp.dot(hbuf[...], b_ref[...], preferred_element_type=F32)


def _in_proj(x, norm_w, w, tm=1024, tn=512):
    s, k = x.shape
    n = w.shape[1]
    return pl.pallas_call(
        functools.partial(_in_proj_body, tm=tm),
        grid=(s // tm, n // tn),
        in_specs=[pl.BlockSpec((tm, k), lambda i, j: (i, 0)),
                  pl.BlockSpec((1, k), lambda i, j: (0, 0)),
                  pl.BlockSpec((k, tn), lambda i, j: (0, j))],
        out_specs=pl.BlockSpec((tm, tn), lambda i, j: (i, j)),
        out_shape=jax.ShapeDtypeStruct((s, n), F32),
        scratch_shapes=[pltpu.VMEM((tm, k), BF16)],
        compiler_params=_cparams(("parallel", "arbitrary")),
        name="in_proj",
    )(x, norm_w.reshape(1, k), w)


def _out_proj_body(x_ref, a_ref, b_ref, c_ref, wa_ref, wb_ref, wc_ref, o_ref):
    acc = jnp.dot(a_ref[...], wa_ref[...], preferred_element_type=F32)
    acc += jnp.dot(b_ref[...], wb_ref[...], preferred_element_type=F32)
    acc += jnp.dot(c_ref[...], wc_ref[...], preferred_element_type=F32)
    o_ref[...] = x_ref[...] + acc


def _out_proj(x, oa, ob, oc, wa, wb, wc, tm=1024, tn=512):
    s, d = x.shape
    ka, kb, kc = oa.shape[1], ob.shape[1], oc.shape[1]
    return pl.pallas_call(
        _out_proj_body,
        grid=(s // tm, d // tn),
        in_specs=[pl.BlockSpec((tm, tn), lambda i, j: (i, j)),
                  pl.BlockSpec((tm, ka), lambda i, j: (i, 0)),
                  pl.BlockSpec((tm, kb), lambda i, j: (i, 0)),
                  pl.BlockSpec((tm, kc), lambda i, j: (i, 0)),
                  pl.BlockSpec((ka, tn), lambda i, j: (0, j)),
                  pl.BlockSpec((kb, tn), lambda i, j: (0, j)),
                  pl.BlockSpec((kc, tn), lambda i, j: (0, j))],
        out_specs=pl.BlockSpec((tm, tn), lambda i, j: (i, j)),
        out_shape=jax.ShapeDtypeStruct((s, d), F32),
        compiler_params=_cparams(("parallel", "arbitrary")),
        name="out_proj",
    )(x, oa, ob, oc, wa, wb, wc)


FFN_HALO = 16


def _ffn_up_body(x_ref, halo_ref, nw_ref, wg_ref, wu_ref, cw_ref, cb_ref, o_ref, hbuf, *, tm):
    i = pl.program_id(0)

    @pl.when(pl.program_id(1) == 0)
    def _stage():
        _stage_rmsnorm(halo_ref, nw_ref, hbuf, 0, FFN_HALO)
        _stage_rmsnorm(x_ref, nw_ref, hbuf, FFN_HALO, tm)

    @pl.when(jnp.logical_and(pl.program_id(1) == 0, i == 0))
    def _no_history():
        hbuf[0:FFN_HALO, :] = jnp.zeros((FFN_HALO, hbuf.shape[1]), hbuf.dtype)

    g = jnp.dot(hbuf[...], wg_ref[...], preferred_element_type=F32)
    u = jnp.dot(hbuf[FFN_HALO:FFN_HALO + tm, :], wu_ref[...], preferred_element_type=F32)
    cw = cw_ref[...]
    conv = (cw[0:1, :] * pltpu.roll(g, 2, 0)[FFN_HALO:, :]
            + cw[1:2, :] * pltpu.roll(g, 1, 0)[FFN_HALO:, :]
            + cw[2:3, :] * g[FFN_HALO:, :]) + cb_ref[...]
    o_ref[...] = (_silu(conv) * u).astype(o_ref.dtype)


def _ffn_up(x, norm_w, wg, wu, cw, cb, tm=1024, tn=256):
    s, k = x.shape
    f = wg.shape[1]
    halo_blocks = tm // FFN_HALO
    return pl.pallas_call(
        functools.partial(_ffn_up_body, tm=tm),
        grid=(s // tm, f // tn),
        in_specs=[pl.BlockSpec((tm, k), lambda i, j: (i, 0)),
                  pl.BlockSpec((FFN_HALO, k), lambda i, j: (jnp.maximum(i * halo_blocks - 1, 0), 0)),
                  pl.BlockSpec((1, k), lambda i, j: (0, 0)),
                  pl.BlockSpec((k, tn), lambda i, j: (0, j)),
                  pl.BlockSpec((k, tn), lambda i, j: (0, j)),
                  pl.BlockSpec((FFN_CONV, tn), lambda i, j: (0, j)),
                  pl.BlockSpec((1, tn), lambda i, j: (0, j))],
        out_specs=pl.BlockSpec((tm, tn), lambda i, j: (i, j)),
        out_shape=jax.ShapeDtypeStruct((s, f), BF16),
        scratch_shapes=[pltpu.VMEM((tm + FFN_HALO, k), BF16)],
        compiler_params=_cparams(("parallel", "arbitrary")),
        name="ffn_up",
    )(x, x, norm_w.reshape(1, k), wg, wu, cw, cb.reshape(1, f))


def _ffn_down_body(x_ref, a_ref, w_ref, o_ref, acc_ref):
    kk = pl.program_id(2)

    @pl.when(kk == 0)
    def _init():
        acc_ref[...] = x_ref[...]

    acc_ref[...] += jnp.dot(a_ref[...], w_ref[...], preferred_element_type=F32)

    @pl.when(kk == pl.num_programs(2) - 1)
    def _done():
        o_ref[...] = acc_ref[...]


def _ffn_down(x, act, w, tm=512, tn=512, ksplit=1):
    s, d = x.shape
    f = act.shape[1]
    tk = f // ksplit
    return pl.pallas_call(
        _ffn_down_body,
        grid=(s // tm, d // tn, ksplit),
        in_specs=[pl.BlockSpec((tm, tn), lambda i, j, kk: (i, j)),
                  pl.BlockSpec((tm, tk), lambda i, j, kk: (i, kk)),
                  pl.BlockSpec((tk, tn), lambda i, j, kk: (kk, j))],
        out_specs=pl.BlockSpec((tm, tn), lambda i, j, kk: (i, j)),
        out_shape=jax.ShapeDtypeStruct((s, d), F32),
        scratch_shapes=[pltpu.VMEM((tm, tn), F32)],
        compiler_params=_cparams(("parallel", "arbitrary", "arbitrary")),
        name="ffn_down",
    )(x, act, w)


def _unit_lower_inverses(n_mats, size, out):
    row = lax.broadcasted_iota(jnp.int32, (size, size), 0)
    col = lax.broadcasted_iota(jnp.int32, (size, size), 1)
    eye = jnp.where(row == col, 1.0, 0.0).astype(F32)
    invs = [eye + n for n in n_mats]
    powers = [_dot(n, n, mode=MODE_INV) for n in n_mats]
    yield
    steps = int(math.log2(size)) - 1
    for s in range(steps):
        last = s == steps - 1
        for i, (inv, p) in enumerate(zip(invs, powers)):
            if last:
                invs[i] = inv + _dot(inv, p, mode=MODE_INV)
            else:
                both = _dot(jnp.concatenate([inv, p], axis=0), p, mode=MODE_INV)
                invs[i] = inv + both[0:size]
                powers[i] = both[size:2 * size]
        yield
    out.extend(invs)


def _alternate(*stage_generators):
    live = list(stage_generators)
    while live:
        for gen in list(live):
            try:
                next(gen)
            except StopIteration:
                live.remove(gen)


def _gdn_body(alog_ref, dtb_ref, q_ref, k_ref, v_ref, z_ref, ba_ref, cq_ref, ck_ref, cv_ref,
              nw_ref, o_ref, xbuf, state_ref, *, tile):
    c = GDN_CHUNK
    hd = GDN_DIM
    width = GDN_GROUP * hd
    n_chunks = tile // c
    t = pl.program_id(1)

    @pl.when(t == 0)
    def _reset():
        state_ref[...] = jnp.zeros_like(state_ref)
        xbuf[:, 0:8, :] = jnp.zeros((3, 8, width), F32)

    convs = []
    for idx, (src, cw_ref) in enumerate(((q_ref, cq_ref), (k_ref, ck_ref), (v_ref, cv_ref))):
        xbuf[idx, 8:8 + tile, :] = src[...]
        cw = cw_ref[...]
        acc = cw[0:1, :] * xbuf[idx, 5:5 + tile, :]
        acc += cw[1:2, :] * xbuf[idx, 6:6 + tile, :]
        acc += cw[2:3, :] * xbuf[idx, 7:7 + tile, :]
        acc += cw[3:4, :] * xbuf[idx, 8:8 + tile, :]
        convs.append(_silu(acc))
        xbuf[idx, 0:8, :] = xbuf[idx, tile:tile + 8, :]
    q_all, k_all, v_all = convs

    ba = ba_ref[...]
    lane = lax.broadcasted_iota(jnp.int32, (1, LANE), 1)
    ri = lax.broadcasted_iota(jnp.int32, (c, c), 0)
    ci = lax.broadcasted_iota(jnp.int32, (c, c), 1)
    lower = jnp.where(ci <= ri, 1.0, 0.0).astype(F32)
    upper = jnp.where(ri <= ci, 1.0, 0.0).astype(F32)
    r2 = lax.broadcasted_iota(jnp.int32, (2 * c, c), 0)
    c2 = lax.broadcasted_iota(jnp.int32, (2 * c, c), 1)
    band = c2 <= jnp.where(r2 < c, r2 - 1, r2 - c)
    nw = nw_ref[...]

    units = []
    for j in range(GDN_GROUP):
        h = pl.program_id(0) * GDN_GROUP + j
        cols = slice(j * hd, (j + 1) * hd)
        q, k, v = q_all[:, cols], k_all[:, cols], v_all[:, cols]
        q = q * lax.rsqrt(jnp.sum(q * q, axis=-1, keepdims=True) + 1e-6) * (hd ** -0.5)
        k = k * lax.rsqrt(jnp.sum(k * k, axis=-1, keepdims=True) + 1e-6)
        b_col = jnp.sum(jnp.where(lane == h, ba, 0.0), axis=-1, keepdims=True)
        a_col = jnp.sum(jnp.where(lane == h + GDN_HEADS, ba, 0.0), axis=-1, keepdims=True)
        beta = _sigmoid(b_col)
        g = -jnp.exp(alog_ref[h]) * _softplus(a_col + dtb_ref[h])
        g_b = g * jnp.ones((1, hd), F32)
        g_col = jnp.concatenate([_dot(lower, g_b[n * c:(n + 1) * c], mode=MODE_CUMSUM)
                                 for n in range(n_chunks)], axis=0)
        g_row = jnp.concatenate([_dot(jnp.ones((8, c), F32), g_b[n * c:(n + 1) * c, 0:c] * upper,
                                      mode=MODE_CUMSUM) for n in range(n_chunks)], axis=1)
        exp_g = jnp.exp(g_col)
        kb = k * beta
        vb = v * beta
        kbe = kb * exp_g
        qg = q * exp_g
        for n in range(n_chunks):
            rows = slice(n * c, (n + 1) * c)
            g_last = g_col[n * c + c - 1:n * c + c, :]
            gamma = jnp.exp(jnp.minimum(g_col[rows, 0:c] - g_row[0:1, n * c:(n + 1) * c], 0.0))
            units.append(dict(j=j, n=n, k=k[rows], kq=jnp.concatenate([kb[rows], q[rows]], axis=0),
                              rhs=jnp.concatenate([vb[rows], kbe[rows]], axis=1), qg=qg[rows],
                              kd=k[rows] * jnp.exp(g_last - g_col[rows]), decay=jnp.exp(g_last),
                              gamma2=jnp.concatenate([gamma, gamma], axis=0)))
    states = [state_ref[j] for j in range(GDN_GROUP)]
    outs = [[None] * n_chunks for _ in range(GDN_GROUP)]

    def prepare(group):
        for un in group:
            un["scores"] = jnp.where(band, _dot(un["kq"], un["k"], _NT) * un["gamma2"], 0.0)
        yield
        t_invs = []
        yield from _unit_lower_inverses([-un["scores"][0:c] for un in group], c, t_invs)
        for un, t_inv in zip(group, t_invs):
            un["uw"] = _dot(t_inv, un["rhs"], mode=MODE_SOLVE)
        yield
        for un in group:
            un["trans"] = _dot(un["kd"], un["uw"][:, hd:2 * hd], _TN, mode=MODE_STATE)
            un["const"] = _dot(un["kd"], un["uw"][:, 0:hd], _TN, mode=MODE_STATE)
        yield

    def chain(group):
        for n in sorted({un["n"] for un in group}):
            for un in group:
                if un["n"] == n:
                    j = un["j"]
                    un["s0"] = states[j]
                    states[j] = (states[j] * un["decay"] - _dot(un["trans"], states[j], mode=MODE_STATE)
                                 + un["const"])
            yield

    def outputs(group, chunks_per_stage):
        chunk_ids = sorted({un["n"] for un in group})
        for i in range(0, len(chunk_ids), chunks_per_stage):
            part = [un for un in group if un["n"] in chunk_ids[i:i + chunks_per_stage]]
            for un in part:
                un["ws"] = _dot(jnp.concatenate([un["uw"][:, hd:2 * hd], un["qg"]], axis=0), un["s0"],
                                mode=MODE_STATE)
            yield
            for un in part:
                v_new = un["uw"][:, 0:hd] - un["ws"][0:c]
                outs[un["j"]][un["n"]] = (un["ws"][c:2 * c]
                                          + _dot(un["scores"][c:2 * c], v_new, mode=MODE_STATE))
            yield

    _alternate(prepare(units))
    _alternate(chain(units))
    _alternate(outputs(units, n_chunks))
    for j in range(GDN_GROUP):
        state_ref[j] = states[j]
    for j in range(GDN_GROUP):
        cols = slice(j * hd, (j + 1) * hd)
        o = jnp.concatenate(outs[j], axis=0)
        o = o * lax.rsqrt(jnp.mean(o * o, axis=-1, keepdims=True) + RMS_EPS) * nw
        o_ref[:, cols] = (o * _silu(z_ref[:, cols])).astype(o_ref.dtype)


def _gdn(proj, conv_w, a_log, dt_bias, norm_w, tile=256):
    s = proj.shape[0]
    width = GDN_GROUP * GDN_DIM
    per = LANE // GDN_DIM * GDN_GROUP
    blk = lambda col: pl.BlockSpec((tile, width), lambda h, t: (t, col // per + h))
    cblk = lambda col: pl.BlockSpec((GDN_CONV, width), lambda h, t: (0, col // per + h))
    smem = pl.BlockSpec(memory_space=pltpu.SMEM)
    return pl.pallas_call(
        functools.partial(_gdn_body, tile=tile),
        grid=(GDN_HEADS // GDN_GROUP, s // tile),
        in_specs=[smem, smem,
                  blk(COL_GDN_Q), blk(COL_GDN_K), blk(COL_GDN_V), blk(COL_GDN_Z),
                  pl.BlockSpec((tile, LANE), lambda h, t: (t, COL_GDN_BA)),
                  cblk(0), cblk(GDN_HEADS), cblk(2 * GDN_HEADS),
                  pl.BlockSpec((1, GDN_DIM), lambda h, t: (0, 0))],
        out_specs=pl.BlockSpec((tile, width), lambda h, t: (t, h)),
        out_shape=jax.ShapeDtypeStruct((s, GDN_WIDTH), BF16),
        scratch_shapes=[pltpu.VMEM((3, tile + 8, width), F32),
                        pltpu.VMEM((GDN_GROUP, GDN_DIM, GDN_DIM), F32)],
        compiler_params=_cparams(("parallel", "arbitrary")),
        name="gdn",
    )(a_log, dt_bias, proj, proj, proj, proj, proj, conv_w, conv_w, conv_w, norm_w.reshape(1, GDN_DIM))


def _t5_bucket_table():
    exact = NUM_BUCKETS // 2
    i = np.arange(SWA_BLK)[:, None]
    j = np.arange(2 * SWA_BLK)[None, :]
    steps = np.maximum(i + SWA_BLK - j, 0)
    tables = []
    for _, dilation in SWA_PATTERNS:
        dist = steps * dilation
        d = np.maximum(dist, 1).astype(np.float32)
        ratio = (np.log(d / np.float32(exact)) / np.float32(math.log(MAX_DISTANCE / exact))
                 * np.float32(NUM_BUCKETS - exact)).astype(np.float32)
        log_b = exact + ratio.astype(np.int32)
        tables.append(np.where(dist < exact, dist, np.minimum(log_b, NUM_BUCKETS - 1)))
    return np.stack(tables).astype(np.int32)


def _swa_body(rb_ref, bkt_ref, q_ref, kc_ref, kp_ref, vc_ref, vp_ref, o_ref,
              bias_ref, kbuf, vbuf, o_scr, m_scr, l_scr, *, tile):
    h = pl.program_id(0)
    t = pl.program_id(1)
    blk = SWA_BLK

    @pl.when(t == 0)
    def _bias():
        for p in range(len(SWA_PATTERNS)):
            bkt = bkt_ref[p]
            bias = jnp.zeros((blk, 2 * blk), F32)
            for b in range(NUM_BUCKETS):
                bias = jnp.where(bkt == b, rb_ref[b, h], bias)
            bias_ref[p] = bias

    kbuf[0:tile, :] = kp_ref[...]
    kbuf[tile:2 * tile, :] = kc_ref[...]
    vbuf[0:tile, :] = vp_ref[...]
    vbuf[tile:2 * tile, :] = vc_ref[...]

    qi = lax.broadcasted_iota(jnp.int32, (blk, 2 * blk), 0)
    kj = lax.broadcasted_iota(jnp.int32, (blk, 2 * blk), 1)
    in_band = (kj >= qi) & (kj <= qi + blk)
    scale = SWA_DIM ** -0.5

    band_first = in_band & (kj >= jnp.where(t == 0, blk, 0))
    ones_row = jnp.ones((1, SWA_DIM), F32)

    for p, (window, dil) in enumerate(SWA_PATTERNS):
        span = blk * dil
        bias = bias_ref[p]
        blocks = [(res + n * span, n) for n in range(tile // span) for res in range(dil)]
        for g0 in range(0, len(blocks), SWA_GROUP):
            group = blocks[g0:g0 + SWA_GROUP]
            scores = []
            for start, n in group:
                qb = q_ref[pl.ds(start, blk, stride=dil), :] * scale
                kw = kbuf[pl.ds(tile + start - span, 2 * blk, stride=dil), :]
                sc = _dot(qb, kw, _NT) + bias
                scores.append(jnp.where(band_first if n == 0 else in_band, sc, NEG_INF))
            probs = []
            for sc in scores:
                m = jnp.max(sc, axis=-1, keepdims=True)
                pe = jnp.exp(sc - m)
                probs.append((pe, m, jnp.sum(pe, axis=-1, keepdims=True)))
            for (start, n), (pe, m, l) in zip(group, probs):
                vw = vbuf[pl.ds(tile + start - span, 2 * blk, stride=dil), :]
                o_scr[p, pl.ds(start, blk, stride=dil), :] = _dot(pe, vw)
                m_scr[p, pl.ds(start, blk, stride=dil), :] = m * ones_row
                l_scr[p, pl.ds(start, blk, stride=dil), :] = l * ones_row

    m_max = jnp.maximum(jnp.maximum(m_scr[0], m_scr[1]), m_scr[2])
    num = jnp.zeros((tile, SWA_DIM), F32)
    den = jnp.zeros((tile, SWA_DIM), F32)
    for p in range(len(SWA_PATTERNS)):
        sc = jnp.exp(m_scr[p] - m_max)
        num += o_scr[p] * sc
        den += l_scr[p] * sc
    o_ref[...] = (num / den).astype(o_ref.dtype)


def _swa(proj, rel_bias, tile=SWA_TILE):
    s = proj.shape[0]
    n_pat = len(SWA_PATTERNS)
    bkt = jnp.asarray(_t5_bucket_table())
    cur = lambda col: pl.BlockSpec((tile, LANE), lambda h, t: (t, col + h))
    prev = lambda col: pl.BlockSpec((tile, LANE), lambda h, t: (jnp.maximum(t - 1, 0), col + h))
    return pl.pallas_call(
        functools.partial(_swa_body, tile=tile),
        grid=(SWA_HEADS, s // tile),
        in_specs=[pl.BlockSpec(memory_space=pltpu.SMEM),
                  pl.BlockSpec((n_pat, SWA_BLK, 2 * SWA_BLK), lambda h, t: (0, 0, 0)),
                  cur(COL_SWA_Q), cur(COL_SWA_K), prev(COL_SWA_K), cur(COL_SWA_V), prev(COL_SWA_V)],
        out_specs=pl.BlockSpec((tile, LANE), lambda h, t: (t, h)),
        out_shape=jax.ShapeDtypeStruct((s, SWA_WIDTH), BF16),
        scratch_shapes=[pltpu.VMEM((n_pat, SWA_BLK, 2 * SWA_BLK), F32),
                        pltpu.VMEM((2 * tile, SWA_DIM), F32),
                        pltpu.VMEM((2 * tile, SWA_DIM), F32),
                        pltpu.VMEM((n_pat, tile, SWA_DIM), F32),
                        pltpu.VMEM((n_pat, tile, SWA_DIM), F32),
                        pltpu.VMEM((n_pat, tile, SWA_DIM), F32)],
        compiler_params=_cparams(("parallel", "arbitrary")),
        name="swa",
    )(rel_bias, bkt, proj, proj, proj, proj, proj)


PRM_W0, PRM_A0, PRM_KK, PRM_KA, PRM_RK, PRM_LNW, PRM_LNB, PRM_MUR, PRM_MUK, PRM_MUV = range(10)
PRM_ROWS = 16


LORA_IN_WIDTH = DECAY_LORA + ICL_LORA + GATE_LORA_PAD


def _rwkv_lora_in_body(wd_ref, ad_ref, gd_ref, mul_ref, mug_ref, o_ref, xb, xg, *, tile):
    @pl.when(pl.program_id(0) == 0)
    def _reset():
        xb[:, 0:8, :] = jnp.zeros((2, 8, LANE), F32)
        xg[0:8, :] = jnp.zeros((8, GATE_LORA_PAD), F32)

    def shifted(buf, src, mu):
        buf[8:8 + tile, :] = src[...]
        cur = buf[8:8 + tile, :]
        prev = buf[7:7 + tile, :]
        buf[0:8, :] = buf[tile:tile + 8, :]
        return cur + mu * (prev - cur)

    wd = shifted(xb.at[0], wd_ref, mul_ref[0:1, :])
    ad = shifted(xb.at[1], ad_ref, mul_ref[1:2, :])
    gd = shifted(xg, gd_ref, mug_ref[...])
    o_ref[:, 0:DECAY_LORA] = jnp.tanh(wd).astype(o_ref.dtype)
    o_ref[:, DECAY_LORA:DECAY_LORA + ICL_LORA] = ad.astype(o_ref.dtype)
    o_ref[:, DECAY_LORA + ICL_LORA:] = _sigmoid(gd).astype(o_ref.dtype)


def _rwkv_lora_in(proj, mu_lora, mu_gate, tile=1024):
    s = proj.shape[0]
    fixed = lambda col: pl.BlockSpec((tile, LANE), lambda t: (t, col))
    return pl.pallas_call(
        functools.partial(_rwkv_lora_in_body, tile=tile),
        grid=(s // tile,),
        in_specs=[fixed(COL_WD), fixed(COL_AD),
                  pl.BlockSpec((tile, GATE_LORA_PAD), lambda t: (t, COL_GD * LANE // GATE_LORA_PAD)),
                  pl.BlockSpec((8, LANE), lambda t: (0, 0)),
                  pl.BlockSpec((1, GATE_LORA_PAD), lambda t: (0, 0))],
        out_specs=pl.BlockSpec((tile, LORA_IN_WIDTH), lambda t: (t, 0)),
        out_shape=jax.ShapeDtypeStruct((s, LORA_IN_WIDTH), BF16),
        scratch_shapes=[pltpu.VMEM((2, tile + 8, LANE), F32),
                        pltpu.VMEM((tile + 8, GATE_LORA_PAD), F32)],
        compiler_params=_cparams(("arbitrary",)),
        name="rwkv_lora_in",
    )(proj, proj, proj, mu_lora, mu_gate)


def _rwkv_body(r_ref, k_ref, v_ref, lora_ref, prm_ref, wup_ref, aup_ref, gup_ref, o_ref, xb, state_ref,
               *, tile):
    c = RWKV_CHUNK
    nd = RWKV_DIM
    t = pl.program_id(1)

    @pl.when(t == 0)
    def _reset():
        state_ref[...] = jnp.zeros_like(state_ref)
        xb[:, 0:8, :] = jnp.zeros((3, 8, RWKV_GROUP * LANE), F32)

    prm = prm_ref[...]
    row = lambda i: prm[i:i + 1, :]

    def shifted(idx, src, mu):
        xb[idx, 8:8 + tile, :] = src[...]
        cur = xb[idx, 8:8 + tile, :]
        prev = xb[idx, 7:7 + tile, :]
        xb[idx, 0:8, :] = xb[idx, tile:tile + 8, :]
        return cur + mu * (prev - cur)

    r = shifted(0, r_ref, row(PRM_MUR))
    k = shifted(1, k_ref, row(PRM_MUK))
    v = shifted(2, v_ref, row(PRM_MUV))

    lora = lora_ref[...]
    w_log = -_softplus(-(row(PRM_W0) + _dot(lora[:, 0:DECAY_LORA], wup_ref[...]))) - 0.5
    log_w = -jnp.exp(w_log)
    a_gate = _sigmoid(row(PRM_A0) + _dot(lora[:, DECAY_LORA:DECAY_LORA + ICL_LORA], aup_ref[...]))
    g_gate = _dot(lora[:, DECAY_LORA + ICL_LORA:], gup_ref[...])
    kx = k * row(PRM_KK)
    k_mod = k * (1.0 + (a_gate - 1.0) * row(PRM_KA))

    ri = lax.broadcasted_iota(jnp.int32, (c, c), 0)
    ci = lax.broadcasted_iota(jnp.int32, (c, c), 1)
    lower = jnp.where(ci <= ri, 1.0, 0.0).astype(F32)
    cum = jnp.concatenate([_dot(lower, log_w[n * c:(n + 1) * c], mode=MODE_CUMSUM)
                           for n in range(tile // c)], axis=0)

    r2 = lax.broadcasted_iota(jnp.int32, (2 * c, 2 * c), 0)
    c2 = lax.broadcasted_iota(jnp.int32, (2 * c, 2 * c), 1)
    col_in = jnp.where(c2 < c, c2, c2 - c)
    band = col_in <= jnp.where(r2 < c, r2 - 1, r2 - c)

    width = RWKV_GROUP * LANE
    n_heads = width // nd
    n_chunks = tile // c
    lane = lax.broadcasted_iota(jnp.int32, (1, LANE), 1)
    head_masks = [(lane // nd) == hh for hh in range(LANE // nd)]

    def head_sum(x):
        blocks = []
        for b in range(RWKV_GROUP):
            xs = x[:, b * LANE:(b + 1) * LANE]
            out = jnp.zeros_like(xs)
            for msk in head_masks:
                out = jnp.where(msk, jnp.sum(jnp.where(msk, xs, 0.0), axis=-1, keepdims=True), out)
            blocks.append(out)
        return blocks[0] if RWKV_GROUP == 1 else jnp.concatenate(blocks, axis=1)

    kk = kx * lax.rsqrt(head_sum(kx * kx) + 1e-12)
    b_vec = kk * a_gate
    c_last = jnp.concatenate(
        [jnp.broadcast_to(cum[n * c + c - 1:n * c + c, :], (c, width)) for n in range(n_chunks)], axis=0)
    p_out = jnp.exp(-cum)
    a_t = -kk * jnp.exp(cum - log_w)
    r_t = r * jnp.exp(cum)
    k_t = k_mod * p_out
    b_t = b_vec * p_out
    decay_end = jnp.exp(c_last - cum)
    k_e = k_mod * decay_end
    b_e = b_vec * decay_end
    decay_all = jnp.exp(c_last)

    units = []
    for hh in range(n_heads):
        lanes = slice(hh * nd, (hh + 1) * nd)
        for n in range(n_chunks):
            rows = slice(n * c, (n + 1) * c)
            units.append(dict(hh=hh, n=n, a=a_t[rows, lanes], r=r_t[rows, lanes], v=v[rows, lanes],
                              ar=jnp.concatenate([a_t[rows, lanes], r_t[rows, lanes]], axis=0),
                              bk=jnp.concatenate([b_t[rows, lanes], k_t[rows, lanes]], axis=0),
                              bk_e=jnp.concatenate([b_e[rows, lanes], k_e[rows, lanes]], axis=0),
                              decay=decay_all[n * c:n * c + 1, lanes]))
    states = [state_ref[hh] for hh in range(n_heads)]
    ys = [[None] * n_chunks for _ in range(n_heads)]

    def prepare(group):
        for un in group:
            un["mm"] = jnp.where(band, _dot(un["ar"], un["bk"], _NT), 0.0)
        yield
        t_invs = []
        yield from _unit_lower_inverses([un["mm"][0:c, 0:c] for un in group], c, t_invs)
        for un in group:
            un["mv"] = _dot(un["mm"][0:c, c:2 * c], un["v"], mode=MODE_SOLVE)
        yield
        for un, t_inv in zip(group, t_invs):
            un["wu"] = _dot(t_inv, jnp.concatenate([un["a"], un["mv"]], axis=1), mode=MODE_SOLVE)
        yield
        for un in group:
            un["trans"] = _dot(un["wu"][:, 0:nd], un["bk_e"][0:c], _TN, mode=MODE_STATE)
            un["const"] = _dot(jnp.concatenate([un["wu"][:, nd:2 * nd], un["v"]], axis=0), un["bk_e"],
                               _TN, mode=MODE_STATE)
        yield

    def chain(group):
        for n in sorted({un["n"] for un in group}):
            for un in group:
                if un["n"] == n:
                    hh = un["hh"]
                    un["s0"] = states[hh]
                    states[hh] = (states[hh] * un["decay"]
                                  + _dot(states[hh], un["trans"], mode=MODE_STATE) + un["const"])
            yield

    def outputs(group, chunks_per_stage):
        chunk_ids = sorted({un["n"] for un in group})
        for i in range(0, len(chunk_ids), chunks_per_stage):
            part = [un for un in group if un["n"] in chunk_ids[i:i + chunks_per_stage]]
            for un in part:
                un["sr"] = _dot(jnp.concatenate([un["wu"][:, 0:nd], un["r"]], axis=0), un["s0"], _NT,
                                mode=MODE_STATE)
            yield
            for un in part:
                u = un["sr"][0:c] + un["wu"][:, nd:2 * nd]
                uv = jnp.concatenate([u, un["v"]], axis=0)
                ys[un["hh"]][un["n"]] = un["sr"][c:2 * c] + _dot(un["mm"][c:2 * c], uv, mode=MODE_STATE)
            yield

    _alternate(prepare(units))
    _alternate(chain(units))
    _alternate(outputs(units, n_chunks))
    for hh in range(n_heads):
        state_ref[hh] = states[hh]
    y = jnp.concatenate([jnp.concatenate(ys[hh], axis=0) for hh in range(n_heads)], axis=1)
    mean = head_sum(y) * (1.0 / nd)
    var = head_sum(jnp.square(y - mean)) * (1.0 / nd)
    y_n = (y - mean) * lax.rsqrt(var + GN_EPS) * row(PRM_LNW) + row(PRM_LNB)
    bonus = head_sum(r * k_mod * row(PRM_RK)) * v
    o_ref[...] = ((y_n + bonus) * g_gate).astype(o_ref.dtype)


def _rwkv(proj, prm, mu_lora, mu_gate, w_up, a_up, g_up, tile=512):
    s = proj.shape[0]
    width = RWKV_GROUP * LANE
    groups = RWKV_WIDTH // width
    lora_in = _rwkv_lora_in(proj, mu_lora, mu_gate)
    blk = lambda col: pl.BlockSpec((tile, width), lambda h, t: (t, col // RWKV_GROUP + h))
    return pl.pallas_call(
        functools.partial(_rwkv_body, tile=tile),
        grid=(groups, s // tile),
        in_specs=[blk(COL_R), blk(COL_K), blk(COL_V),
                  pl.BlockSpec((tile, LORA_IN_WIDTH), lambda h, t: (t, 0)),
                  pl.BlockSpec((PRM_ROWS, width), lambda h, t: (0, h)),
                  pl.BlockSpec((DECAY_LORA, width), lambda h, t: (0, h)),
                  pl.BlockSpec((ICL_LORA, width), lambda h, t: (0, h)),
                  pl.BlockSpec((GATE_LORA_PAD, width), lambda h, t: (0, h))],
        out_specs=pl.BlockSpec((tile, width), lambda h, t: (t, h)),
        out_shape=jax.ShapeDtypeStruct((s, RWKV_WIDTH), BF16),
        scratch_shapes=[pltpu.VMEM((3, tile + 8, width), F32),
                        pltpu.VMEM((width // RWKV_DIM, RWKV_DIM, RWKV_DIM), F32)],
        compiler_params=_cparams(("parallel", "arbitrary")),
        name="rwkv7",
    )(proj, proj, proj, lora_in, prm, w_up, a_up, g_up)


def _layout_rwkv_params(mu, w0, a0, k_k, k_a, r_k, ln_w, ln_b):
    rw = RWKV_WIDTH
    mu_r, mu_k, mu_v = mu[0:rw], mu[rw:2 * rw], mu[2 * rw:3 * rw]
    o = 3 * rw
    mu_wd = mu[o:o + DECAY_LORA]; o += DECAY_LORA
    mu_ad = mu[o:o + ICL_LORA]; o += ICL_LORA
    mu_gd = mu[o:o + GATE_LORA]
    rows = [w0, a0, k_k, k_a, r_k.reshape(rw), ln_w, ln_b, mu_r, mu_k, mu_v]
    prm = jnp.stack(rows + [jnp.zeros((rw,), F32)] * (PRM_ROWS - len(rows))).astype(F32)
    mu_lora = jnp.stack([mu_wd, mu_ad] + [jnp.zeros((LANE,), F32)] * 6).astype(F32)
    mu_gate = jnp.concatenate([mu_gd, jnp.zeros((GATE_LORA_PAD - GATE_LORA,), F32)]).reshape(1, GATE_LORA_PAD)
    return prm, mu_lora, mu_gate


def kernel(x, attn_norm, w_in, gdn_conv, gdn_a_log, gdn_dt_bias, gdn_norm, rwkv_mu, rwkv_w0, rwkv_w_up, rwkv_a0, rwkv_a_up, rwkv_g_up, rwkv_k_k, rwkv_k_a, rwkv_r_k, rwkv_ln_w, rwkv_ln_b, w_out, ffn_norm, w_ffn_gate, w_ffn_up, ffn_conv, ffn_conv_b, w_ffn_down, rel_bias, final_norm):
    batch, seq, d = x.shape
    depth = w_in.shape[0]
    outs = []
    w_in_bf16 = w_in.astype(BF16)
    rows_in = [x.reshape(seq, d)] if batch == 1 else [x[b] for b in range(batch)]
    for xb in rows_in:
        for l in range(depth):
            proj = _in_proj(xb, attn_norm[l], _layout_w_in_pallas(w_in_bf16, l))
            o_a = _gdn(proj, gdn_conv[l], gdn_a_log[l], gdn_dt_bias[l], gdn_norm[l])
            o_b = _swa(proj, rel_bias)
            prm, mu_lora, mu_gate = _layout_rwkv_params(
                rwkv_mu[l], rwkv_w0[l], rwkv_a0[l], rwkv_k_k[l], rwkv_k_a[l], rwkv_r_k[l],
                rwkv_ln_w[l], rwkv_ln_b[l])
            g_up = jnp.concatenate(
                [rwkv_g_up[l], jnp.zeros((GATE_LORA_PAD - GATE_LORA, RWKV_WIDTH), F32)], axis=0)
            o_c = _rwkv(proj, prm, mu_lora, mu_gate, rwkv_w_up[l].astype(BF16),
                        rwkv_a_up[l].astype(BF16), g_up.astype(BF16))
            wo = _to_bf16(w_out, l)
            xb = _out_proj(xb, o_a, o_b, o_c, wo[0:GDN_WIDTH], wo[GDN_WIDTH:GDN_WIDTH + SWA_WIDTH],
                           wo[GDN_WIDTH + SWA_WIDTH:])
            act = _ffn_up(xb, ffn_norm[l], _to_bf16(w_ffn_gate, l), _to_bf16(w_ffn_up, l), ffn_conv[l],
                          ffn_conv_b[l])
            xb = _ffn_down(xb, act, _to_bf16(w_ffn_down, l))
        outs.append(_rmsnorm(xb, final_norm, x.dtype))
    return outs[0].reshape(1, seq, d) if batch == 1 else jnp.stack(outs)
```
